```python
import math
import jax, jax.numpy as jnp
from jax import lax
import numpy as np

D_MODEL = 1024
BATCH = 16
SEQ = 2048
DEPTH = 4

GRID_W = 64
CTX_LEN = 256
HEAD_DIM = 64
ROPE_BASE = 10000.0
EPS = 1e-6
Q_BLOCK = 128
N_MOD = 6
D_FF = 4 * D_MODEL
N_EVEN = (DEPTH + 1) // 2
N_ODD = DEPTH // 2

GQA_Q_HEADS = 12
GQA_KV_HEADS = 4
GQA_GROUP = GQA_Q_HEADS // GQA_KV_HEADS
GQA_Q_W = GQA_Q_HEADS * HEAD_DIM
GQA_KV_W = GQA_KV_HEADS * HEAD_DIM
SSM_WIDTH = 256
SSM_GROUP = 16
SSM_GROUPS = SSM_WIDTH // SSM_GROUP
SSM_STATE = 64
SSM_DT_MIN = 0.001
SSM_DT_MAX = 0.1
EVEN_IN_W = GQA_Q_W + 2 * GQA_KV_W + SSM_WIDTH
EVEN_MIX_W = GQA_Q_W + SSM_WIDTH

MLA_HEADS = 8
MLA_Q_RANK = 512
MLA_KV_RANK = 256
MLA_NOPE = 64
MLA_ROPE = 32
MLA_QK = MLA_NOPE + MLA_ROPE
MLA_V = 64
NA_HEADS = 8
NA_W = NA_HEADS * HEAD_DIM
NA_WIN_R = 8
NA_WIN_C = 16
ODD_IN_W = MLA_Q_RANK + MLA_KV_RANK + MLA_ROPE + 3 * NA_W
ODD_MIX_W = MLA_HEADS * MLA_V + NA_W

kernel_name = 'hybrid_diffusion_gqa_s5_mla_natten'


def rms_norm(x, g):
    xf = x.astype(jnp.float32)
    y = xf * lax.rsqrt(jnp.mean(jnp.square(xf), axis=-1, keepdims=True) + EPS)
    return (y * g.astype(jnp.float32)).astype(x.dtype)


def modulate(x, g, shift, scale):
    return rms_norm(x, g) * (1 + scale) + shift


def axial_rope_tables(n_tokens, rot_dim):
    t = jnp.arange(n_tokens)
    rows = (t // GRID_W).astype(jnp.float32)
    cols = (t % GRID_W).astype(jnp.float32)
    axis_dim = rot_dim // 2
    freqs = ROPE_BASE ** (-jnp.arange(0, axis_dim, 2, dtype=jnp.float32) / axis_dim)
    ang_r = rows[:, None] * freqs
    ang_c = cols[:, None] * freqs
    ang = jnp.concatenate([ang_r, ang_r, ang_c, ang_c], axis=-1)
    return jnp.cos(ang), jnp.sin(ang)


def apply_axial_rope(x, cos, sin):
    xf = x.astype(jnp.float32)
    x1, x2, x3, x4 = jnp.split(xf, 4, axis=-1)
    rot = jnp.concatenate([-x2, x1, -x4, x3], axis=-1)
    return (xf * cos[:, None, :] + rot * sin[:, None, :]).astype(x.dtype)


def blocked_attention(q, k, v, scale):
    b, n = q.shape[:2]
    nb = n // Q_BLOCK
    qb = jnp.swapaxes(q.reshape((b, nb, Q_BLOCK) + q.shape[2:]), 0, 1)

    def one_block(qblk):
        s = jnp.einsum('bqkgd,bskd->bkgqs', qblk, k, preferred_element_type=jnp.float32) * scale
        p = jax.nn.softmax(s, axis=-1).astype(v.dtype)
        return jnp.einsum('bkgqs,bskd->bqkgd', p, v)

    out = lax.map(one_block, qb)
    return jnp.swapaxes(out, 0, 1).reshape((b, n) + out.shape[3:])


def s5_discretize(lam_re, lam_im, log_dt, b_re, b_im):
    f32 = jnp.float32
    lam_re, lam_im = lam_re.astype(f32), lam_im.astype(f32)
    dt = jnp.exp(log_dt.astype(f32))[:, None]
    mag = jnp.exp(lam_re * dt)
    a_re = mag * jnp.cos(lam_im * dt)
    a_im = mag * jnp.sin(lam_im * dt)
    den = jnp.square(lam_re) + jnp.square(lam_im)
    f_re = ((a_re - 1.0) * lam_re + a_im * lam_im) / den
    f_im = (a_im * lam_re - (a_re - 1.0) * lam_im) / den
    b_re, b_im = b_re.astype(f32), b_im.astype(f32)
    bb_re = f_re[..., None] * b_re - f_im[..., None] * b_im
    bb_im = f_re[..., None] * b_im + f_im[..., None] * b_re
    return a_re, a_im, bb_re, bb_im


def _complex_affine_combine(e1, e2):
    a1r, a1i, b1r, b1i = e1
    a2r, a2i, b2r, b2i = e2
    return (a2r * a1r - a2i * a1i, a2r * a1i + a2i * a1r,
            a2r * b1r - a2i * b1i + b2r, a2r * b1i + a2i * b1r + b2i)


def s5_scan(u, a_re, a_im, bb_re, bb_im, h0=None):
    n = u.shape[1]
    bu_re = jnp.einsum('bngp,gsp->bngs', u, bb_re)
    bu_im = jnp.einsum('bngp,gsp->bngs', u, bb_im)
    shape = (1, n) + a_re.shape
    ar = jnp.broadcast_to(a_re, shape)
    ai = jnp.broadcast_to(a_im, shape)
    p_re, p_im, h_re, h_im = lax.associative_scan(_complex_affine_combine, (ar, ai, bu_re, bu_im), axis=1)
    if h0 is not None:
        h0_re, h0_im = h0[0][:, None], h0[1][:, None]
        h_re = h_re + p_re * h0_re - p_im * h0_im
        h_im = h_im + p_re * h0_im + p_im * h0_re
    return h_re, h_im


def s5_readout(h_re, h_im, c_re, c_im):
    return jnp.einsum('bngs,gps->bngp', h_re, c_re) - jnp.einsum('bngs,gps->bngp', h_im, c_im)


def maybe_flip(t, d):
    return t[:, ::-1] if d == 1 else t


def s5_mixer(u_lat, u_ctx, lam_re, lam_im, log_dt, b_re, b_im, c_re, c_im, d_skip, w_glu, b_glu, need_ctx):
    f32 = jnp.float32
    out_dtype = u_lat.dtype

    def grouped(u):
        return u.astype(f32).reshape(u.shape[:2] + (SSM_GROUPS, SSM_GROUP))

    ul, uc = grouped(u_lat), grouped(u_ctx)
    d_g = d_skip.astype(f32).reshape(SSM_GROUPS, SSM_GROUP)
    y_lat = d_g * ul
    y_ctx = d_g * uc if need_ctx else None
    for d in range(2):
        a_re, a_im, bb_re, bb_im = s5_discretize(lam_re[d], lam_im[d], log_dt[d], b_re[d], b_im[d])
        cr, ci = c_re[d].astype(f32), c_im[d].astype(f32)
        hc_re, hc_im = s5_scan(maybe_flip(uc, d), a_re, a_im, bb_re, bb_im)
        hl_re, hl_im = s5_scan(maybe_flip(ul, d), a_re, a_im, bb_re, bb_im, h0=(hc_re[:, -1], hc_im[:, -1]))
        y_lat = y_lat + maybe_flip(s5_readout(hl_re, hl_im, cr, ci), d)
        if need_ctx:
            y_ctx = y_ctx + maybe_flip(s5_readout(hc_re, hc_im, cr, ci), d)
    wg, bg = w_glu.astype(f32), b_glu.astype(f32)

    def glu(y):
        y = jax.nn.gelu(y.reshape(y.shape[:2] + (SSM_WIDTH,)))
        return (y * jax.nn.sigmoid(y @ wg + bg)).astype(out_dtype)

    return glu(y_lat), (glu(y_ctx) if need_ctx else None)


def even_mixer(h_lat, h_ctx, w_in, w_out, g_q, g_k, lam_re, lam_im, log_dt, b_re, b_im, c_re, c_im,
               d_skip, w_glu, b_glu, need_ctx):
    b, n, _ = h_lat.shape
    n_ctx = h_ctx.shape[1]
    cos, sin = axial_rope_tables(n, HEAD_DIM)
    cuts = [GQA_Q_W, GQA_Q_W + GQA_KV_W, GQA_Q_W + 2 * GQA_KV_W]

    def project(h):
        t = h.shape[:2]
        q, k, v, u = jnp.split(h @ w_in, cuts, axis=-1)
        q = rms_norm(q.reshape(t + (GQA_Q_HEADS, HEAD_DIM)), g_q)
        k = rms_norm(k.reshape(t + (GQA_KV_HEADS, HEAD_DIM)), g_k)
        v = v.reshape(t + (GQA_KV_HEADS, HEAD_DIM))
        return q, k, v, u

    def grouped_q(q):
        return q.reshape(q.shape[:2] + (GQA_KV_HEADS, GQA_GROUP, HEAD_DIM))

    q_l, k_l, v_l, u_l = project(h_lat)
    q_c, k_c, v_c, u_c = project(h_ctx)
    q_l = apply_axial_rope(q_l, cos, sin)
    k_l = apply_axial_rope(k_l, cos, sin)
    scale = HEAD_DIM ** -0.5
    att_l = blocked_attention(grouped_q(q_l), jnp.concatenate([k_c, k_l], axis=1),
                              jnp.concatenate([v_c, v_l], axis=1), scale).reshape(b, n, GQA_Q_W)
    ssm_l, ssm_c = s5_mixer(u_l, u_c, lam_re, lam_im, log_dt, b_re, b_im, c_re, c_im, d_skip, w_glu, b_glu, need_ctx)
    out_l = jnp.concatenate([att_l, ssm_l], axis=-1) @ w_out
    out_c = None
    if need_ctx:
        att_c = blocked_attention(grouped_q(q_c), k_c, v_c, scale).reshape(b, n_ctx, GQA_Q_W)
        out_c = jnp.concatenate([att_c, ssm_c], axis=-1) @ w_out
    return out_l, out_c


def neighbourhood_attention(q, k, v, k_ctx, v_ctx, rpb, rows):
    b, n, h, dh = q.shape
    wr = min(NA_WIN_R, rows)
    n_loc = wr * NA_WIN_C
    scale = dh ** -0.5
    kg = k.reshape(b, rows, GRID_W, h, dh)
    vg = v.reshape(b, rows, GRID_W, h, dh)
    qg = jnp.swapaxes(q.reshape(b, rows, GRID_W, h, dh), 0, 1)
    row_start = jnp.clip(jnp.arange(rows) - wr // 2, 0, rows - wr)
    q_col = jnp.arange(GRID_W)
    col_idx = jnp.clip(q_col - NA_WIN_C // 2, 0, GRID_W - NA_WIN_C)[:, None] + jnp.arange(NA_WIN_C)
    col_bias = rpb[:, :, col_idx - q_col[:, None] + (NA_WIN_C - 1)]

    def one_row(args):
        r, q_row = args
        rs = row_start[r]
        kw = lax.dynamic_slice_in_dim(kg, rs, wr, axis=1)[:, :, col_idx]
        vw = lax.dynamic_slice_in_dim(vg, rs, wr, axis=1)[:, :, col_idx]
        bias = jnp.transpose(col_bias[:, rs + jnp.arange(wr) - r + (NA_WIN_R - 1)], (0, 2, 1, 3))
        s_loc = jnp.einsum('bqhd,bjqchd->bhqjc', q_row, kw, preferred_element_type=jnp.float32) * scale + bias
        s_ctx = jnp.einsum('bqhd,bshd->bhqs', q_row, k_ctx, preferred_element_type=jnp.float32) * scale
        s = jnp.concatenate([s_loc.reshape(b, h, GRID_W, n_loc), s_ctx], axis=-1)
        p = jax.nn.softmax(s, axis=-1).astype(v.dtype)
        p_loc = p[..., :n_loc].reshape(b, h, GRID_W, wr, NA_WIN_C)
        return (jnp.einsum('bhqjc,bjqchd->bqhd', p_loc, vw)
                + jnp.einsum('bhqs,bshd->bqhd', p[..., n_loc:], v_ctx))

    out = lax.map(one_row, (jnp.arange(rows), qg))
    return jnp.swapaxes(out, 0, 1).reshape(b, n, h * dh)


def odd_mixer(h_lat, h_ctx, w_in, w_out, g_cq, g_ckv, w_uq, w_ukv, g_mq, g_mk, g_nq, g_nk, rpb, need_ctx):
    b, n, _ = h_lat.shape
    n_ctx = h_ctx.shape[1]
    rows = n // GRID_W
    cos, sin = axial_rope_tables(n, MLA_ROPE)
    c1 = MLA_Q_RANK
    c2 = c1 + MLA_KV_RANK
    c3 = c2 + MLA_ROPE
    cuts = [c1, c2, c3, c3 + NA_W, c3 + 2 * NA_W]

    def project(h):
        t = h.shape[:2]
        cq, ckv, kr, nq, nk, nv = jnp.split(h @ w_in, cuts, axis=-1)
        q = (rms_norm(cq, g_cq) @ w_uq).reshape(t + (MLA_HEADS, MLA_QK))
        kv = (rms_norm(ckv, g_ckv) @ w_ukv).reshape(t + (MLA_HEADS, MLA_NOPE + MLA_V))
        k = jnp.concatenate([kv[..., :MLA_NOPE],
                             jnp.broadcast_to(kr[:, :, None, :], t + (MLA_HEADS, MLA_ROPE))], axis=-1)
        mla = (rms_norm(q, g_mq), rms_norm(k, g_mk), kv[..., MLA_NOPE:])
        na = (rms_norm(nq.reshape(t + (NA_HEADS, HEAD_DIM)), g_nq),
              rms_norm(nk.reshape(t + (NA_HEADS, HEAD_DIM)), g_nk),
              nv.reshape(t + (NA_HEADS, HEAD_DIM)))
        return mla, na

    def rope_tail(t):
        return jnp.concatenate([t[..., :MLA_NOPE], apply_axial_rope(t[..., MLA_NOPE:], cos, sin)], axis=-1)

    (mq_l, mk_l, mv_l), (nq_l, nk_l, nv_l) = project(h_lat)
    (mq_c, mk_c, mv_c), (nq_c, nk_c, nv_c) = project(h_ctx)
    mq_l, mk_l = rope_tail(mq_l), rope_tail(mk_l)
    mla_scale = MLA_QK ** -0.5
    mla_l = blocked_attention(mq_l[:, :, :, None], jnp.concatenate([mk_c, mk_l], axis=1),
                              jnp.concatenate([mv_c, mv_l], axis=1), mla_scale).reshape(b, n, MLA_HEADS * MLA_V)
    na_l = neighbourhood_attention(nq_l, nk_l, nv_l, nk_c, nv_c, rpb, rows)
    out_l = jnp.concatenate([mla_l, na_l], axis=-1) @ w_out
    out_c = None
    if need_ctx:
        mla_c = blocked_attention(mq_c[:, :, :, None], mk_c, mv_c, mla_scale).reshape(b, n_ctx, MLA_HEADS * MLA_V)
        na_c = blocked_attention(nq_c[:, :, :, None], nk_c, nv_c, HEAD_DIM ** -0.5).reshape(b, n_ctx, NA_W)
        out_c = jnp.concatenate([mla_c, na_c], axis=-1) @ w_out
    return out_l, out_c


def sq_relu_mlp(h, w1, w2):
    return jnp.square(jax.nn.relu(h @ w1)) @ w2


def setup_inputs(seed: int = 0) -> dict:
    key = jax.random.key(seed)
    keys = iter(jax.random.split(key, 40))
    f32 = jnp.float32

    def normal(shape, scale):
        return scale * jax.random.normal(next(keys), shape, f32)

    def gain(shape):
        return 1.0 + 0.01 * jax.random.normal(next(keys), shape, f32)

    ne, no = N_EVEN, N_ODD
    G, N, P = SSM_GROUPS, SSM_STATE, SSM_GROUP
    return {
        'x': normal((BATCH, SEQ, D_MODEL), 1.0),
        'c': normal((BATCH, D_MODEL), 1.0),
        'ctx': normal((BATCH, CTX_LEN, D_MODEL), 1.0),
        'c_ctx': normal((D_MODEL,), 1.0),
        'w_mod': normal((DEPTH, D_MODEL, N_MOD * D_MODEL), D_MODEL ** -0.5),
        'b_mod': normal((DEPTH, N_MOD * D_MODEL), 0.01),
        'g_norm1': gain((DEPTH, D_MODEL)),
        'g_norm2': gain((DEPTH, D_MODEL)),
        'w_ff1': normal((DEPTH, D_MODEL, D_FF), D_MODEL ** -0.5),
        'w_ff2': normal((DEPTH, D_FF, D_MODEL), D_FF ** -0.5),
        'e_w_in': normal((ne, D_MODEL, EVEN_IN_W), D_MODEL ** -0.5),
        'e_w_out': normal((ne, EVEN_MIX_W, D_MODEL), EVEN_MIX_W ** -0.5),
        'e_g_q': gain((ne, HEAD_DIM)),
        'e_g_k': gain((ne, HEAD_DIM)),
        'ssm_lam_re': -0.5 + normal((ne, 2, G, N), 0.01),
        'ssm_lam_im': jnp.pi * jnp.arange(N, dtype=f32) + normal((ne, 2, G, N), 0.01),
        'ssm_log_dt': jax.random.uniform(next(keys), (ne, 2, G), f32, math.log(SSM_DT_MIN), math.log(SSM_DT_MAX)),
        'ssm_b_re': normal((ne, 2, G, N, P), (2 * P) ** -0.5),
        'ssm_b_im': normal((ne, 2, G, N, P), (2 * P) ** -0.5),
        'ssm_c_re': normal((ne, 2, G, P, N), 0.5),
        'ssm_c_im': normal((ne, 2, G, P, N), 0.5),
        'ssm_d': normal((ne, SSM_WIDTH), 0.5),
        'ssm_w_glu': normal((ne, SSM_WIDTH, SSM_WIDTH), SSM_WIDTH ** -0.5),
        'ssm_b_glu': normal((ne, SSM_WIDTH), 0.01),
        'o_w_in': normal((no, D_MODEL, ODD_IN_W), D_MODEL ** -0.5),
        'o_w_out': normal((no, ODD_MIX_W, D_MODEL), ODD_MIX_W ** -0.5),
        'mla_g_cq': gain((no, MLA_Q_RANK)),
        'mla_g_ckv': gain((no, MLA_KV_RANK)),
        'mla_w_uq': normal((no, MLA_Q_RANK, MLA_HEADS * MLA_QK), MLA_Q_RANK ** -0.5),
        'mla_w_ukv': normal((no, MLA_KV_RANK, MLA_HEADS * (MLA_NOPE + MLA_V)), MLA_KV_RANK ** -0.5),
        'mla_g_q': gain((no, MLA_QK)),
        'mla_g_k': gain((no, MLA_QK)),
        'na_g_q': gain((no, HEAD_DIM)),
        'na_g_k': gain((no, HEAD_DIM)),
        'na_rpb': normal((no, NA_HEADS, 2 * NA_WIN_R - 1, 2 * NA_WIN_C - 1), 0.1),
    }


def reference(x, c, ctx, c_ctx, w_mod, b_mod, g_norm1, g_norm2, w_ff1, w_ff2,
              e_w_in, e_w_out, e_g_q, e_g_k, ssm_lam_re, ssm_lam_im, ssm_log_dt, ssm_b_re, ssm_b_im,
              ssm_c_re, ssm_c_im, ssm_d, ssm_w_glu, ssm_b_glu,
              o_w_in, o_w_out, mla_g_cq, mla_g_ckv, mla_w_uq, mla_w_ukv, mla_g_q, mla_g_k,
              na_g_q, na_g_k, na_rpb):
    cond_lat = jax.nn.silu(c)[:, None, :]
    cond_ctx = jax.nn.silu(c_ctx)[None, None, :]
    xc = ctx
    for i in range(DEPTH):
        need_ctx = i < DEPTH - 1
        j = i // 2
        m_lat = jnp.split(cond_lat @ w_mod[i] + b_mod[i], N_MOD, axis=-1)
        m_ctx = jnp.split(cond_ctx @ w_mod[i] + b_mod[i], N_MOD, axis=-1)
        a_lat = modulate(x, g_norm1[i], m_lat[0], m_lat[1])
        a_ctx = modulate(xc, g_norm1[i], m_ctx[0], m_ctx[1])
        if i % 2 == 0:
            o_lat, o_ctx = even_mixer(a_lat, a_ctx, e_w_in[j], e_w_out[j], e_g_q[j], e_g_k[j],
                                      ssm_lam_re[j], ssm_lam_im[j], ssm_log_dt[j], ssm_b_re[j], ssm_b_im[j],
                                      ssm_c_re[j], ssm_c_im[j], ssm_d[j], ssm_w_glu[j], ssm_b_glu[j], need_ctx)
        else:
            o_lat, o_ctx = odd_mixer(a_lat, a_ctx, o_w_in[j], o_w_out[j], mla_g_cq[j], mla_g_ckv[j],
                                     mla_w_uq[j], mla_w_ukv[j], mla_g_q[j], mla_g_k[j],
                                     na_g_q[j], na_g_k[j], na_rpb[j], need_ctx)
        x = x + m_lat[2] * o_lat
        x = x + m_lat[5] * sq_relu_mlp(modulate(x, g_norm2[i], m_lat[3], m_lat[4]), w_ff1[i], w_ff2[i])
        if need_ctx:
            xc = xc + m_ctx[2] * o_ctx
            xc = xc + m_ctx[5] * sq_relu_mlp(modulate(xc, g_norm2[i], m_ctx[3], m_ctx[4]), w_ff1[i], w_ff2[i])
    return x
```

```python
import functools
import math

import numpy as np
import jax
import jax.numpy as jnp
from jax import lax
from jax.experimental import pallas as pl
from jax.experimental.pallas import tpu as pltpu

F32 = jnp.float32
BF16 = jnp.bfloat16

D_MODEL = 1024
GRID_W = 64
HEAD_DIM = 64
ROPE_BASE = 10000.0
EPS = 1e-6
N_MOD = 6
D_FF = 4 * D_MODEL

GQA_Q_HEADS = 12
GQA_KV_HEADS = 4
GQA_GROUP = GQA_Q_HEADS // GQA_KV_HEADS
GQA_Q_W = GQA_Q_HEADS * HEAD_DIM
GQA_KV_W = GQA_KV_HEADS * HEAD_DIM
SSM_WIDTH = 256
SSM_GROUP = 16
SSM_GROUPS = SSM_WIDTH // SSM_GROUP
SSM_STATE = 64
SSM_CHUNK = 8

MLA_HEADS = 8
MLA_Q_RANK = 512
MLA_KV_RANK = 256
MLA_NOPE = 64
MLA_ROPE = 32
MLA_QK = MLA_NOPE + MLA_ROPE
MLA_V = 64
MLA_PAD = 128
NA_HEADS = 8
NA_W = NA_HEADS * HEAD_DIM
NA_WIN_R = 8
NA_WIN_C = 16
NA_ROWS_PER_BLOCK = 4

TM = 256
MASK_VALUE = -1e30
VMEM_LIMIT = 52 * 1024 * 1024


def _cparams(n_axes):
    return pltpu.CompilerParams(dimension_semantics=("parallel",) * n_axes, vmem_limit_bytes=VMEM_LIMIT)


def _resident(shape):
    nd = len(shape)
    return pl.BlockSpec(shape, lambda *_: (0,) * nd, pipeline_mode=pl.Buffered(1))


def _dot(a, b):
    return jnp.dot(a, b, preferred_element_type=F32)


def _dot_nt(a, b):
    return lax.dot_general(a, b, (((1,), (1,)), ((), ())), preferred_element_type=F32)


def _split_dot(x, w):
    hi = x.astype(BF16)
    lo = (x - hi.astype(F32)).astype(BF16)
    return _dot(hi, w) + _dot(lo, w)


def _modulate(xf, g, shift, scale):
    ms = jnp.mean(xf * xf, axis=-1, keepdims=True)
    return (xf * lax.rsqrt(ms + EPS) * g) * (1.0 + scale) + shift


def _head_rms(x, down, up, head_dim):
    ss = _split_dot(x * x, down)
    inv = lax.rsqrt(ss * (1.0 / head_dim) + EPS)
    return x * _split_dot(inv, up)


def _rope(x, cos, sin_a, sin_b, shift):
    n = x.shape[-1]
    reps = n // cos.shape[-1]
    cos, sin_a, sin_b = (jnp.tile(t, (1, reps)) for t in (cos, sin_a, sin_b))
    return x * cos + pltpu.roll(x, n - shift, 1) * sin_a + pltpu.roll(x, shift, 1) * sin_b


def _mod_kernel(c_ref, w_ref, b_ref, o_ref):
    c = c_ref[...]
    s = (c * jax.nn.sigmoid(c)).astype(BF16)
    o_ref[0] = _dot(s, w_ref[0].astype(BF16)) + b_ref[0]


def _mod_vectors(cond, w_mod, b_mod):
    depth, d, n = w_mod.shape
    rows = cond.shape[0]
    tn = 1536
    return pl.pallas_call(
        _mod_kernel,
        grid=(depth, n // tn),
        in_specs=[pl.BlockSpec((rows, d), lambda l, j: (0, 0)),
                  pl.BlockSpec((1, d, tn), lambda l, j: (l, 0, j)),
                  pl.BlockSpec((1, 1, tn), lambda l, j: (l, 0, j))],
        out_specs=pl.BlockSpec((1, rows, tn), lambda l, j: (l, 0, j)),
        out_shape=jax.ShapeDtypeStruct((depth, rows, n), F32),
        compiler_params=_cparams(2),
        name="mod_vectors",
    )(cond, w_mod, b_mod.reshape(depth, 1, n))


def _even_in_kernel(x_ref, mod_ref, g1_ref, w_ref, down_ref, up_ref, gqk_ref, cos_ref, sa_ref, sb_ref,
                    q_ref, k_ref, v_ref, u_ref):
    m = mod_ref[0]
    a = _modulate(x_ref[...], g1_ref[...], m[:, 0:D_MODEL], m[:, D_MODEL:2 * D_MODEL]).astype(BF16)
    h = _dot(a, w_ref[...])
    nqk = GQA_Q_W + GQA_KV_W
    qk = _head_rms(h[:, :nqk], down_ref[...], up_ref[...], HEAD_DIM) * gqk_ref[...]
    qk = _rope(qk, cos_ref[...], sa_ref[...], sb_ref[...], HEAD_DIM // 4).astype(BF16)
    q_ref[...] = qk[:, :GQA_Q_W]
    k_ref[...] = qk[:, GQA_Q_W:]
    v_ref[...] = h[:, nqk:nqk + GQA_KV_W].astype(BF16)
    u_ref[...] = h[:, nqk + GQA_KV_W:].astype(BF16)


def _even_in_proj(x, modtab, g1, w_in, down, up, gqk, tabs, batch, nblk):
    rows = x.shape[0]
    row_spec = lambda w: pl.BlockSpec((TM, w), lambda b, i: (b * nblk + i, 0))
    tab_spec = pl.BlockSpec((TM, 128), lambda b, i: (i, 0))
    return pl.pallas_call(
        _even_in_kernel,
        grid=(batch, nblk),
        in_specs=[row_spec(D_MODEL),
                  pl.BlockSpec((1, 1, N_MOD * D_MODEL), lambda b, i: (2 * b + jnp.minimum(i, 1), 0, 0)),
                  _resident(g1.shape), _resident(w_in.shape), _resident(down.shape), _resident(up.shape),
                  _resident(gqk.shape), tab_spec, tab_spec, tab_spec],
        out_specs=[row_spec(GQA_Q_W), row_spec(GQA_KV_W), row_spec(GQA_KV_W), row_spec(SSM_WIDTH)],
        out_shape=[jax.ShapeDtypeStruct((rows, GQA_Q_W), BF16), jax.ShapeDtypeStruct((rows, GQA_KV_W), BF16),
                   jax.ShapeDtypeStruct((rows, GQA_KV_W), BF16), jax.ShapeDtypeStruct((rows, SSM_WIDTH), BF16)],
        compiler_params=_cparams(2),
        name="even_in_proj",
    )(x, modtab, g1, w_in, down, up, gqk, *tabs)


def _odd_in_kernel(x_ref, mod_ref, g1_ref, w_ref, gcq_ref, gckv_ref, wuq_ref, wuk_ref, wuv_ref,
                   gmq_ref, gmk_ref, down_ref, up_ref, gnqk_ref, cos_ref, sa_ref, sb_ref,
                   mq_ref, mk_ref, mv_ref, nq_ref, nk_ref, nv_ref):
    m = mod_ref[0]
    a = _modulate(x_ref[...], g1_ref[...], m[:, 0:D_MODEL], m[:, D_MODEL:2 * D_MODEL]).astype(BF16)
    h = _dot(a, w_ref[...])
    c1 = MLA_Q_RANK
    c2 = c1 + MLA_KV_RANK
    c3 = c2 + MLA_PAD

    def rms(t, g):
        return (t * lax.rsqrt(jnp.mean(t * t, axis=-1, keepdims=True) + EPS) * g).astype(BF16)

    q = _dot(rms(h[:, :c1], gcq_ref[...]), wuq_ref[...])
    ckv = rms(h[:, c1:c2], gckv_ref[...])
    k = _dot(ckv, wuk_ref[...]) + jnp.tile(h[:, c2:c3], (1, MLA_HEADS))
    mv_ref[...] = _dot(ckv, wuv_ref[...]).astype(BF16)

    def mla_heads(t, g):
        parts = []
        for hh in range(MLA_HEADS):
            th = t[:, hh * MLA_PAD:(hh + 1) * MLA_PAD]
            ss = jnp.sum(th * th, axis=-1, keepdims=True)
            parts.append(th * lax.rsqrt(ss * (1.0 / MLA_QK) + EPS))
        t = jnp.concatenate(parts, axis=1) * g
        return _rope(t, cos_ref[...], sa_ref[...], sb_ref[...], MLA_ROPE // 4).astype(BF16)

    mq_ref[...] = mla_heads(q, gmq_ref[...])
    mk_ref[...] = mla_heads(k, gmk_ref[...])

    nqk = _head_rms(h[:, c3:c3 + 2 * NA_W], down_ref[...], up_ref[...], HEAD_DIM) * gnqk_ref[...]
    nqk = nqk.astype(BF16)
    nq_ref[...] = nqk[:, :NA_W]
    nk_ref[...] = nqk[:, NA_W:]
    nv_ref[...] = h[:, c3 + 2 * NA_W:].astype(BF16)


def _odd_in_proj(x, modtab, g1, w_in, gcq, gckv, wuq, wuk, wuv, gmq, gmk, down, up, gnqk, tabs, batch, nblk):
    rows = x.shape[0]
    row_spec = lambda w: pl.BlockSpec((TM, w), lambda b, i: (b * nblk + i, 0))
    tab_spec = pl.BlockSpec((TM, 128), lambda b, i: (i, 0))
    consts = (g1, w_in, gcq, gckv, wuq, wuk, wuv, gmq, gmk, down, up, gnqk)
    mla_w = MLA_HEADS * MLA_PAD
    widths = (mla_w, mla_w, MLA_HEADS * MLA_V, NA_W, NA_W, NA_W)
    return pl.pallas_call(
        _odd_in_kernel,
        grid=(batch, nblk),
        in_specs=[row_spec(D_MODEL),
                  pl.BlockSpec((1, 1, N_MOD * D_MODEL), lambda b, i: (2 * b + jnp.minimum(i, 1), 0, 0))]
                 + [_resident(c.shape) for c in consts] + [tab_spec] * 3,
        out_specs=[row_spec(w) for w in widths],
        out_shape=[jax.ShapeDtypeStruct((rows, w), BF16) for w in widths],
        compiler_params=_cparams(2),
        name="odd_in_proj",
    )(x, modtab, *consts, *tabs)


def _attn_kernel(q_ref, k_ref, v_ref, o_ref, *, n_kv, group, dk, dv, n_ctx, tq):
    n_keys = k_ref.shape[1]

    def run(nk):
        outs = [None] * (n_kv * group)
        for kv in range(n_kv):
            kh = k_ref[0, :nk, kv * dk:(kv + 1) * dk]
            vh = v_ref[0, :nk, kv * dv:(kv + 1) * dv]
            qs = [q_ref[0, :, (kv * group + g) * dk:(kv * group + g + 1) * dk] for g in range(group)]
            qg = jnp.concatenate(qs, axis=0) if group > 1 else qs[0]
            s = _dot_nt(qg, kh)
            p = jnp.exp(s - jnp.max(s, axis=-1, keepdims=True))
            l = jnp.sum(p, axis=-1, keepdims=True)
            o = _dot(p.astype(BF16), vh) / l
            for g in range(group):
                outs[kv * group + g] = o[g * tq:(g + 1) * tq]
        o_ref[0] = jnp.concatenate(outs, axis=1).astype(o_ref.dtype)

    is_ctx = pl.program_id(1) < n_ctx // tq

    @pl.when(is_ctx)
    def _():
        run(n_ctx)

    @pl.when(jnp.logical_not(is_ctx))
    def _():
        run(n_keys)


def _attention(q, k, v, *, n_kv, group, dk, dv, n_ctx, tq):
    batch, s, _ = q.shape
    kern = functools.partial(_attn_kernel, n_kv=n_kv, group=group, dk=dk, dv=dv, n_ctx=n_ctx, tq=tq)
    return pl.pallas_call(
        kern,
        grid=(batch, s // tq),
        in_specs=[pl.BlockSpec((1, tq, q.shape[2]), lambda b, i: (b, i, 0)),
                  pl.BlockSpec((1, s, k.shape[2]), lambda b, i: (b, 0, 0)),
                  pl.BlockSpec((1, s, v.shape[2]), lambda b, i: (b, 0, 0))],
        out_specs=pl.BlockSpec((1, tq, n_kv * group * dv), lambda b, i: (b, i, 0)),
        out_shape=jax.ShapeDtypeStruct((batch, s, n_kv * group * dv), BF16),
        compiler_params=_cparams(2),
        name="attention_dk%d" % dk,
    )(q, k, v)


def _na_geometry(rows):
    rb = NA_ROWS_PER_BLOCK
    wr = min(NA_WIN_R, rows)
    span = min(rows, wr + rb - 1)
    starts, variants, keys = [], [], {}
    for r0 in range(0, rows, rb):
        rs0 = int(np.clip(r0 - wr // 2, 0, rows - wr))
        start = min(rs0, rows - span)
        rel = tuple(int(np.clip(r0 + dr - wr // 2, 0, rows - wr)) - (r0 + dr) for dr in range(rb))
        key = (rel, start - r0)
        variants.append(keys.setdefault(key, len(keys)))
        starts.append(start)
    firsts = [variants.index(v) for v in range(len(keys))]
    return wr, span, starts, variants, firsts


def _na_bias_table(rpb, rows):
    rb = NA_ROWS_PER_BLOCK
    wr, span, starts, variants, firsts = _na_geometry(rows)
    idx_r = np.zeros((len(firsts), rb, span), np.int32)
    ok_r = np.zeros((len(firsts), rb, span), bool)
    for v, blk in enumerate(firsts):
        r0, start = blk * rb, starts[blk]
        for dr in range(rb):
            r = r0 + dr
            rs = int(np.clip(r - wr // 2, 0, rows - wr))
            for j in range(span):
                kr = start + j
                ok_r[v, dr, j] = rs <= kr < rs + wr
                idx_r[v, dr, j] = np.clip(kr - r + NA_WIN_R - 1, 0, 2 * NA_WIN_R - 2)
    c = np.arange(GRID_W)
    cs = np.clip(c - NA_WIN_C // 2, 0, GRID_W - NA_WIN_C)
    kc = np.arange(GRID_W)
    ok_c = (kc[None, :] >= cs[:, None]) & (kc[None, :] < cs[:, None] + NA_WIN_C)
    idx_c = np.clip(kc[None, :] - c[:, None] + NA_WIN_C - 1, 0, 2 * NA_WIN_C - 2)
    ir = idx_r[:, :, None, :, None]
    ic = idx_c[None, None, :, None, :]
    bias = rpb[:, ir, ic]
    ok = ok_r[:, :, None, :, None] & ok_c[None, None, :, None, :]
    bias = jnp.where(ok[None], bias, MASK_VALUE)
    bias = jnp.transpose(bias, (1, 0, 2, 3, 4, 5))
    return bias.reshape(len(firsts), rpb.shape[0], rb * GRID_W, span * GRID_W)


def _na_kernel(start_ref, var_ref, q_ref, k_ref, v_ref, bias_ref, o_ref, *, n_ctx, n_loc):
    i = pl.program_id(1)
    tq = q_ref.shape[1]

    def finish(h, s_list, v_list):
        m = s_list[0].max(axis=-1, keepdims=True)
        for s in s_list[1:]:
            m = jnp.maximum(m, s.max(axis=-1, keepdims=True))
        ps = [jnp.exp(s - m) for s in s_list]
        l = sum(p.sum(axis=-1, keepdims=True) for p in ps)
        o = sum(_dot(p.astype(BF16), v) for p, v in zip(ps, v_list))
        return o / l

    def head_slice(ref, rows, h):
        return ref[0, rows, h * HEAD_DIM:(h + 1) * HEAD_DIM]

    @pl.when(i == 0)
    def _():
        outs = []
        for h in range(NA_HEADS):
            qh = head_slice(q_ref, slice(None), h)
            kc = head_slice(k_ref, slice(0, n_ctx), h)
            vc = head_slice(v_ref, slice(0, n_ctx), h)
            outs.append(finish(h, [_dot_nt(qh, kc)], [vc]))
        o_ref[0] = jnp.concatenate(outs, axis=1).astype(o_ref.dtype)

    @pl.when(i > 0)
    def _():
        off = pl.multiple_of(n_ctx + start_ref[i - 1] * GRID_W, GRID_W)
        var = var_ref[i - 1]
        outs = []
        for h in range(NA_HEADS):
            qh = head_slice(q_ref, slice(None), h)
            kc = head_slice(k_ref, slice(0, n_ctx), h)
            vc = head_slice(v_ref, slice(0, n_ctx), h)
            kl = head_slice(k_ref, pl.ds(off, n_loc), h)
            vl = head_slice(v_ref, pl.ds(off, n_loc), h)
            s_loc = _dot_nt(qh, kl) + bias_ref[var, h]
            outs.append(finish(h, [_dot_nt(qh, kc), s_loc], [vc, vl]))
        o_ref[0] = jnp.concatenate(outs, axis=1).astype(o_ref.dtype)


def _neighbourhood_attention(q, k, v, bias, rows, n_ctx):
    batch, s, w = q.shape
    _, span, starts, variants, _ = _na_geometry(rows)
    tq = NA_ROWS_PER_BLOCK * GRID_W
    assert tq == n_ctx
    n_loc = span * GRID_W
    kern = functools.partial(_na_kernel, n_ctx=n_ctx, n_loc=n_loc)
    grid_spec = pltpu.PrefetchScalarGridSpec(
        num_scalar_prefetch=2,
        grid=(batch, s // tq),
        in_specs=[pl.BlockSpec((1, tq, w), lambda b, i, *_: (b, i, 0)),
                  pl.BlockSpec((1, s, w), lambda b, i, *_: (b, 0, 0)),
                  pl.BlockSpec((1, s, w), lambda b, i, *_: (b, 0, 0)),
                  pl.BlockSpec(bias.shape, lambda b, i, *_: (0, 0, 0, 0), pipeline_mode=pl.Buffered(1))],
        out_specs=pl.BlockSpec((1, tq, w), lambda b, i, *_: (b, i, 0)),
    )
    return pl.pallas_call(
        kern,
        grid_spec=grid_spec,
        out_shape=jax.ShapeDtypeStruct((batch, s, w), BF16),
        compiler_params=_cparams(2),
        name="neighbourhood_attention",
    )(jnp.asarray(starts, jnp.int32), jnp.asarray(variants, jnp.int32), q, k, v, bias)


def _s5_matrices(lam_re, lam_im, log_dt, b_re, b_im, c_re, c_im, d_skip):
    L, G, P, N = SSM_CHUNK, SSM_GROUPS, SSM_GROUP, SSM_STATE
    hi = lax.Precision.HIGHEST
    dt = jnp.exp(log_dt)[..., None]
    pw = jnp.arange(L + 1, dtype=F32)[:, None, None, None]
    mag = jnp.exp(lam_re * dt * pw)
    e_re = mag * jnp.cos(lam_im * dt * pw)
    e_im = mag * jnp.sin(lam_im * dt * pw)
    a_re, a_im = e_re[1], e_im[1]
    den = jnp.square(lam_re) + jnp.square(lam_im)
    f_re = ((a_re - 1.0) * lam_re + a_im * lam_im) / den
    f_im = (a_im * lam_re - (a_re - 1.0) * lam_im) / den
    bb_re = f_re[..., None] * b_re - f_im[..., None] * b_im
    bb_im = f_re[..., None] * b_im + f_im[..., None] * b_re
    ce_re = c_re[None] * e_re[:, :, :, None, :] - c_im[None] * e_im[:, :, :, None, :]
    ce_im = c_re[None] * e_im[:, :, :, None, :] + c_im[None] * e_re[:, :, :, None, :]
    kk = (jnp.einsum("kdgqn,dgnp->kdgqp", ce_re, bb_re, precision=hi)
          - jnp.einsum("kdgqn,dgnp->kdgqp", ce_im, bb_im, precision=hi))
    lag = np.arange(L)[None, :] - np.arange(L)[:, None]
    kf = kk[np.clip(lag, 0, L - 1), 0]
    kb = kk[np.clip(-lag, 0, L - 1), 1]
    skip = jnp.eye(P, dtype=F32)[None] * d_skip.reshape(G, P)[:, None, :]
    m = lambda cond: jnp.asarray(cond, F32)[:, :, None, None, None]
    kst = m(lag >= 0) * kf + m(lag <= 0) * kb + m(lag == 0) * skip[None, None]
    eye_g = jnp.eye(G, dtype=F32)
    t_mat = (jnp.transpose(kst, (0, 2, 4, 1, 3))[:, :, :, :, None, :]
             * eye_g[None, :, None, None, :, None]).reshape(L * G * P, L * G * P)

    def state_in(pows, d):
        x_re = e_re[pows, d][..., None] * bb_re[d][None] - e_im[pows, d][..., None] * bb_im[d][None]
        x_im = e_re[pows, d][..., None] * bb_im[d][None] + e_im[pows, d][..., None] * bb_re[d][None]
        place = lambda x: (jnp.transpose(x, (0, 1, 3, 2))[:, :, :, None, :]
                           * eye_g[None, :, None, :, None]).reshape(L * G * P, G * N)
        return place(x_re), place(x_im)

    wf_re, wf_im = state_in(np.arange(L)[::-1].copy(), 0)
    wb_re, wb_im = state_in(np.arange(L), 1)
    w1 = jnp.concatenate([t_mat, wf_re, wf_im, wb_re, wb_im], axis=1)

    def state_out(pows, d):
        place = lambda x: (jnp.transpose(x, (1, 3, 0, 2))[:, :, :, None, :]
                           * eye_g[:, None, None, :, None]).reshape(G * N, L * G * P)
        return place(ce_re[pows, d]), place(-ce_im[pows, d])

    cf_re, cf_im = state_out(np.arange(1, L + 1), 0)
    cb_re, cb_im = state_out(np.arange(L, 0, -1), 1)
    wc_re = jnp.concatenate([cf_re, cb_re], axis=0)
    wc_im = jnp.concatenate([cf_im, cb_im], axis=0)
    al_re = e_re[L].reshape(2, 1, G * N)
    al_im = e_im[L].reshape(2, 1, G * N)
    return w1.astype(BF16), wc_re.astype(BF16), wc_im.astype(BF16), al_re, al_im


def _mm_kernel(a_ref, b_ref, o_ref):
    o_ref[...] = _dot(a_ref[...], b_ref[...]).astype(o_ref.dtype)


def _matmul(a, b, tm, tn, out_dtype):
    m, kdim = a.shape
    n = b.shape[1]
    return pl.pallas_call(
        _mm_kernel,
        grid=(n // tn, m // tm),
        in_specs=[pl.BlockSpec((tm, kdim), lambda j, i: (i, 0)),
                  pl.BlockSpec((kdim, tn), lambda j, i: (0, j))],
        out_specs=pl.BlockSpec((tm, tn), lambda j, i: (i, j)),
        out_shape=jax.ShapeDtypeStruct((m, n), out_dtype),
        compiler_params=_cparams(2),
        name="s5_chunk_matmul",
    )(a, b)


def _s5_carry_kernel(sre_ref, sim_ref, are_ref, aim_ref, hre_ref, him_ref, *, ctx_chunks):
    n_chunks = sre_ref.shape[0]
    backward = pl.program_id(0) == 1
    a_re = jnp.broadcast_to(are_ref[0], sre_ref.shape[1:])
    a_im = jnp.broadcast_to(aim_ref[0], sre_ref.shape[1:])

    def step(j, carry):
        h_re, h_im = carry
        cb = jnp.where(j < ctx_chunks, ctx_chunks - 1 - j, n_chunks + ctx_chunks - 1 - j)
        c = jnp.where(backward, cb, j)
        hre_ref[c] = h_re.astype(hre_ref.dtype)
        him_ref[c] = h_im.astype(him_ref.dtype)
        n_re = a_re * h_re - a_im * h_im + sre_ref[c]
        n_im = a_re * h_im + a_im * h_re + sim_ref[c]
        return n_re, n_im

    zero = jnp.zeros(sre_ref.shape[1:], F32)
    lax.fori_loop(0, n_chunks, step, (zero, zero))


def _s5_carry(ys, al_re, al_im, n_chunks, batch, ctx_chunks):
    gn = SSM_GROUPS * SSM_STATE
    tw = 256
    base = SSM_CHUNK * SSM_WIDTH // tw
    per_dir = 2 * gn // tw
    s3 = ys.reshape(n_chunks, batch, ys.shape[1])
    kern = functools.partial(_s5_carry_kernel, ctx_chunks=ctx_chunks)
    blk = (n_chunks, batch, tw)
    return pl.pallas_call(
        kern,
        grid=(2, gn // tw),
        in_specs=[pl.BlockSpec(blk, lambda d, j: (0, 0, base + d * per_dir + j)),
                  pl.BlockSpec(blk, lambda d, j: (0, 0, base + d * per_dir + gn // tw + j)),
                  pl.BlockSpec((1, 1, tw), lambda d, j: (d, 0, j)),
                  pl.BlockSpec((1, 1, tw), lambda d, j: (d, 0, j))],
        out_specs=[pl.BlockSpec(blk, lambda d, j: (0, 0, d * (gn // tw) + j))] * 2,
        out_shape=[jax.ShapeDtypeStruct((n_chunks, batch, 2 * gn), BF16)] * 2,
        compiler_params=_cparams(2),
        name="s5_carry",
    )(s3, s3, al_re, al_im)


def _s5_readout_kernel(hre_ref, him_ref, wre_ref, wim_ref, y_ref, o_ref):
    o_ref[...] = y_ref[...] + _dot(hre_ref[...], wre_ref[...]) + _dot(him_ref[...], wim_ref[...])


def _s5_readout(h_re, h_im, wc_re, wc_im, ys, tm):
    m, kdim = h_re.shape
    n = wc_re.shape[1]
    tn = 1024
    return pl.pallas_call(
        _s5_readout_kernel,
        grid=(n // tn, m // tm),
        in_specs=[pl.BlockSpec((tm, kdim), lambda j, i: (i, 0)),
                  pl.BlockSpec((tm, kdim), lambda j, i: (i, 0)),
                  pl.BlockSpec((kdim, tn), lambda j, i: (0, j)),
                  pl.BlockSpec((kdim, tn), lambda j, i: (0, j)),
                  pl.BlockSpec((tm, tn), lambda j, i: (i, j))],
        out_specs=pl.BlockSpec((tm, tn), lambda j, i: (i, j)),
        out_shape=jax.ShapeDtypeStruct((m, n), F32),
        compiler_params=_cparams(2),
        name="s5_readout",
    )(h_re, h_im, wc_re, wc_im, ys)


def _row_tile(m, target=512):
    t = min(m, target)
    while m % t or t % 16:
        t -= 16
    return t


def _s5_mixer(u, mats, batch, s, n_ctx):
    w1, wc_re, wc_im, al_re, al_im = mats
    n_chunks = s // SSM_CHUNK
    cw = SSM_CHUNK * SSM_WIDTH
    uc = jnp.transpose(u.reshape(batch, n_chunks, cw), (1, 0, 2)).reshape(n_chunks * batch, cw)
    tm = _row_tile(n_chunks * batch)
    ys = _matmul(uc, w1, tm, 1024, F32)
    h_re, h_im = _s5_carry(ys, al_re, al_im, n_chunks, batch, n_ctx // SSM_CHUNK)
    gn2 = 2 * SSM_GROUPS * SSM_STATE
    y = _s5_readout(h_re.reshape(-1, gn2), h_im.reshape(-1, gn2), wc_re, wc_im, ys, tm)
    return jnp.transpose(y.reshape(n_chunks, batch, cw), (1, 0, 2)).reshape(batch * s, SSM_WIDTH)


def _gelu_tanh(y):
    return 0.5 * y * (1.0 + jnp.tanh(math.sqrt(2.0 / math.pi) * (y + 0.044715 * (y * y * y))))


def _out_mlp_kernel(*refs, even):
    if even:
        x_ref, m1_ref, m2_ref, mod_ref, wglu_ref, bglu_ref, wo_ref, g2_ref, w1_ref, w2_ref, o_ref = refs
        y = _gelu_tanh(m2_ref[...])
        z = _dot(y.astype(BF16), wglu_ref[...]) + bglu_ref[...]
        second = (y * jax.nn.sigmoid(z)).astype(BF16)
    else:
        x_ref, m1_ref, m2_ref, mod_ref, wo_ref, g2_ref, w1_ref, w2_ref, o_ref = refs
        second = m2_ref[...]
    m = mod_ref[0]
    mod = lambda j: m[:, j * D_MODEL:(j + 1) * D_MODEL]
    mix = jnp.concatenate([m1_ref[...], second], axis=1)
    x1 = x_ref[...] + mod(2) * _dot(mix, wo_ref[...])
    a = _modulate(x1, g2_ref[...], mod(3), mod(4)).astype(BF16)
    acc = None
    ck = 1024
    for c in range(D_FF // ck):
        h = jnp.maximum(_dot(a, w1_ref[:, c * ck:(c + 1) * ck]), 0.0)
        part = _dot((h * h).astype(BF16), w2_ref[c * ck:(c + 1) * ck, :])
        acc = part if acc is None else acc + part
    o_ref[...] = x1 + mod(5) * acc


def _out_mlp(x, mix1, mix2, modtab, consts, batch, nblk, even):
    rows = x.shape[0]
    row_spec = lambda w: pl.BlockSpec((TM, w), lambda b, i: (b * nblk + i, 0))
    return pl.pallas_call(
        functools.partial(_out_mlp_kernel, even=even),
        grid=(batch, nblk),
        in_specs=[row_spec(D_MODEL), row_spec(mix1.shape[1]), row_spec(mix2.shape[1]),
                  pl.BlockSpec((1, 1, N_MOD * D_MODEL), lambda b, i: (2 * b + jnp.minimum(i, 1), 0, 0))]
                 + [_resident(c.shape) for c in consts],
        out_specs=row_spec(D_MODEL),
        out_shape=jax.ShapeDtypeStruct((rows, D_MODEL), F32),
        compiler_params=_cparams(2),
        name="out_mlp_even" if even else "out_mlp_odd",
    )(x, mix1, mix2, modtab, *consts)


def _rope_tables(n_ctx, n_lat, rot_dim, head_w, lane_off):
    t = jnp.arange(n_lat)
    rows = (t // GRID_W).astype(F32)
    cols = (t % GRID_W).astype(F32)
    axis_dim = rot_dim // 2
    freqs = ROPE_BASE ** (-jnp.arange(0, axis_dim, 2, dtype=F32) / axis_dim)
    ang_r = rows[:, None] * freqs
    ang_c = cols[:, None] * freqs
    ang = jnp.concatenate([ang_r, ang_r, ang_c, ang_c], axis=-1)
    cos, sin = jnp.cos(ang), jnp.sin(ang)
    quarter = rot_dim // 4
    first = (np.arange(rot_dim) // quarter) % 2 == 0
    sin_a = jnp.where(first, -sin, 0.0)
    sin_b = jnp.where(first, 0.0, sin)

    def widen(tab, fill):
        full = jnp.full((n_lat, head_w), fill, F32).at[:, lane_off:lane_off + rot_dim].set(tab)
        full = jnp.concatenate([jnp.full((n_ctx, head_w), fill, F32), full], axis=0)
        return jnp.tile(full, (1, 128 // head_w))

    return widen(cos, 1.0), widen(sin_a, 0.0), widen(sin_b, 0.0)


def _head_sum_matrices(n_heads, head_dim):
    down = np.kron(np.eye(n_heads), np.ones((head_dim, 1)))
    pad = (-n_heads) % 128
    down = np.pad(down, ((0, 0), (0, pad)))
    return jnp.asarray(down, BF16), jnp.asarray(down.T, BF16)


def _pad_heads(w, n_heads, width, pad_to):
    lead = w.shape[:-1]
    w = w.reshape(lead + (n_heads, width))
    w = jnp.pad(w, [(0, 0)] * len(lead) + [(0, 0), (0, pad_to - width)])
    return w.reshape(lead + (n_heads * pad_to,))


def kernel(x, c, ctx, c_ctx, w_mod, b_mod, g_norm1, g_norm2, w_ff1, w_ff2, e_w_in, e_w_out, e_g_q, e_g_k, ssm_lam_re, ssm_lam_im, ssm_log_dt, ssm_b_re, ssm_b_im, ssm_c_re, ssm_c_im, ssm_d, ssm_w_glu, ssm_b_glu, o_w_in, o_w_out, mla_g_cq, mla_g_ckv, mla_w_uq, mla_w_ukv, mla_g_q, mla_g_k, na_g_q, na_g_k, na_rpb):
    batch, n_lat, d = x.shape
    n_ctx = ctx.shape[1]
    depth = w_mod.shape[0]
    assert d == D_MODEL and n_ctx == TM and n_lat % TM == 0 and n_lat % GRID_W == 0
    s = n_ctx + n_lat
    nblk = s // TM
    rows = n_lat // GRID_W

    pad_rows = (-(batch + 1)) % 8
    cond = jnp.concatenate([c, c_ctx[None], jnp.zeros((pad_rows, d), F32)], axis=0)
    mods = _mod_vectors(cond, w_mod, b_mod)
    modtabs = jnp.stack([jnp.broadcast_to(mods[:, batch:batch + 1], (depth, batch, N_MOD * d)), mods[:, :batch]],
                        axis=2).reshape(depth, 2 * batch, 1, N_MOD * d)

    xs = jnp.concatenate([ctx, x], axis=1).reshape(batch * s, d)

    even_tabs = _rope_tables(n_ctx, n_lat, HEAD_DIM, HEAD_DIM, 0)
    mla_tabs = _rope_tables(n_ctx, n_lat, MLA_ROPE, MLA_PAD, MLA_NOPE)
    down_e, up_e = _head_sum_matrices(GQA_Q_HEADS + GQA_KV_HEADS, HEAD_DIM)
    down_o, up_o = _head_sum_matrices(2 * NA_HEADS, HEAD_DIM)
    row = lambda v: v.reshape(1, -1)

    for i in range(depth):
        j = i // 2
        g1, g2 = row(g_norm1[i]), row(g_norm2[i])
        w1, w2 = w_ff1[i].astype(BF16), w_ff2[i].astype(BF16)
        if i % 2 == 0:
            gqk = jnp.concatenate([jnp.tile(e_g_q[j], GQA_Q_HEADS) * HEAD_DIM ** -0.5,
                                   jnp.tile(e_g_k[j], GQA_KV_HEADS)])
            q, k, v, u = _even_in_proj(xs, modtabs[i], g1, e_w_in[j].astype(BF16), down_e, up_e, row(gqk),
                                       even_tabs, batch, nblk)
            att = _attention(q.reshape(batch, s, -1), k.reshape(batch, s, -1), v.reshape(batch, s, -1),
                             n_kv=GQA_KV_HEADS, group=GQA_GROUP, dk=HEAD_DIM, dv=HEAD_DIM, n_ctx=n_ctx, tq=128)
            mats = _s5_matrices(ssm_lam_re[j], ssm_lam_im[j], ssm_log_dt[j], ssm_b_re[j], ssm_b_im[j],
                                ssm_c_re[j], ssm_c_im[j], ssm_d[j])
            y = _s5_mixer(u, mats, batch, s, n_ctx)
            consts = (ssm_w_glu[j].astype(BF16), row(ssm_b_glu[j]), e_w_out[j].astype(BF16), g2, w1, w2)
            xs = _out_mlp(xs, att.reshape(batch * s, -1), y, modtabs[i], consts, batch, nblk, True)
        else:
            w_in = o_w_in[j]
            c1 = MLA_Q_RANK
            c2 = c1 + MLA_KV_RANK
            c3 = c2 + MLA_ROPE
            kr_cols = jnp.pad(w_in[:, c2:c3], ((0, 0), (MLA_NOPE, MLA_PAD - MLA_QK)))
            w_in_p = jnp.concatenate([w_in[:, :c2], kr_cols, w_in[:, c3:]], axis=1).astype(BF16)
            wuq = _pad_heads(mla_w_uq[j], MLA_HEADS, MLA_QK, MLA_PAD).astype(BF16)
            wukv = mla_w_ukv[j].reshape(MLA_KV_RANK, MLA_HEADS, MLA_NOPE + MLA_V)
            wuk = _pad_heads(wukv[:, :, :MLA_NOPE].reshape(MLA_KV_RANK, -1), MLA_HEADS, MLA_NOPE, MLA_PAD).astype(BF16)
            wuv = wukv[:, :, MLA_NOPE:].reshape(MLA_KV_RANK, -1).astype(BF16)
            gmq = _pad_heads(jnp.tile(mla_g_q[j], MLA_HEADS) * MLA_QK ** -0.5, MLA_HEADS, MLA_QK, MLA_PAD)
            gmk = _pad_heads(jnp.tile(mla_g_k[j], MLA_HEADS), MLA_HEADS, MLA_QK, MLA_PAD)
            gnqk = jnp.concatenate([jnp.tile(na_g_q[j], NA_HEADS) * HEAD_DIM ** -0.5, jnp.tile(na_g_k[j], NA_HEADS)])
            mq, mk, mv, nq, nk, nv = _odd_in_proj(
                xs, modtabs[i], g1, w_in_p, row(mla_g_cq[j]), row(mla_g_ckv[j]), wuq, wuk, wuv,
                row(gmq), row(gmk), down_o, up_o, row(gnqk), mla_tabs, batch, nblk)
            b3 = lambda t: t.reshape(batch, s, -1)
            mla = _attention(b3(mq), b3(mk), b3(mv), n_kv=MLA_HEADS, group=1, dk=MLA_PAD, dv=MLA_V,
                             n_ctx=n_ctx, tq=TM)
            bias = _na_bias_table(na_rpb[j], rows)
            na = _neighbourhood_attention(b3(nq), b3(nk), b3(nv), bias, rows, n_ctx)
            consts = (o_w_out[j].astype(BF16), g2, w1, w2)
            xs = _out_mlp(xs, mla.reshape(batch * s, -1), na.reshape(batch * s, -1), modtabs[i], consts,
                          batch, nblk, False)
    return xs.reshape(batch, s, d)[:, n_ctx:]
```

```python
import functools
import math

import numpy as np
import jax
import jax.numpy as jnp
from jax import lax
from jax.experimental import pallas as pl
from jax.experimental.pallas import tpu as pltpu

F32 = jnp.float32
BF16 = jnp.bfloat16

D_MODEL = 1024
GRID_W = 64
HEAD_DIM = 64
ROPE_BASE = 10000.0
EPS = 1e-6
N_MOD = 6
D_FF = 4 * D_MODEL
LOG2E = math.log2(math.e)

GQA_Q_HEADS = 12
GQA_KV_HEADS = 4
GQA_GROUP = GQA_Q_HEADS // GQA_KV_HEADS
GQA_Q_W = GQA_Q_HEADS * HEAD_DIM
GQA_KV_W = GQA_KV_HEADS * HEAD_DIM
SSM_WIDTH = 256
SSM_GROUP = 16
SSM_GROUPS = SSM_WIDTH // SSM_GROUP
SSM_STATE = 64
SSM_CHUNK = 8

MLA_HEADS = 8
MLA_Q_RANK = 512
MLA_KV_RANK = 256
MLA_NOPE = 64
MLA_ROPE = 32
MLA_QK = MLA_NOPE + MLA_ROPE
MLA_V = 64
MLA_PAD = 128
V_EXT = 128
NA_HEADS = 8
NA_W = NA_HEADS * HEAD_DIM
NA_WIN_R = 8
NA_WIN_C = 16
NA_ROWS_PER_BLOCK = 4

N_CTX = 256
TM = 512
TQ = 256
KEY_CHUNK = 512
MASK_VALUE = -1e30
VMEM_LIMIT = 52 * 1024 * 1024


def _cparams(n_axes):
    return pltpu.CompilerParams(dimension_semantics=("parallel",) * n_axes, vmem_limit_bytes=VMEM_LIMIT)


def _resident(shape):
    nd = len(shape)
    return pl.BlockSpec(shape, lambda *_: (0,) * nd, pipeline_mode=pl.Buffered(1))


def _dot(a, b):
    return jnp.dot(a, b, preferred_element_type=F32)


def _dot_nt(a, b):
    return lax.dot_general(a, b, (((1,), (1,)), ((), ())), preferred_element_type=F32)


def _split_dot(x, w):
    hi = x.astype(BF16)
    lo = (x - hi.astype(F32)).astype(BF16)
    return _dot(hi, w) + _dot(lo, w)


def _modulate(xf, g, shift, scale):
    ms = jnp.mean(xf * xf, axis=-1, keepdims=True)
    return (xf * lax.rsqrt(ms + EPS) * g) * (1.0 + scale) + shift


def _head_rms(x, down, up, head_dim):
    ss = _split_dot(x * x, down)
    inv = lax.rsqrt(ss * (1.0 / head_dim) + EPS)
    return x * _split_dot(inv, up)


def _rope(x, cos, sin_a, sin_b, shift):
    n = x.shape[-1]
    reps = n // cos.shape[-1]
    cos, sin_a, sin_b = (jnp.tile(t, (1, reps)) for t in (cos, sin_a, sin_b))
    return x * cos + pltpu.roll(x, n - shift, 1) * sin_a + pltpu.roll(x, shift, 1) * sin_b


class _Rows:
    def __init__(self, batch, n_lat):
        assert (batch * N_CTX) % TM == 0 and n_lat % TM == 0 and (batch * N_CTX) % n_lat == 0
        self.batch, self.n_lat = batch, n_lat
        self.n_ctx_rows = batch * N_CTX
        self.n_rows = self.n_ctx_rows + batch * n_lat
        self.ctx_blocks = self.n_ctx_rows // TM
        self.lat_blocks = batch * n_lat // TM
        self.blocks_per_batch = n_lat // TM

    def mod_index(self, blk):
        return jnp.where(blk < self.ctx_blocks, 0, 1 + (blk - self.ctx_blocks) // self.blocks_per_batch)

    def rope_index(self, blk):
        return jnp.where(blk < self.ctx_blocks, 0, 1 + (blk - self.ctx_blocks) % self.blocks_per_batch)

    def query_block(self, b, i, tq):
        cs = N_CTX // tq
        return jnp.where(i < cs, b * cs + i, self.n_ctx_rows // tq + b * (self.n_lat // tq) + i - cs)

    def latent_block(self, b):
        return self.n_ctx_rows // self.n_lat + b


def _mod_kernel(c_ref, w_ref, b_ref, o_ref):
    c = c_ref[...]
    s = (c * jax.nn.sigmoid(c)).astype(BF16)
    o_ref[0] = _dot(s, w_ref[0].astype(BF16)) + b_ref[0]


def _mod_vectors(cond, w_mod, b_mod):
    depth, d, n = w_mod.shape
    rows = cond.shape[0]
    tn = 1536
    return pl.pallas_call(
        _mod_kernel,
        grid=(depth, n // tn),
        in_specs=[pl.BlockSpec((rows, d), lambda l, j: (0, 0)),
                  pl.BlockSpec((1, d, tn), lambda l, j: (l, 0, j)),
                  pl.BlockSpec((1, 1, tn), lambda l, j: (l, 0, j))],
        out_specs=pl.BlockSpec((1, rows, tn), lambda l, j: (l, 0, j)),
        out_shape=jax.ShapeDtypeStruct((depth, rows, n), F32),
        compiler_params=_cparams(2),
        name="mod_vectors",
    )(cond, w_mod, b_mod.reshape(depth, 1, n))


def _even_in_kernel(x_ref, mod_ref, g1_ref, w_ref, down_ref, up_ref, gqk_ref, ones_ref, cos_ref, sa_ref, sb_ref,
                    q_ref, k_ref, v_ref, u_ref):
    m = mod_ref[0]
    a = _modulate(x_ref[...], g1_ref[...], m[:, 0:D_MODEL], m[:, D_MODEL:2 * D_MODEL]).astype(BF16)
    h = _dot(a, w_ref[...])
    nqk = GQA_Q_W + GQA_KV_W
    nv = GQA_KV_HEADS * V_EXT
    qk = _head_rms(h[:, :nqk], down_ref[...], up_ref[...], HEAD_DIM) * gqk_ref[...]
    qk = _rope(qk, cos_ref[...], sa_ref[...], sb_ref[...], HEAD_DIM // 4).astype(BF16)
    q_ref[...] = qk[:, :GQA_Q_W]
    k_ref[...] = qk[:, GQA_Q_W:]
    v_ref[...] = (h[:, nqk:nqk + nv] + ones_ref[...]).astype(BF16)
    u_ref[...] = h[:, nqk + nv:].astype(BF16)


def _mod_spec(rows):
    return pl.BlockSpec((1, 1, N_MOD * D_MODEL), lambda i: (rows.mod_index(i), 0, 0))


def _even_in_proj(x, modtab, consts, tabs, rows):
    row_spec = lambda w: pl.BlockSpec((TM, w), lambda i: (i, 0))
    tab_spec = pl.BlockSpec((TM, 128), lambda i: (rows.rope_index(i), 0))
    widths = (GQA_Q_W, GQA_KV_W, GQA_KV_HEADS * V_EXT, SSM_WIDTH)
    return pl.pallas_call(
        _even_in_kernel,
        grid=(rows.n_rows // TM,),
        in_specs=[row_spec(D_MODEL), _mod_spec(rows)] + [_resident(c.shape) for c in consts] + [tab_spec] * 3,
        out_specs=[row_spec(w) for w in widths],
        out_shape=[jax.ShapeDtypeStruct((rows.n_rows, w), BF16) for w in widths],
        compiler_params=_cparams(1),
        name="even_in_proj",
    )(x, modtab, *consts, *tabs)


def _odd_in_kernel(x_ref, mod_ref, g1_ref, w_ref, gcq_ref, gckv_ref, wuq_ref, wuk_ref, wuv_ref,
                   gmq_ref, gmk_ref, down_ref, up_ref, gnqk_ref, ones_ref, cos_ref, sa_ref, sb_ref,
                   mq_ref, mk_ref, mv_ref, nq_ref, nk_ref, nv_ref):
    m = mod_ref[0]
    a = _modulate(x_ref[...], g1_ref[...], m[:, 0:D_MODEL], m[:, D_MODEL:2 * D_MODEL]).astype(BF16)
    h = _dot(a, w_ref[...])
    c1 = MLA_Q_RANK
    c2 = c1 + MLA_KV_RANK
    c3 = c2 + MLA_PAD

    def rms(t, g):
        return (t * lax.rsqrt(jnp.mean(t * t, axis=-1, keepdims=True) + EPS) * g).astype(BF16)

    q = _dot(rms(h[:, :c1], gcq_ref[...]), wuq_ref[...])
    ckv = rms(h[:, c1:c2], gckv_ref[...])
    k = _dot(ckv, wuk_ref[...]) + jnp.tile(h[:, c2:c3], (1, MLA_HEADS))
    mv_ref[...] = (_dot(ckv, wuv_ref[...]) + ones_ref[...]).astype(BF16)

    def mla_heads(t, g):
        parts = []
        for hh in range(MLA_HEADS):
            th = t[:, hh * MLA_PAD:(hh + 1) * MLA_PAD]
            ss = jnp.sum(th * th, axis=-1, keepdims=True)
            parts.append(th * lax.rsqrt(ss * (1.0 / MLA_QK) + EPS))
        t = jnp.concatenate(parts, axis=1) * g
        return _rope(t, cos_ref[...], sa_ref[...], sb_ref[...], MLA_ROPE // 4).astype(BF16)

    mq_ref[...] = mla_heads(q, gmq_ref[...])
    mk_ref[...] = mla_heads(k, gmk_ref[...])

    nqk = _head_rms(h[:, c3:c3 + 2 * NA_W], down_ref[...], up_ref[...], HEAD_DIM) * gnqk_ref[...]
    nqk = nqk.astype(BF16)
    nq_ref[...] = nqk[:, :NA_W]
    nk_ref[...] = nqk[:, NA_W:]
    nv_ref[...] = h[:, c3 + 2 * NA_W:].astype(BF16)


def _odd_in_proj(x, modtab, consts, tabs, rows):
    row_spec = lambda w: pl.BlockSpec((TM, w), lambda i: (i, 0))
    tab_spec = pl.BlockSpec((TM, 128), lambda i: (rows.rope_index(i), 0))
    mla_w = MLA_HEADS * MLA_PAD
    widths = (mla_w, mla_w, MLA_HEADS * V_EXT, NA_W, NA_W, NA_W)
    return pl.pallas_call(
        _odd_in_kernel,
        grid=(rows.n_rows // TM,),
        in_specs=[row_spec(D_MODEL), _mod_spec(rows)] + [_resident(c.shape) for c in consts] + [tab_spec] * 3,
        out_specs=[row_spec(w) for w in widths],
        out_shape=[jax.ShapeDtypeStruct((rows.n_rows, w), BF16) for w in widths],
        compiler_params=_cparams(1),
        name="odd_in_proj",
    )(x, modtab, *consts, *tabs)


def _attn_kernel(q_ref, kc_ref, kl_ref, vc_ref, vl_ref, o_ref, *, n_kv, group, dk, tq, online, first_step):
    n_lat = kl_ref.shape[0]

    def attend(with_latent):
        outs = [None] * (n_kv * group)
        for kv in range(n_kv):
            ksl = slice(kv * dk, (kv + 1) * dk)
            vsl = slice(kv * V_EXT, (kv + 1) * V_EXT)
            qs = [q_ref[:, (kv * group + g) * dk:(kv * group + g + 1) * dk] for g in range(group)]
            qg = jnp.concatenate(qs, axis=0) if group > 1 else qs[0]
            s = _dot_nt(qg, kc_ref[:, ksl])
            m = jnp.max(s, axis=-1, keepdims=True)
            if not with_latent:
                acc = _dot(jnp.exp2(s - m).astype(BF16), vc_ref[:, vsl])
            elif online:
                acc = _dot(jnp.exp2(s - m).astype(BF16), vc_ref[:, vsl])
                for c in range(n_lat // KEY_CHUNK):
                    keys = slice(c * KEY_CHUNK, (c + 1) * KEY_CHUNK)
                    s = _dot_nt(qg, kl_ref[keys, ksl])
                    m_new = jnp.maximum(m, jnp.max(s, axis=-1, keepdims=True))
                    acc = jnp.exp2(m - m_new) * acc + _dot(jnp.exp2(s - m_new).astype(BF16), vl_ref[keys, vsl])
                    m = m_new
            else:
                s_lat = _dot_nt(qg, kl_ref[:, ksl])
                m = jnp.maximum(m, jnp.max(s_lat, axis=-1, keepdims=True))
                acc = (_dot(jnp.exp2(s - m).astype(BF16), vc_ref[:, vsl])
                       + _dot(jnp.exp2(s_lat - m).astype(BF16), vl_ref[:, vsl]))
            o = acc * (1.0 / pltpu.roll(acc, HEAD_DIM, 1))
            for g in range(group):
                outs[kv * group + g] = o[g * tq:(g + 1) * tq, :HEAD_DIM]
        o_ref[...] = jnp.concatenate(outs, axis=1).astype(o_ref.dtype)

    if first_step == 0:
        is_ctx = pl.program_id(1) < N_CTX // tq

        @pl.when(is_ctx)
        def _():
            attend(False)

        @pl.when(jnp.logical_not(is_ctx))
        def _():
            attend(True)
    else:
        attend(True)


def _attention(q, k, v, rows, *, n_kv, group, dk, tq, online, skip_ctx_queries):
    ctx_steps = N_CTX // tq
    first = ctx_steps if skip_ctx_queries else 0
    kern = functools.partial(_attn_kernel, n_kv=n_kv, group=group, dk=dk, tq=tq, online=online, first_step=first)
    qmap = lambda b, i: (rows.query_block(b, i + first, tq), 0)
    wo = n_kv * group * HEAD_DIM
    return pl.pallas_call(
        kern,
        grid=(rows.batch, rows.n_lat // tq + ctx_steps - first),
        in_specs=[pl.BlockSpec((tq, q.shape[1]), qmap),
                  pl.BlockSpec((N_CTX, k.shape[1]), lambda b, i: (b, 0)),
                  pl.BlockSpec((rows.n_lat, k.shape[1]), lambda b, i: (rows.latent_block(b), 0)),
                  pl.BlockSpec((N_CTX, v.shape[1]), lambda b, i: (b, 0)),
                  pl.BlockSpec((rows.n_lat, v.shape[1]), lambda b, i: (rows.latent_block(b), 0))],
        out_specs=pl.BlockSpec((tq, wo), qmap),
        out_shape=jax.ShapeDtypeStruct((rows.n_rows, wo), BF16),
        compiler_params=_cparams(2),
        name="attention_dk%d" % dk,
    )(q, k, k, v, v)


def _na_geometry(grid_rows):
    rb = NA_ROWS_PER_BLOCK
    wr = min(NA_WIN_R, grid_rows)
    span = min(grid_rows, wr + rb - 1)
    starts, variants, keys = [], [], {}
    for r0 in range(0, grid_rows, rb):
        rs0 = int(np.clip(r0 - wr // 2, 0, grid_rows - wr))
        start = min(rs0, grid_rows - span)
        rel = tuple(int(np.clip(r0 + dr - wr // 2, 0, grid_rows - wr)) - (r0 + dr) for dr in range(rb))
        key = (rel, start - r0)
        variants.append(keys.setdefault(key, len(keys)))
        starts.append(start)
    firsts = [variants.index(v) for v in range(len(keys))]
    return wr, span, starts, variants, firsts


def _na_bias_table(rpb, grid_rows):
    rb = NA_ROWS_PER_BLOCK
    wr, span, starts, variants, firsts = _na_geometry(grid_rows)
    nv = len(firsts)
    idx_r = np.zeros((nv, rb, span), np.int32)
    ok_r = np.zeros((nv, rb, span), bool)
    for v, blk in enumerate(firsts):
        r0, start = blk * rb, starts[blk]
        for dr in range(rb):
            r = r0 + dr
            rs = int(np.clip(r - wr // 2, 0, grid_rows - wr))
            for j in range(span):
                kr = start + j
                ok_r[v, dr, j] = rs <= kr < rs + wr
                idx_r[v, dr, j] = np.clip(kr - r + NA_WIN_R - 1, 0, 2 * NA_WIN_R - 2)
    c = np.arange(GRID_W)
    cs = np.clip(c - NA_WIN_C // 2, 0, GRID_W - NA_WIN_C)
    kc = np.arange(GRID_W)
    ok_c = (kc[None, :] >= cs[:, None]) & (kc[None, :] < cs[:, None] + NA_WIN_C)
    idx_c = np.clip(kc[None, :] - c[:, None] + NA_WIN_C - 1, 0, 2 * NA_WIN_C - 2)
    n_rel_c = 2 * NA_WIN_C - 1
    picked = rpb[:, idx_r.reshape(-1)].reshape(rpb.shape[0], nv, rb, span, n_rel_c)
    onehot = jnp.asarray(idx_c[None, :, :] == np.arange(n_rel_c)[:, None, None], F32)
    bias = jnp.einsum("hvdjx,xck->vhdcjk", picked, onehot, precision=lax.Precision.HIGHEST)
    ok = ok_r[:, None, :, None, :, None] & ok_c[None, None, None, :, None, :]
    bias = jnp.where(ok, bias * LOG2E, MASK_VALUE)
    return bias.reshape(nv, rpb.shape[0], rb * GRID_W, span * GRID_W)


def _na_kernel(start_ref, var_ref, q_ref, kc_ref, kl_ref, vc_ref, vl_ref, bias_ref, o_ref, *, n_loc, first_step):
    i = pl.program_id(1) + first_step

    def softmax_pv(s_list, v_list):
        m = jnp.max(s_list[0], axis=-1, keepdims=True)
        for s in s_list[1:]:
            m = jnp.maximum(m, jnp.max(s, axis=-1, keepdims=True))
        ps = [jnp.exp2(s - m) for s in s_list]
        l = sum(jnp.sum(p, axis=-1, keepdims=True) for p in ps)
        o = sum(_dot(p.astype(BF16), v) for p, v in zip(ps, v_list))
        return o / l

    def heads(fn):
        outs = []
        for h in range(NA_HEADS):
            hs = slice(h * HEAD_DIM, (h + 1) * HEAD_DIM)
            outs.append(fn(h, hs, q_ref[:, hs], _dot_nt(q_ref[:, hs], kc_ref[:, hs]), vc_ref[:, hs]))
        o_ref[...] = jnp.concatenate(outs, axis=1).astype(o_ref.dtype)

    def context_queries():
        heads(lambda h, hs, qh, s_ctx, vc: softmax_pv([s_ctx], [vc]))

    def latent_queries():
        off = pl.multiple_of(start_ref[i - 1] * GRID_W, GRID_W)
        var = var_ref[i - 1]

        def one(h, hs, qh, s_ctx, vc):
            s_loc = _dot_nt(qh, kl_ref[pl.ds(off, n_loc), hs]) + bias_ref[var, h]
            return softmax_pv([s_ctx, s_loc], [vc, vl_ref[pl.ds(off, n_loc), hs]])

        heads(one)

    if first_step == 0:
        pl.when(i == 0)(context_queries)
        pl.when(i > 0)(latent_queries)
    else:
        latent_queries()


def _neighbourhood_attention(q, k, v, bias, rows, skip_ctx_queries):
    grid_rows = rows.n_lat // GRID_W
    _, span, starts, variants, _ = _na_geometry(grid_rows)
    assert NA_ROWS_PER_BLOCK * GRID_W == TQ
    first = 1 if skip_ctx_queries else 0
    kern = functools.partial(_na_kernel, n_loc=span * GRID_W, first_step=first)
    qmap = lambda b, i, *_: (rows.query_block(b, i + first, TQ), 0)
    cmap = lambda b, i, *_: (b, 0)
    lmap = lambda b, i, *_: (rows.latent_block(b), 0)
    grid_spec = pltpu.PrefetchScalarGridSpec(
        num_scalar_prefetch=2,
        grid=(rows.batch, rows.n_lat // TQ + 1 - first),
        in_specs=[pl.BlockSpec((TQ, NA_W), qmap),
                  pl.BlockSpec((N_CTX, NA_W), cmap), pl.BlockSpec((rows.n_lat, NA_W), lmap),
                  pl.BlockSpec((N_CTX, NA_W), cmap), pl.BlockSpec((rows.n_lat, NA_W), lmap),
                  pl.BlockSpec(bias.shape, lambda b, i, *_: (0, 0, 0, 0), pipeline_mode=pl.Buffered(1))],
        out_specs=pl.BlockSpec((TQ, NA_W), qmap),
    )
    return pl.pallas_call(
        kern,
        grid_spec=grid_spec,
        out_shape=jax.ShapeDtypeStruct((rows.n_rows, NA_W), BF16),
        compiler_params=_cparams(2),
        name="neighbourhood_attention",
    )(jnp.asarray(starts, jnp.int32), jnp.asarray(variants, jnp.int32), q, k, k, v, v, bias)


def _s5_matrices(lam_re, lam_im, log_dt, b_re, b_im, c_re, c_im, d_skip):
    L, G, P, N = SSM_CHUNK, SSM_GROUPS, SSM_GROUP, SSM_STATE
    hi = lax.Precision.HIGHEST
    dt = jnp.exp(log_dt)[..., None]
    pw = jnp.arange(L + 1, dtype=F32)[:, None, None, None]
    mag = jnp.exp(lam_re * dt * pw)
    e_re = mag * jnp.cos(lam_im * dt * pw)
    e_im = mag * jnp.sin(lam_im * dt * pw)
    a_re, a_im = e_re[1], e_im[1]
    den = jnp.square(lam_re) + jnp.square(lam_im)
    f_re = ((a_re - 1.0) * lam_re + a_im * lam_im) / den
    f_im = (a_im * lam_re - (a_re - 1.0) * lam_im) / den
    bb_re = f_re[..., None] * b_re - f_im[..., None] * b_im
    bb_im = f_re[..., None] * b_im + f_im[..., None] * b_re
    ce_re = c_re[None] * e_re[:, :, :, None, :] - c_im[None] * e_im[:, :, :, None, :]
    ce_im = c_re[None] * e_im[:, :, :, None, :] + c_im[None] * e_re[:, :, :, None, :]
    kk = (jnp.einsum("kdgqn,dgnp->kdgqp", ce_re, bb_re, precision=hi)
          - jnp.einsum("kdgqn,dgnp->kdgqp", ce_im, bb_im, precision=hi))
    lag = np.arange(L)[None, :] - np.arange(L)[:, None]
    kf = kk[np.clip(lag, 0, L - 1), 0]
    kb = kk[np.clip(-lag, 0, L - 1), 1]
    skip = jnp.eye(P, dtype=F32)[None] * d_skip.reshape(G, P)[:, None, :]
    m = lambda cond: jnp.asarray(cond, F32)[:, :, None, None, None]
    kst = m(lag >= 0) * kf + m(lag <= 0) * kb + m(lag == 0) * skip[None, None]
    eye_g = jnp.eye(G, dtype=F32)
    t_mat = (jnp.transpose(kst, (0, 2, 4, 1, 3))[:, :, :, :, None, :]
             * eye_g[None, :, None, None, :, None]).reshape(L * G * P, L * G * P)

    def state_in(pows, d):
        x_re = e_re[pows, d][..., None] * bb_re[d][None] - e_im[pows, d][..., None] * bb_im[d][None]
        x_im = e_re[pows, d][..., None] * bb_im[d][None] + e_im[pows, d][..., None] * bb_re[d][None]
        place = lambda x: (jnp.transpose(x, (0, 1, 3, 2))[:, :, :, None, :]
                           * eye_g[None, :, None, :, None]).reshape(L * G * P, G * N)
        return place(x_re), place(x_im)

    wf_re, wf_im = state_in(np.arange(L)[::-1].copy(), 0)
    wb_re, wb_im = state_in(np.arange(L), 1)
    w1 = jnp.concatenate([t_mat, wf_re, wf_im, wb_re, wb_im], axis=1)

    def state_out(pows, d):
        place = lambda x: (jnp.transpose(x, (1, 3, 0, 2))[:, :, :, None, :]
                           * eye_g[:, None, None, :, None]).reshape(G * N, L * G * P)
        return place(ce_re[pows, d]), place(-ce_im[pows, d])

    cf_re, cf_im = state_out(np.arange(1, L + 1), 0)
    cb_re, cb_im = state_out(np.arange(L, 0, -1), 1)
    wc_re = jnp.concatenate([cf_re, cb_re], axis=0)
    wc_im = jnp.concatenate([cf_im, cb_im], axis=0)
    al_re = e_re[L].reshape(2, 1, G * N)
    al_im = e_im[L].reshape(2, 1, G * N)
    return w1.astype(BF16), wc_re.astype(BF16), wc_im.astype(BF16), al_re, al_im


def _mm_kernel(a_ref, b_ref, o_ref):
    o_ref[...] = _dot(a_ref[...], b_ref[...]).astype(o_ref.dtype)


def _matmul(a, b, tm, tn, out_dtype):
    m, kdim = a.shape
    n = b.shape[1]
    return pl.pallas_call(
        _mm_kernel,
        grid=(n // tn, m // tm),
        in_specs=[pl.BlockSpec((tm, kdim), lambda j, i: (i, 0)),
                  pl.BlockSpec((kdim, tn), lambda j, i: (0, j))],
        out_specs=pl.BlockSpec((tm, tn), lambda j, i: (i, j)),
        out_shape=jax.ShapeDtypeStruct((m, n), out_dtype),
        compiler_params=_cparams(2),
        name="s5_chunk_matmul",
    )(a, b)


def _s5_carry_kernel(sre_ref, sim_ref, are_ref, aim_ref, hre_ref, him_ref, *, ctx_chunks):
    n_chunks = sre_ref.shape[0]
    backward = pl.program_id(0) == 1
    a_re = jnp.broadcast_to(are_ref[0], sre_ref.shape[1:])
    a_im = jnp.broadcast_to(aim_ref[0], sre_ref.shape[1:])

    def step(j, carry):
        h_re, h_im = carry
        cb = jnp.where(j < ctx_chunks, ctx_chunks - 1 - j, n_chunks + ctx_chunks - 1 - j)
        c = jnp.where(backward, cb, j)
        hre_ref[c] = h_re.astype(hre_ref.dtype)
        him_ref[c] = h_im.astype(him_ref.dtype)
        n_re = a_re * h_re - a_im * h_im + sre_ref[c]
        n_im = a_re * h_im + a_im * h_re + sim_ref[c]
        return n_re, n_im

    zero = jnp.zeros(sre_ref.shape[1:], F32)
    lax.fori_loop(0, n_chunks, step, (zero, zero))


def _s5_carry(ys, al_re, al_im, n_chunks, batch, ctx_chunks):
    gn = SSM_GROUPS * SSM_STATE
    tw = 256
    base = SSM_CHUNK * SSM_WIDTH // tw
    per_dir = 2 * gn // tw
    s3 = ys.reshape(n_chunks, batch, ys.shape[1])
    kern = functools.partial(_s5_carry_kernel, ctx_chunks=ctx_chunks)
    blk = (n_chunks, batch, tw)
    return pl.pallas_call(
        kern,
        grid=(2, gn // tw),
        in_specs=[pl.BlockSpec(blk, lambda d, j: (0, 0, base + d * per_dir + j)),
                  pl.BlockSpec(blk, lambda d, j: (0, 0, base + d * per_dir + gn // tw + j)),
                  pl.BlockSpec((1, 1, tw), lambda d, j: (d, 0, j)),
                  pl.BlockSpec((1, 1, tw), lambda d, j: (d, 0, j))],
        out_specs=[pl.BlockSpec(blk, lambda d, j: (0, 0, d * (gn // tw) + j))] * 2,
        out_shape=[jax.ShapeDtypeStruct((n_chunks, batch, 2 * gn), BF16)] * 2,
        compiler_params=_cparams(2),
        name="s5_carry",
    )(s3, s3, al_re, al_im)


def _s5_readout_kernel(hre_ref, him_ref, wre_ref, wim_ref, y_ref, o_ref):
    o_ref[...] = y_ref[...] + _dot(hre_ref[...], wre_ref[...]) + _dot(him_ref[...], wim_ref[...])


def _s5_readout(h_re, h_im, wc_re, wc_im, ys, tm):
    m, kdim = h_re.shape
    n = wc_re.shape[1]
    tn = 1024
    return pl.pallas_call(
        _s5_readout_kernel,
        grid=(n // tn, m // tm),
        in_specs=[pl.BlockSpec((tm, kdim), lambda j, i: (i, 0)),
                  pl.BlockSpec((tm, kdim), lambda j, i: (i, 0)),
                  pl.BlockSpec((kdim, tn), lambda j, i: (0, j)),
                  pl.BlockSpec((kdim, tn), lambda j, i: (0, j)),
                  pl.BlockSpec((tm, tn), lambda j, i: (i, j))],
        out_specs=pl.BlockSpec((tm, tn), lambda j, i: (i, j)),
        out_shape=jax.ShapeDtypeStruct((m, n), F32),
        compiler_params=_cparams(2),
        name="s5_readout",
    )(h_re, h_im, wc_re, wc_im, ys)


def _row_tile(m, target=512):
    t = min(m, target)
    while m % t or t % 16:
        t -= 16
    return t


def _s5_mixer(u, mats, rows):
    w1, wc_re, wc_im, al_re, al_im = mats
    batch = rows.batch
    cw = SSM_CHUNK * SSM_WIDTH
    ctx_chunks = N_CTX // SSM_CHUNK
    n_chunks = ctx_chunks + rows.n_lat // SSM_CHUNK
    uc = jnp.concatenate([u[:rows.n_ctx_rows].reshape(batch, ctx_chunks, cw),
                          u[rows.n_ctx_rows:].reshape(batch, n_chunks - ctx_chunks, cw)], axis=1)
    uc = jnp.transpose(uc, (1, 0, 2)).reshape(n_chunks * batch, cw)
    tm = _row_tile(n_chunks * batch)
    ys = _matmul(uc, w1, tm, 1024, F32)
    h_re, h_im = _s5_carry(ys, al_re, al_im, n_chunks, batch, ctx_chunks)
    gn2 = 2 * SSM_GROUPS * SSM_STATE
    y = _s5_readout(h_re.reshape(-1, gn2), h_im.reshape(-1, gn2), wc_re, wc_im, ys, tm)
    y = jnp.transpose(y.reshape(n_chunks, batch, cw), (1, 0, 2))
    return jnp.concatenate([y[:, :ctx_chunks].reshape(rows.n_ctx_rows, SSM_WIDTH),
                            y[:, ctx_chunks:].reshape(batch * rows.n_lat, SSM_WIDTH)], axis=0)


def _gelu_tanh(y):
    return 0.5 * y * (1.0 + jnp.tanh(math.sqrt(2.0 / math.pi) * (y + 0.044715 * (y * y * y))))


def _out_mlp_kernel(*refs, even):
    if even:
        x_ref, m1_ref, m2_ref, mod_ref, wglu_ref, bglu_ref, wo_ref, g2_ref, w1_ref, w2_ref, o_ref = refs
        y = _gelu_tanh(m2_ref[...])
        z = _dot(y.astype(BF16), wglu_ref[...]) + bglu_ref[...]
        second = (y * jax.nn.sigmoid(z)).astype(BF16)
    else:
        x_ref, m1_ref, m2_ref, mod_ref, wo_ref, g2_ref, w1_ref, w2_ref, o_ref = refs
        second = m2_ref[...]
    m = mod_ref[0]
    mod = lambda j: m[:, j * D_MODEL:(j + 1) * D_MODEL]
    mix = jnp.concatenate([m1_ref[...], second], axis=1)
    x1 = x_ref[...] + mod(2) * _dot(mix, wo_ref[...])
    a = _modulate(x1, g2_ref[...], mod(3), mod(4)).astype(BF16)
    acc = None
    ck = 1024
    for c in range(D_FF // ck):
        h = jnp.maximum(_dot(a, w1_ref[:, c * ck:(c + 1) * ck]), 0.0)
        part = _dot((h * h).astype(BF16), w2_ref[c * ck:(c + 1) * ck, :])
        acc = part if acc is None else acc + part
    o_ref[...] = x1 + mod(5) * acc


def _out_mlp(x, mix1, mix2, modtab, consts, rows, even, latent_only):
    first = rows.ctx_blocks if latent_only else 0
    row_spec = lambda w: pl.BlockSpec((TM, w), lambda i: (i + first, 0))
    n_blocks = rows.n_rows // TM - first
    return pl.pallas_call(
        functools.partial(_out_mlp_kernel, even=even),
        grid=(n_blocks,),
        in_specs=[row_spec(D_MODEL), row_spec(mix1.shape[1]), row_spec(mix2.shape[1]),
                  pl.BlockSpec((1, 1, N_MOD * D_MODEL), lambda i: (rows.mod_index(i + first), 0, 0))]
                 + [_resident(c.shape) for c in consts],
        out_specs=pl.BlockSpec((TM, D_MODEL), lambda i: (i, 0)),
        out_shape=jax.ShapeDtypeStruct((n_blocks * TM, D_MODEL), F32),
        compiler_params=_cparams(1),
        name="out_mlp_even" if even else "out_mlp_odd",
    )(x, mix1, mix2, modtab, *consts)


def _rope_tables(n_lat, rot_dim, head_w, lane_off):
    t = jnp.arange(n_lat)
    grid_r = (t // GRID_W).astype(F32)
    grid_c = (t % GRID_W).astype(F32)
    axis_dim = rot_dim // 2
    freqs = ROPE_BASE ** (-jnp.arange(0, axis_dim, 2, dtype=F32) / axis_dim)
    ang_r = grid_r[:, None] * freqs
    ang_c = grid_c[:, None] * freqs
    ang = jnp.concatenate([ang_r, ang_r, ang_c, ang_c], axis=-1)
    cos, sin = jnp.cos(ang), jnp.sin(ang)
    quarter = rot_dim // 4
    first = (np.arange(rot_dim) // quarter) % 2 == 0
    sin_a = jnp.where(first, -sin, 0.0)
    sin_b = jnp.where(first, 0.0, sin)

    def widen(tab, fill):
        full = jnp.full((n_lat, head_w), fill, F32).at[:, lane_off:lane_off + rot_dim].set(tab)
        full = jnp.concatenate([jnp.full((TM, head_w), fill, F32), full], axis=0)
        return jnp.tile(full, (1, 128 // head_w))

    return widen(cos, 1.0), widen(sin_a, 0.0), widen(sin_b, 0.0)


def _head_sum_matrices(n_heads, head_dim):
    down = np.kron(np.eye(n_heads), np.ones((head_dim, 1)))
    pad = (-n_heads) % 128
    down = np.pad(down, ((0, 0), (0, pad)))
    return jnp.asarray(down, BF16), jnp.asarray(down.T, BF16)


def _pad_heads(w, n_heads, width, pad_to):
    lead = w.shape[:-1]
    w = w.reshape(lead + (n_heads, width))
    w = jnp.pad(w, [(0, 0)] * len(lead) + [(0, 0), (0, pad_to - width)])
    return w.reshape(lead + (n_heads * pad_to,))


def kernel(x, c, ctx, c_ctx, w_mod, b_mod, g_norm1, g_norm2, w_ff1, w_ff2, e_w_in, e_w_out, e_g_q, e_g_k, ssm_lam_re, ssm_lam_im, ssm_log_dt, ssm_b_re, ssm_b_im, ssm_c_re, ssm_c_im, ssm_d, ssm_w_glu, ssm_b_glu, o_w_in, o_w_out, mla_g_cq, mla_g_ckv, mla_w_uq, mla_w_ukv, mla_g_q, mla_g_k, na_g_q, na_g_k, na_rpb):
    batch, n_lat, d = x.shape
    depth = w_mod.shape[0]
    assert d == D_MODEL and ctx.shape[1] == N_CTX and n_lat % GRID_W == 0
    rows = _Rows(batch, n_lat)

    pad_rows = (-(batch + 1)) % 8
    cond = jnp.concatenate([c_ctx[None], c, jnp.zeros((pad_rows, d), F32)], axis=0)
    modtabs = _mod_vectors(cond, w_mod, b_mod).reshape(depth, -1, 1, N_MOD * d)

    xs = jnp.concatenate([ctx.reshape(rows.n_ctx_rows, d), x.reshape(batch * n_lat, d)], axis=0)

    even_tabs = _rope_tables(n_lat, HEAD_DIM, HEAD_DIM, 0)
    mla_tabs = _rope_tables(n_lat, MLA_ROPE, MLA_PAD, MLA_NOPE)
    down_e, up_e = _head_sum_matrices(GQA_Q_HEADS + GQA_KV_HEADS, HEAD_DIM)
    down_o, up_o = _head_sum_matrices(2 * NA_HEADS, HEAD_DIM)
    row = lambda v: v.reshape(1, -1)
    ones_half = np.concatenate([np.zeros(HEAD_DIM), np.ones(V_EXT - HEAD_DIM)])
    ones_e = jnp.asarray(np.tile(ones_half, GQA_KV_HEADS)[None], F32)
    ones_o = jnp.asarray(np.tile(ones_half, MLA_HEADS)[None], F32)

    for i in range(depth):
        j = i // 2
        last = i == depth - 1
        g1, g2 = row(g_norm1[i]), row(g_norm2[i])
        w1, w2 = w_ff1[i].astype(BF16), w_ff2[i].astype(BF16)
        if i % 2 == 0:
            gqk = jnp.concatenate([jnp.tile(e_g_q[j], GQA_Q_HEADS) * (HEAD_DIM ** -0.5 * LOG2E),
                                   jnp.tile(e_g_k[j], GQA_KV_HEADS)])
            w_in = e_w_in[j]
            nqk = GQA_Q_W + GQA_KV_W
            w_in_p = jnp.concatenate([w_in[:, :nqk],
                                      _pad_heads(w_in[:, nqk:nqk + GQA_KV_W], GQA_KV_HEADS, HEAD_DIM, V_EXT),
                                      w_in[:, nqk + GQA_KV_W:]], axis=1).astype(BF16)
            q, k, v, u = _even_in_proj(xs, modtabs[i], (g1, w_in_p, down_e, up_e, row(gqk), ones_e),
                                       even_tabs, rows)
            att = _attention(q, k, v, rows, n_kv=GQA_KV_HEADS, group=GQA_GROUP, dk=HEAD_DIM, tq=TQ, online=True,
                             skip_ctx_queries=last)
            mats = _s5_matrices(ssm_lam_re[j], ssm_lam_im[j], ssm_log_dt[j], ssm_b_re[j], ssm_b_im[j],
                                ssm_c_re[j], ssm_c_im[j], ssm_d[j])
            y = _s5_mixer(u, mats, rows)
            consts = (ssm_w_glu[j].astype(BF16), row(ssm_b_glu[j]), e_w_out[j].astype(BF16), g2, w1, w2)
            xs = _out_mlp(xs, att, y, modtabs[i], consts, rows, True, last)
        else:
            w_in = o_w_in[j]
            c1 = MLA_Q_RANK
            c2 = c1 + MLA_KV_RANK
            c3 = c2 + MLA_ROPE
            kr_cols = jnp.pad(w_in[:, c2:c3], ((0, 0), (MLA_NOPE, MLA_PAD - MLA_QK)))
            w_in_p = jnp.concatenate([w_in[:, :c2], kr_cols, w_in[:, c3:]], axis=1).astype(BF16)
            wuq = _pad_heads(mla_w_uq[j], MLA_HEADS, MLA_QK, MLA_PAD).astype(BF16)
            wukv = mla_w_ukv[j].reshape(MLA_KV_RANK, MLA_HEADS, MLA_NOPE + MLA_V)
            wuk = _pad_heads(wukv[:, :, :MLA_NOPE].reshape(MLA_KV_RANK, -1), MLA_HEADS, MLA_NOPE, MLA_PAD).astype(BF16)
            wuv = _pad_heads(wukv[:, :, MLA_NOPE:].reshape(MLA_KV_RANK, -1), MLA_HEADS, MLA_V, V_EXT).astype(BF16)
            gmq = _pad_heads(jnp.tile(mla_g_q[j], MLA_HEADS) * (MLA_QK ** -0.5 * LOG2E), MLA_HEADS, MLA_QK, MLA_PAD)
            gmk = _pad_heads(jnp.tile(mla_g_k[j], MLA_HEADS), MLA_HEADS, MLA_QK, MLA_PAD)
            gnqk = jnp.concatenate([jnp.tile(na_g_q[j], NA_HEADS) * (HEAD_DIM ** -0.5 * LOG2E),
                                    jnp.tile(na_g_k[j], NA_HEADS)])
            consts = (g1, w_in_p, row(mla_g_cq[j]), row(mla_g_ckv[j]), wuq, wuk, wuv, row(gmq), row(gmk),
                      down_o, up_o, row(gnqk), ones_o)
            mq, mk, mv, nq, nk, nv = _odd_in_proj(xs, modtabs[i], consts, mla_tabs, rows)
            mla = _attention(mq, mk, mv, rows, n_kv=MLA_HEADS, group=1, dk=MLA_PAD, tq=TQ, online=False,
                             skip_ctx_queries=last)
            bias = _na_bias_table(na_rpb[j], n_lat // GRID_W)
            na = _neighbourhood_attention(nq, nk, nv, bias, rows, last)
            consts = (o_w_out[j].astype(BF16), g2, w1, w2)
            xs = _out_mlp(xs, mla, na, modtabs[i], consts, rows, False, last)
    return xs.reshape(batch, n_lat, d)
```

```python
import functools
import math

import numpy as np
import jax
import jax.numpy as jnp
from jax import lax
from jax.experimental import pallas as pl
from jax.experimental.pallas import tpu as pltpu

F32 = jnp.float32
BF16 = jnp.bfloat16

D_MODEL = 1024
GRID_W = 64
HEAD_DIM = 64
ROPE_BASE = 10000.0
EPS = 1e-6
N_MOD = 6
D_FF = 4 * D_MODEL
LOG2E = math.log2(math.e)

GQA_Q_HEADS = 12
GQA_KV_HEADS = 4
GQA_GROUP = GQA_Q_HEADS // GQA_KV_HEADS
GQA_Q_W = GQA_Q_HEADS * HEAD_DIM
GQA_KV_W = GQA_KV_HEADS * HEAD_DIM
SSM_WIDTH = 256
SSM_GROUP = 16
SSM_GROUPS = SSM_WIDTH // SSM_GROUP
SSM_STATE = 64
SSM_CHUNK = 8

MLA_HEADS = 8
MLA_Q_RANK = 512
MLA_KV_RANK = 256
MLA_NOPE = 64
MLA_ROPE = 32
MLA_QK = MLA_NOPE + MLA_ROPE
MLA_V = 64
MLA_PAD = 128
V_EXT = 128
NA_HEADS = 8
NA_W = NA_HEADS * HEAD_DIM
NA_WIN_R = 8
NA_WIN_C = 16
NA_ROWS_PER_BLOCK = 4

N_CTX = 256
TM = 512
TQ = 256
KEY_CHUNK = 512
MASK_VALUE = -1e30
VMEM_LIMIT = 52 * 1024 * 1024


def _cparams(n_axes):
    return pltpu.CompilerParams(dimension_semantics=("parallel",) * n_axes, vmem_limit_bytes=VMEM_LIMIT)


def _resident(shape):
    nd = len(shape)
    return pl.BlockSpec(shape, lambda *_: (0,) * nd, pipeline_mode=pl.Buffered(1))


def _dot(a, b):
    return jnp.dot(a, b, preferred_element_type=F32)


def _dot_nt(a, b):
    return lax.dot_general(a, b, (((1,), (1,)), ((), ())), preferred_element_type=F32)


def _split_dot(x, w):
    hi = x.astype(BF16)
    lo = (x - hi.astype(F32)).astype(BF16)
    return _dot(hi, w) + _dot(lo, w)


def _modulate(xf, g, shift, scale):
    ms = jnp.mean(xf * xf, axis=-1, keepdims=True)
    return (xf * lax.rsqrt(ms + EPS) * g) * (1.0 + scale) + shift


def _head_rms(x, down, up, head_dim):
    ss = _split_dot(x * x, down)
    inv = lax.rsqrt(ss * (1.0 / head_dim) + EPS)
    return x * _split_dot(inv, up)


def _rope(x, cos, sin_a, sin_b, shift):
    n = x.shape[-1]
    reps = n // cos.shape[-1]
    cos, sin_a, sin_b = (jnp.tile(t, (1, reps)) for t in (cos, sin_a, sin_b))
    return x * cos + pltpu.roll(x, n - shift, 1) * sin_a + pltpu.roll(x, shift, 1) * sin_b


class _Rows:
    def __init__(self, batch, n_lat):
        assert (batch * N_CTX) % TM == 0 and n_lat % TM == 0 and (batch * N_CTX) % n_lat == 0
        self.batch, self.n_lat = batch, n_lat
        self.n_ctx_rows = batch * N_CTX
        self.n_rows = self.n_ctx_rows + batch * n_lat
        self.ctx_blocks = self.n_ctx_rows // TM
        self.lat_blocks = batch * n_lat // TM
        self.blocks_per_batch = n_lat // TM

    def mod_index(self, blk):
        return jnp.where(blk < self.ctx_blocks, 0, 1 + (blk - self.ctx_blocks) // self.blocks_per_batch)

    def rope_index(self, blk):
        return jnp.where(blk < self.ctx_blocks, 0, 1 + (blk - self.ctx_blocks) % self.blocks_per_batch)

    def query_block(self, b, i, tq):
        cs = N_CTX // tq
        return jnp.where(i < cs, b * cs + i, self.n_ctx_rows // tq + b * (self.n_lat // tq) + i - cs)

    def latent_block(self, b):
        return self.n_ctx_rows // self.n_lat + b


def _mod_kernel(c_ref, w_ref, b_ref, o_ref):
    c = c_ref[...]
    s = (c * jax.nn.sigmoid(c)).astype(BF16)
    o_ref[0] = _dot(s, w_ref[0].astype(BF16)) + b_ref[0]


def _mod_vectors(cond, w_mod, b_mod):
    depth, d, n = w_mod.shape
    rows = cond.shape[0]
    tn = 1536
    return pl.pallas_call(
        _mod_kernel,
        grid=(depth, n // tn),
        in_specs=[pl.BlockSpec((rows, d), lambda l, j: (0, 0)),
                  pl.BlockSpec((1, d, tn), lambda l, j: (l, 0, j)),
                  pl.BlockSpec((1, 1, tn), lambda l, j: (l, 0, j))],
        out_specs=pl.BlockSpec((1, rows, tn), lambda l, j: (l, 0, j)),
        out_shape=jax.ShapeDtypeStruct((depth, rows, n), F32),
        compiler_params=_cparams(2),
        name="mod_vectors",
    )(cond, w_mod, b_mod.reshape(depth, 1, n))


def _even_in_kernel(x_ref, mod_ref, g1_ref, w_ref, down_ref, up_ref, gqk_ref, ones_ref, perm_ref,
                    cos_ref, sa_ref, sb_ref, q_ref, k_ref, v_ref, u_ref):
    m = mod_ref[0]
    a = _modulate(x_ref[...], g1_ref[...], m[:, 0:D_MODEL], m[:, D_MODEL:2 * D_MODEL]).astype(BF16)
    h = _dot(a, w_ref[...])
    nqk = GQA_Q_W + GQA_KV_W
    nv = GQA_KV_HEADS * V_EXT
    qk = _head_rms(h[:, :nqk], down_ref[...], up_ref[...], HEAD_DIM) * gqk_ref[...]
    qk = _rope(qk, cos_ref[...], sa_ref[...], sb_ref[...], HEAD_DIM // 4).astype(BF16)
    q_ref[...] = qk[:, :GQA_Q_W]
    k_ref[...] = qk[:, GQA_Q_W:]
    v_ref[...] = (h[:, nqk:nqk + nv] + ones_ref[...]).astype(BF16)
    us = _dot(perm_ref[...], h[:, nqk + nv:].astype(BF16))
    nc = TM // SSM_CHUNK
    u_ref[...] = jnp.concatenate([us[s * nc:(s + 1) * nc] for s in range(SSM_CHUNK)], axis=1).astype(BF16)


def _mod_spec(rows):
    return pl.BlockSpec((1, 1, N_MOD * D_MODEL), lambda i: (rows.mod_index(i), 0, 0))


def _even_in_proj(x, modtab, consts, tabs, rows):
    row_spec = lambda w: pl.BlockSpec((TM, w), lambda i: (i, 0))
    tab_spec = pl.BlockSpec((TM, 128), lambda i: (rows.rope_index(i), 0))
    widths = (GQA_Q_W, GQA_KV_W, GQA_KV_HEADS * V_EXT)
    chunk_rows, chunk_w = TM // SSM_CHUNK, SSM_CHUNK * SSM_WIDTH
    return pl.pallas_call(
        _even_in_kernel,
        grid=(rows.n_rows // TM,),
        in_specs=[row_spec(D_MODEL), _mod_spec(rows)] + [_resident(c.shape) for c in consts] + [tab_spec] * 3,
        out_specs=[row_spec(w) for w in widths] + [pl.BlockSpec((chunk_rows, chunk_w), lambda i: (i, 0))],
        out_shape=[jax.ShapeDtypeStruct((rows.n_rows, w), BF16) for w in widths]
                  + [jax.ShapeDtypeStruct((rows.n_rows // SSM_CHUNK, chunk_w), BF16)],
        compiler_params=_cparams(1),
        name="even_in_proj",
    )(x, modtab, *consts, *tabs)


def _odd_in_kernel(x_ref, mod_ref, g1_ref, w_ref, gcq_ref, gckv_ref, wuq_ref, wuk_ref, wuv_ref,
                   gmq_ref, gmk_ref, down_ref, up_ref, gnqk_ref, ones_ref, cos_ref, sa_ref, sb_ref,
                   mq_ref, mk_ref, mv_ref, nq_ref, nk_ref, nv_ref):
    m = mod_ref[0]
    a = _modulate(x_ref[...], g1_ref[...], m[:, 0:D_MODEL], m[:, D_MODEL:2 * D_MODEL]).astype(BF16)
    h = _dot(a, w_ref[...])
    c1 = MLA_Q_RANK
    c2 = c1 + MLA_KV_RANK
    c3 = c2 + MLA_PAD

    def rms(t, g):
        return (t * lax.rsqrt(jnp.mean(t * t, axis=-1, keepdims=True) + EPS) * g).astype(BF16)

    q = _dot(rms(h[:, :c1], gcq_ref[...]), wuq_ref[...])
    ckv = rms(h[:, c1:c2], gckv_ref[...])
    k = _dot(ckv, wuk_ref[...]) + jnp.tile(h[:, c2:c3], (1, MLA_HEADS))
    mv_ref[...] = (_dot(ckv, wuv_ref[...]) + ones_ref[...]).astype(BF16)

    def mla_heads(t, g):
        parts = []
        for hh in range(MLA_HEADS):
            th = t[:, hh * MLA_PAD:(hh + 1) * MLA_PAD]
            ss = jnp.sum(th * th, axis=-1, keepdims=True)
            parts.append(th * lax.rsqrt(ss * (1.0 / MLA_QK) + EPS))
        t = jnp.concatenate(parts, axis=1) * g
        return _rope(t, cos_ref[...], sa_ref[...], sb_ref[...], MLA_ROPE // 4).astype(BF16)

    mq_ref[...] = mla_heads(q, gmq_ref[...])
    mk_ref[...] = mla_heads(k, gmk_ref[...])

    nqk = _head_rms(h[:, c3:c3 + 2 * NA_W], down_ref[...], up_ref[...], HEAD_DIM) * gnqk_ref[...]
    nqk = nqk.astype(BF16)
    nq_ref[...] = nqk[:, :NA_W]
    nk_ref[...] = nqk[:, NA_W:]
    nv_ref[...] = h[:, c3 + 2 * NA_W:].astype(BF16)


def _odd_in_proj(x, modtab, consts, tabs, rows):
    row_spec = lambda w: pl.BlockSpec((TM, w), lambda i: (i, 0))
    tab_spec = pl.BlockSpec((TM, 128), lambda i: (rows.rope_index(i), 0))
    mla_w = MLA_HEADS * MLA_PAD
    widths = (mla_w, mla_w, MLA_HEADS * V_EXT, NA_W, NA_W, NA_W)
    return pl.pallas_call(
        _odd_in_kernel,
        grid=(rows.n_rows // TM,),
        in_specs=[row_spec(D_MODEL), _mod_spec(rows)] + [_resident(c.shape) for c in consts] + [tab_spec] * 3,
        out_specs=[row_spec(w) for w in widths],
        out_shape=[jax.ShapeDtypeStruct((rows.n_rows, w), BF16) for w in widths],
        compiler_params=_cparams(1),
        name="odd_in_proj",
    )(x, modtab, *consts, *tabs)


def _attn_kernel(q_ref, kc_ref, kl_ref, vc_ref, vl_ref, o_ref, *, n_kv, group, dk, tq, online, first_step):
    n_lat = kl_ref.shape[0]

    def attend(with_latent):
        outs = [None] * (n_kv * group)
        for kv in range(n_kv):
            ksl = slice(kv * dk, (kv + 1) * dk)
            vsl = slice(kv * V_EXT, (kv + 1) * V_EXT)
            qs = [q_ref[:, (kv * group + g) * dk:(kv * group + g + 1) * dk] for g in range(group)]
            qg = jnp.concatenate(qs, axis=0) if group > 1 else qs[0]
            s = _dot_nt(qg, kc_ref[:, ksl])
            m = jnp.max(s, axis=-1, keepdims=True)
            if not with_latent:
                acc = _dot(jnp.exp2(s - m).astype(BF16), vc_ref[:, vsl])
            elif online:
                acc = _dot(jnp.exp2(s - m).astype(BF16), vc_ref[:, vsl])
                for c in range(n_lat // KEY_CHUNK):
                    keys = slice(c * KEY_CHUNK, (c + 1) * KEY_CHUNK)
                    s = _dot_nt(qg, kl_ref[keys, ksl])
                    m_new = jnp.maximum(m, jnp.max(s, axis=-1, keepdims=True))
                    acc = jnp.exp2(m - m_new) * acc + _dot(jnp.exp2(s - m_new).astype(BF16), vl_ref[keys, vsl])
                    m = m_new
            else:
                s_lat = _dot_nt(qg, kl_ref[:, ksl])
                m = jnp.maximum(m, jnp.max(s_lat, axis=-1, keepdims=True))
                acc = (_dot(jnp.exp2(s - m).astype(BF16), vc_ref[:, vsl])
                       + _dot(jnp.exp2(s_lat - m).astype(BF16), vl_ref[:, vsl]))
            o = acc * (1.0 / pltpu.roll(acc, HEAD_DIM, 1))
            for g in range(group):
                outs[kv * group + g] = o[g * tq:(g + 1) * tq, :HEAD_DIM]
        o_ref[...] = jnp.concatenate(outs, axis=1).astype(o_ref.dtype)

    if first_step == 0:
        is_ctx = pl.program_id(1) < N_CTX // tq

        @pl.when(is_ctx)
        def _():
            attend(False)

        @pl.when(jnp.logical_not(is_ctx))
        def _():
            attend(True)
    else:
        attend(True)


def _attention(q, k, v, rows, *, n_kv, group, dk, tq, online, skip_ctx_queries):
    ctx_steps = N_CTX // tq
    first = ctx_steps if skip_ctx_queries else 0
    kern = functools.partial(_attn_kernel, n_kv=n_kv, group=group, dk=dk, tq=tq, online=online, first_step=first)
    qmap = lambda b, i: (rows.query_block(b, i + first, tq), 0)
    wo = n_kv * group * HEAD_DIM
    return pl.pallas_call(
        kern,
        grid=(rows.batch, rows.n_lat // tq + ctx_steps - first),
        in_specs=[pl.BlockSpec((tq, q.shape[1]), qmap),
                  pl.BlockSpec((N_CTX, k.shape[1]), lambda b, i: (b, 0)),
                  pl.BlockSpec((rows.n_lat, k.shape[1]), lambda b, i: (rows.latent_block(b), 0)),
                  pl.BlockSpec((N_CTX, v.shape[1]), lambda b, i: (b, 0)),
                  pl.BlockSpec((rows.n_lat, v.shape[1]), lambda b, i: (rows.latent_block(b), 0))],
        out_specs=pl.BlockSpec((tq, wo), qmap),
        out_shape=jax.ShapeDtypeStruct((rows.n_rows, wo), BF16),
        compiler_params=_cparams(2),
        name="attention_dk%d" % dk,
    )(q, k, k, v, v)


def _na_geometry(grid_rows):
    rb = NA_ROWS_PER_BLOCK
    wr = min(NA_WIN_R, grid_rows)
    span = min(grid_rows, wr + rb - 1)
    starts, variants, keys = [], [], {}
    for r0 in range(0, grid_rows, rb):
        rs0 = int(np.clip(r0 - wr // 2, 0, grid_rows - wr))
        start = min(rs0, grid_rows - span)
        rel = tuple(int(np.clip(r0 + dr - wr // 2, 0, grid_rows - wr)) - (r0 + dr) for dr in range(rb))
        key = (rel, start - r0)
        variants.append(keys.setdefault(key, len(keys)))
        starts.append(start)
    firsts = [variants.index(v) for v in range(len(keys))]
    return wr, span, starts, variants, firsts


def _na_bias_table(rpb, grid_rows):
    rb = NA_ROWS_PER_BLOCK
    wr, span, starts, variants, firsts = _na_geometry(grid_rows)
    nv = len(firsts)
    idx_r = np.zeros((nv, rb, span), np.int32)
    ok_r = np.zeros((nv, rb, span), bool)
    for v, blk in enumerate(firsts):
        r0, start = blk * rb, starts[blk]
        for dr in range(rb):
            r = r0 + dr
            rs = int(np.clip(r - wr // 2, 0, grid_rows - wr))
            for j in range(span):
                kr = start + j
                ok_r[v, dr, j] = rs <= kr < rs + wr
                idx_r[v, dr, j] = np.clip(kr - r + NA_WIN_R - 1, 0, 2 * NA_WIN_R - 2)
    c = np.arange(GRID_W)
    cs = np.clip(c - NA_WIN_C // 2, 0, GRID_W - NA_WIN_C)
    kc = np.arange(GRID_W)
    ok_c = (kc[None, :] >= cs[:, None]) & (kc[None, :] < cs[:, None] + NA_WIN_C)
    idx_c = np.clip(kc[None, :] - c[:, None] + NA_WIN_C - 1, 0, 2 * NA_WIN_C - 2)
    n_rel_c = 2 * NA_WIN_C - 1
    picked = rpb[:, idx_r.reshape(-1)].reshape(rpb.shape[0], nv, rb, span, n_rel_c)
    onehot = jnp.asarray(idx_c[None, :, :] == np.arange(n_rel_c)[:, None, None], F32)
    bias = jnp.einsum("hvdjx,xck->vhdcjk", picked, onehot, precision=lax.Precision.HIGHEST)
    ok = ok_r[:, None, :, None, :, None] & ok_c[None, None, None, :, None, :]
    bias = jnp.where(ok, bias * LOG2E, MASK_VALUE)
    return bias.reshape(nv, rpb.shape[0], rb * GRID_W, span * GRID_W)


def _na_kernel(start_ref, var_ref, q_ref, kc_ref, kl_ref, vc_ref, vl_ref, bias_ref, o_ref, *, n_loc, first_step):
    i = pl.program_id(1) + first_step

    def softmax_pv(s_list, v_list):
        m = jnp.max(s_list[0], axis=-1, keepdims=True)
        for s in s_list[1:]:
            m = jnp.maximum(m, jnp.max(s, axis=-1, keepdims=True))
        ps = [jnp.exp2(s - m) for s in s_list]
        l = sum(jnp.sum(p, axis=-1, keepdims=True) for p in ps)
        o = sum(_dot(p.astype(BF16), v) for p, v in zip(ps, v_list))
        return o / l

    def heads(fn):
        outs = []
        for h in range(NA_HEADS):
            hs = slice(h * HEAD_DIM, (h + 1) * HEAD_DIM)
            outs.append(fn(h, hs, q_ref[:, hs], _dot_nt(q_ref[:, hs], kc_ref[:, hs]), vc_ref[:, hs]))
        o_ref[...] = jnp.concatenate(outs, axis=1).astype(o_ref.dtype)

    def context_queries():
        heads(lambda h, hs, qh, s_ctx, vc: softmax_pv([s_ctx], [vc]))

    def latent_queries():
        off = pl.multiple_of(start_ref[i - 1] * GRID_W, GRID_W)
        var = var_ref[i - 1]

        def one(h, hs, qh, s_ctx, vc):
            s_loc = _dot_nt(qh, kl_ref[pl.ds(off, n_loc), hs]) + bias_ref[var, h]
            return softmax_pv([s_ctx, s_loc], [vc, vl_ref[pl.ds(off, n_loc), hs]])

        heads(one)

    if first_step == 0:
        pl.when(i == 0)(context_queries)
        pl.when(i > 0)(latent_queries)
    else:
        latent_queries()


def _neighbourhood_attention(q, k, v, bias, rows, skip_ctx_queries):
    grid_rows = rows.n_lat // GRID_W
    _, span, starts, variants, _ = _na_geometry(grid_rows)
    assert NA_ROWS_PER_BLOCK * GRID_W == TQ
    first = 1 if skip_ctx_queries else 0
    kern = functools.partial(_na_kernel, n_loc=span * GRID_W, first_step=first)
    qmap = lambda b, i, *_: (rows.query_block(b, i + first, TQ), 0)
    cmap = lambda b, i, *_: (b, 0)
    lmap = lambda b, i, *_: (rows.latent_block(b), 0)
    grid_spec = pltpu.PrefetchScalarGridSpec(
        num_scalar_prefetch=2,
        grid=(rows.batch, rows.n_lat // TQ + 1 - first),
        in_specs=[pl.BlockSpec((TQ, NA_W), qmap),
                  pl.BlockSpec((N_CTX, NA_W), cmap), pl.BlockSpec((rows.n_lat, NA_W), lmap),
                  pl.BlockSpec((N_CTX, NA_W), cmap), pl.BlockSpec((rows.n_lat, NA_W), lmap),
                  pl.BlockSpec(bias.shape, lambda b, i, *_: (0, 0, 0, 0), pipeline_mode=pl.Buffered(1))],
        out_specs=pl.BlockSpec((TQ, NA_W), qmap),
    )
    return pl.pallas_call(
        kern,
        grid_spec=grid_spec,
        out_shape=jax.ShapeDtypeStruct((rows.n_rows, NA_W), BF16),
        compiler_params=_cparams(2),
        name="neighbourhood_attention",
    )(jnp.asarray(starts, jnp.int32), jnp.asarray(variants, jnp.int32), q, k, k, v, v, bias)


def _s5_matrices(lam_re, lam_im, log_dt, b_re, b_im, c_re, c_im, d_skip):
    L, G, P, N = SSM_CHUNK, SSM_GROUPS, SSM_GROUP, SSM_STATE
    hi = lax.Precision.HIGHEST
    dt = jnp.exp(log_dt)[..., None]
    pw = jnp.arange(L + 1, dtype=F32)[:, None, None, None]
    mag = jnp.exp(lam_re * dt * pw)
    e_re = mag * jnp.cos(lam_im * dt * pw)
    e_im = mag * jnp.sin(lam_im * dt * pw)
    a_re, a_im = e_re[1], e_im[1]
    den = jnp.square(lam_re) + jnp.square(lam_im)
    f_re = ((a_re - 1.0) * lam_re + a_im * lam_im) / den
    f_im = (a_im * lam_re - (a_re - 1.0) * lam_im) / den
    bb_re = f_re[..., None] * b_re - f_im[..., None] * b_im
    bb_im = f_re[..., None] * b_im + f_im[..., None] * b_re
    ce_re = c_re[None] * e_re[:, :, :, None, :] - c_im[None] * e_im[:, :, :, None, :]
    ce_im = c_re[None] * e_im[:, :, :, None, :] + c_im[None] * e_re[:, :, :, None, :]
    kk = (jnp.einsum("kdgqn,dgnp->kdgqp", ce_re, bb_re, precision=hi)
          - jnp.einsum("kdgqn,dgnp->kdgqp", ce_im, bb_im, precision=hi))
    lag = np.arange(L)[None, :] - np.arange(L)[:, None]
    kf = kk[np.clip(lag, 0, L - 1), 0]
    kb = kk[np.clip(-lag, 0, L - 1), 1]
    skip = jnp.eye(P, dtype=F32)[None] * d_skip.reshape(G, P)[:, None, :]
    m = lambda cond: jnp.asarray(cond, F32)[:, :, None, None, None]
    kst = m(lag >= 0) * kf + m(lag <= 0) * kb + m(lag == 0) * skip[None, None]
    a_t = jnp.transpose(kst, (0, 2, 4, 1, 3)).reshape(L * G * P, L * P)

    def state_in(pows, d):
        x_re = e_re[pows, d][..., None] * bb_re[d][None] - e_im[pows, d][..., None] * bb_im[d][None]
        x_im = e_re[pows, d][..., None] * bb_im[d][None] + e_im[pows, d][..., None] * bb_re[d][None]
        flat = lambda x: jnp.pad(jnp.transpose(x, (0, 1, 3, 2)).reshape(L * G * P, N), ((0, 0), (0, L * P - N)))
        return [flat(x_re), flat(x_im)]

    a_w1 = jnp.concatenate([a_t] + state_in(np.arange(L)[::-1].copy(), 0) + state_in(np.arange(L), 1), axis=1)
    w1 = _expand_group_blocks(a_w1[None], L * G * P + 4 * G * N, row_shift=4)

    def state_out(x):
        return jnp.transpose(x, (1, 3, 0, 2)).reshape(G * N, L * P)

    pf, pb = np.arange(1, L + 1), np.arange(L, 0, -1)
    a_wc = jnp.stack([jnp.concatenate([state_out(ce_re[pf, 0]), state_out(ce_re[pb, 1])], axis=0),
                      jnp.concatenate([state_out(-ce_im[pf, 0]), state_out(-ce_im[pb, 1])], axis=0)])
    wc = _expand_group_blocks(a_wc, L * G * P, row_shift=6)
    al_re = e_re[L].reshape(2, 1, G * N)
    al_im = e_im[L].reshape(2, 1, G * N)
    return w1[0], wc[0], wc[1], al_re, al_im


def _expand_kernel(a_ref, o_ref, *, row_shift, n_response_tiles):
    j = pl.program_id(1)
    a = a_ref[0].astype(BF16)
    n_rows, tn = o_ref.shape[1], o_ref.shape[2]
    src = lax.broadcasted_iota(jnp.int32, (a.shape[1], tn), 0)
    col = lax.broadcasted_iota(jnp.int32, (a.shape[1], tn), 1)
    row_group = (lax.broadcasted_iota(jnp.int32, (n_rows, tn), 0) >> row_shift) & (SSM_GROUPS - 1)
    out_col = lax.broadcasted_iota(jnp.int32, (n_rows, tn), 1)

    def emit(spread, col_group):
        val = _dot(a, jnp.where(spread, 1.0, 0.0).astype(BF16))
        o_ref[0] = jnp.where(row_group == col_group, val, 0.0).astype(o_ref.dtype)

    @pl.when(j < n_response_tiles)
    def _():
        cg = col + j * tn
        spread = ((src >> 4) == (cg >> 8)) & ((src & 15) == (cg & 15))
        emit(spread, ((out_col + j * tn) >> 4) & (SSM_GROUPS - 1))

    @pl.when(j >= n_response_tiles)
    def _():
        spread = src == (col & (SSM_STATE - 1))
        emit(spread, (out_col >> 6) & (SSM_GROUPS - 1))


def _expand_group_blocks(a, n_cols, *, row_shift):
    k, n_rows, _ = a.shape
    tn = 1024
    n_response_tiles = SSM_CHUNK * SSM_WIDTH // tn
    assert SSM_GROUP == 16 and SSM_STATE == 64 and SSM_CHUNK * SSM_GROUP == 128 and tn % (SSM_GROUPS * SSM_STATE) == 0
    kern = functools.partial(_expand_kernel, row_shift=row_shift, n_response_tiles=n_response_tiles)
    return pl.pallas_call(
        kern,
        grid=(k, n_cols // tn),
        in_specs=[pl.BlockSpec((1, n_rows, 128), lambda d, j: (d, 0, jnp.maximum(j - n_response_tiles + 1, 0)))],
        out_specs=pl.BlockSpec((1, n_rows, tn), lambda d, j: (d, 0, j)),
        out_shape=jax.ShapeDtypeStruct((k, n_rows, n_cols), BF16),
        compiler_params=_cparams(2),
        name="s5_expand",
    )(a)


def _mm_kernel(a_ref, b_ref, o_ref):
    o_ref[...] = _dot(a_ref[...], b_ref[...]).astype(o_ref.dtype)


def _matmul(a, b, tm, tn, out_dtype):
    m, kdim = a.shape
    n = b.shape[1]
    return pl.pallas_call(
        _mm_kernel,
        grid=(n // tn, m // tm),
        in_specs=[pl.BlockSpec((tm, kdim), lambda j, i: (i, 0)),
                  pl.BlockSpec((kdim, tn), lambda j, i: (0, j))],
        out_specs=pl.BlockSpec((tm, tn), lambda j, i: (i, j)),
        out_shape=jax.ShapeDtypeStruct((m, n), out_dtype),
        compiler_params=_cparams(2),
        name="s5_chunk_matmul",
    )(a, b)


def _s5_carry_kernel(sre_ref, sim_ref, are_ref, aim_ref, hre_ref, him_ref, *, ctx_chunks):
    n_chunks = sre_ref.shape[0]
    backward = pl.program_id(0) == 1
    a_re = jnp.broadcast_to(are_ref[0], sre_ref.shape[1:])
    a_im = jnp.broadcast_to(aim_ref[0], sre_ref.shape[1:])

    def step(j, carry):
        h_re, h_im = carry
        cb = jnp.where(j < ctx_chunks, ctx_chunks - 1 - j, n_chunks + ctx_chunks - 1 - j)
        c = jnp.where(backward, cb, j)
        hre_ref[c] = h_re.astype(hre_ref.dtype)
        him_ref[c] = h_im.astype(him_ref.dtype)
        n_re = a_re * h_re - a_im * h_im + sre_ref[c]
        n_im = a_re * h_im + a_im * h_re + sim_ref[c]
        return n_re, n_im

    zero = jnp.zeros(sre_ref.shape[1:], F32)
    lax.fori_loop(0, n_chunks, step, (zero, zero))


def _s5_carry(ys, al_re, al_im, n_chunks, batch, ctx_chunks):
    gn = SSM_GROUPS * SSM_STATE
    tw = 256
    base = SSM_CHUNK * SSM_WIDTH // tw
    per_dir = 2 * gn // tw
    s3 = ys.reshape(n_chunks, batch, ys.shape[1])
    kern = functools.partial(_s5_carry_kernel, ctx_chunks=ctx_chunks)
    blk = (n_chunks, batch, tw)
    return pl.pallas_call(
        kern,
        grid=(2, gn // tw),
        in_specs=[pl.BlockSpec(blk, lambda d, j: (0, 0, base + d * per_dir + j)),
                  pl.BlockSpec(blk, lambda d, j: (0, 0, base + d * per_dir + gn // tw + j)),
                  pl.BlockSpec((1, 1, tw), lambda d, j: (d, 0, j)),
                  pl.BlockSpec((1, 1, tw), lambda d, j: (d, 0, j))],
        out_specs=[pl.BlockSpec(blk, lambda d, j: (0, 0, d * (gn // tw) + j))] * 2,
        out_shape=[jax.ShapeDtypeStruct((n_chunks, batch, 2 * gn), BF16)] * 2,
        compiler_params=_cparams(2),
        name="s5_carry",
    )(s3, s3, al_re, al_im)


def _s5_readout_kernel(hre_ref, him_ref, wre_ref, wim_ref, y_ref, o_ref):
    o_ref[...] = y_ref[...] + _dot(hre_ref[...], wre_ref[...]) + _dot(him_ref[...], wim_ref[...])


def _s5_readout(h_re, h_im, wc_re, wc_im, ys, tm):
    m, kdim = h_re.shape
    n = wc_re.shape[1]
    tn = 1024
    return pl.pallas_call(
        _s5_readout_kernel,
        grid=(n // tn, m // tm),
        in_specs=[pl.BlockSpec((tm, kdim), lambda j, i: (i, 0)),
                  pl.BlockSpec((tm, kdim), lambda j, i: (i, 0)),
                  pl.BlockSpec((kdim, tn), lambda j, i: (0, j)),
                  pl.BlockSpec((kdim, tn), lambda j, i: (0, j)),
                  pl.BlockSpec((tm, tn), lambda j, i: (i, j))],
        out_specs=pl.BlockSpec((tm, tn), lambda j, i: (i, j)),
        out_shape=jax.ShapeDtypeStruct((m, n), F32),
        compiler_params=_cparams(2),
        name="s5_readout",
    )(h_re, h_im, wc_re, wc_im, ys)


def _row_tile(m, target=512):
    t = min(m, target)
    while m % t or t % 16:
        t -= 16
    return t


def _s5_mixer(u, mats, rows):
    w1, wc_re, wc_im, al_re, al_im = mats
    batch = rows.batch
    cw = SSM_CHUNK * SSM_WIDTH
    ctx_chunks = N_CTX // SSM_CHUNK
    n_chunks = ctx_chunks + rows.n_lat // SSM_CHUNK
    n_ctx_chunk_rows = rows.n_ctx_rows // SSM_CHUNK
    uc = jnp.concatenate([u[:n_ctx_chunk_rows].reshape(batch, ctx_chunks, cw),
                          u[n_ctx_chunk_rows:].reshape(batch, n_chunks - ctx_chunks, cw)], axis=1)
    uc = jnp.transpose(uc, (1, 0, 2)).reshape(n_chunks * batch, cw)
    tm = _row_tile(n_chunks * batch)
    ys = _matmul(uc, w1, tm, 1024, F32)
    h_re, h_im = _s5_carry(ys, al_re, al_im, n_chunks, batch, ctx_chunks)
    gn2 = 2 * SSM_GROUPS * SSM_STATE
    y = _s5_readout(h_re.reshape(-1, gn2), h_im.reshape(-1, gn2), wc_re, wc_im, ys, tm)
    y = jnp.transpose(y.reshape(n_chunks, batch, cw), (1, 0, 2))
    return jnp.concatenate([y[:, :ctx_chunks].reshape(n_ctx_chunk_rows, cw),
                            y[:, ctx_chunks:].reshape(-1, cw)], axis=0)


def _gelu_tanh(y):
    return 0.5 * y * (1.0 + jnp.tanh(math.sqrt(2.0 / math.pi) * (y + 0.044715 * (y * y * y))))


def _out_mlp_kernel(*refs, even):
    if even:
        x_ref, m1_ref, m2_ref, mod_ref, perm_ref, wglu_ref, bglu_ref, wo_ref, g2_ref, w1_ref, w2_ref, o_ref = refs
        yc = m2_ref[...]
        ys = jnp.concatenate([yc[:, s * SSM_WIDTH:(s + 1) * SSM_WIDTH] for s in range(SSM_CHUNK)], axis=0)
        hi = ys.astype(BF16)
        lo = (ys - hi.astype(F32)).astype(BF16)
        y = _gelu_tanh(_dot(perm_ref[...], hi) + _dot(perm_ref[...], lo))
        z = _dot(y.astype(BF16), wglu_ref[...]) + bglu_ref[...]
        second = (y * jax.nn.sigmoid(z)).astype(BF16)
    else:
        x_ref, m1_ref, m2_ref, mod_ref, wo_ref, g2_ref, w1_ref, w2_ref, o_ref = refs
        second = m2_ref[...]
    m = mod_ref[0]
    mod = lambda j: m[:, j * D_MODEL:(j + 1) * D_MODEL]
    mix = jnp.concatenate([m1_ref[...], second], axis=1)
    x1 = x_ref[...] + mod(2) * _dot(mix, wo_ref[...])
    a = _modulate(x1, g2_ref[...], mod(3), mod(4)).astype(BF16)
    acc = None
    ck = 1024
    for c in range(D_FF // ck):
        h = jnp.maximum(_dot(a, w1_ref[:, c * ck:(c + 1) * ck]), 0.0)
        part = _dot((h * h).astype(BF16), w2_ref[c * ck:(c + 1) * ck, :])
        acc = part if acc is None else acc + part
    o_ref[...] = x1 + mod(5) * acc


def _out_mlp(x, mix1, mix2, modtab, consts, rows, even, latent_only):
    first = rows.ctx_blocks if latent_only else 0
    row_spec = lambda w: pl.BlockSpec((TM, w), lambda i: (i + first, 0))
    n_blocks = rows.n_rows // TM - first
    mix2_rows = TM // SSM_CHUNK if even else TM
    return pl.pallas_call(
        functools.partial(_out_mlp_kernel, even=even),
        grid=(n_blocks,),
        in_specs=[row_spec(D_MODEL), row_spec(mix1.shape[1]),
                  pl.BlockSpec((mix2_rows, mix2.shape[1]), lambda i: (i + first, 0)),
                  pl.BlockSpec((1, 1, N_MOD * D_MODEL), lambda i: (rows.mod_index(i + first), 0, 0))]
                 + [_resident(c.shape) for c in consts],
        out_specs=pl.BlockSpec((TM, D_MODEL), lambda i: (i, 0)),
        out_shape=jax.ShapeDtypeStruct((n_blocks * TM, D_MODEL), F32),
        compiler_params=_cparams(1),
        name="out_mlp_even" if even else "out_mlp_odd",
    )(x, mix1, mix2, modtab, *consts)


def _rope_tables(n_lat, rot_dim, head_w, lane_off):
    t = jnp.arange(n_lat)
    grid_r = (t // GRID_W).astype(F32)
    grid_c = (t % GRID_W).astype(F32)
    axis_dim = rot_dim // 2
    freqs = ROPE_BASE ** (-jnp.arange(0, axis_dim, 2, dtype=F32) / axis_dim)
    ang_r = grid_r[:, None] * freqs
    ang_c = grid_c[:, None] * freqs
    ang = jnp.concatenate([ang_r, ang_r, ang_c, ang_c], axis=-1)
    cos, sin = jnp.cos(ang), jnp.sin(ang)
    quarter = rot_dim // 4
    first = (np.arange(rot_dim) // quarter) % 2 == 0
    sin_a = jnp.where(first, -sin, 0.0)
    sin_b = jnp.where(first, 0.0, sin)

    def widen(tab, fill):
        full = jnp.full((n_lat, head_w), fill, F32).at[:, lane_off:lane_off + rot_dim].set(tab)
        full = jnp.concatenate([jnp.full((TM, head_w), fill, F32), full], axis=0)
        return jnp.tile(full, (1, 128 // head_w))

    return widen(cos, 1.0), widen(sin_a, 0.0), widen(sin_b, 0.0)


def _head_sum_matrices(n_heads, head_dim):
    down = np.kron(np.eye(n_heads), np.ones((head_dim, 1)))
    pad = (-n_heads) % 128
    down = np.pad(down, ((0, 0), (0, pad)))
    return jnp.asarray(down, BF16), jnp.asarray(down.T, BF16)


def _pad_heads(w, n_heads, width, pad_to):
    lead = w.shape[:-1]
    w = w.reshape(lead + (n_heads, width))
    w = jnp.pad(w, [(0, 0)] * len(lead) + [(0, 0), (0, pad_to - width)])
    return w.reshape(lead + (n_heads * pad_to,))


def kernel(x, c, ctx, c_ctx, w_mod, b_mod, g_norm1, g_norm2, w_ff1, w_ff2, e_w_in, e_w_out, e_g_q, e_g_k, ssm_lam_re, ssm_lam_im, ssm_log_dt, ssm_b_re, ssm_b_im, ssm_c_re, ssm_c_im, ssm_d, ssm_w_glu, ssm_b_glu, o_w_in, o_w_out, mla_g_cq, mla_g_ckv, mla_w_uq, mla_w_ukv, mla_g_q, mla_g_k, na_g_q, na_g_k, na_rpb):
    batch, n_lat, d = x.shape
    depth = w_mod.shape[0]
    assert d == D_MODEL and ctx.shape[1] == N_CTX and n_lat % GRID_W == 0
    rows = _Rows(batch, n_lat)

    pad_rows = (-(batch + 1)) % 8
    cond = jnp.concatenate([c_ctx[None], c, jnp.zeros((pad_rows, d), F32)], axis=0)
    modtabs = _mod_vectors(cond, w_mod, b_mod).reshape(depth, -1, 1, N_MOD * d)

    xs = jnp.concatenate([ctx.reshape(rows.n_ctx_rows, d), x.reshape(batch * n_lat, d)], axis=0)

    even_tabs = _rope_tables(n_lat, HEAD_DIM, HEAD_DIM, 0)
    mla_tabs = _rope_tables(n_lat, MLA_ROPE, MLA_PAD, MLA_NOPE)
    down_e, up_e = _head_sum_matrices(GQA_Q_HEADS + GQA_KV_HEADS, HEAD_DIM)
    down_o, up_o = _head_sum_matrices(2 * NA_HEADS, HEAD_DIM)
    row = lambda v: v.reshape(1, -1)
    ones_half = np.concatenate([np.zeros(HEAD_DIM), np.ones(V_EXT - HEAD_DIM)])
    ones_e = jnp.asarray(np.tile(ones_half, GQA_KV_HEADS)[None], F32)
    ones_o = jnp.asarray(np.tile(ones_half, MLA_HEADS)[None], F32)
    tok = np.arange(TM)
    perm_np = np.zeros((TM, TM), np.float32)
    perm_np[(tok % SSM_CHUNK) * (TM // SSM_CHUNK) + tok // SSM_CHUNK, tok] = 1.0
    perm, perm_t = jnp.asarray(perm_np, BF16), jnp.asarray(perm_np.T, BF16)

    for i in range(depth):
        j = i // 2
        last = i == depth - 1
        g1, g2 = row(g_norm1[i]), row(g_norm2[i])
        w1, w2 = w_ff1[i].astype(BF16), w_ff2[i].astype(BF16)
        if i % 2 == 0:
            gqk = jnp.concatenate([jnp.tile(e_g_q[j], GQA_Q_HEADS) * (HEAD_DIM ** -0.5 * LOG2E),
                                   jnp.tile(e_g_k[j], GQA_KV_HEADS)])
            w_in = e_w_in[j]
            nqk = GQA_Q_W + GQA_KV_W
            w_in_p = jnp.concatenate([w_in[:, :nqk],
                                      _pad_heads(w_in[:, nqk:nqk + GQA_KV_W], GQA_KV_HEADS, HEAD_DIM, V_EXT),
                                      w_in[:, nqk + GQA_KV_W:]], axis=1).astype(BF16)
            q, k, v, u = _even_in_proj(xs, modtabs[i], (g1, w_in_p, down_e, up_e, row(gqk), ones_e, perm),
                                       even_tabs, rows)
            att = _attention(q, k, v, rows, n_kv=GQA_KV_HEADS, group=GQA_GROUP, dk=HEAD_DIM, tq=TQ, online=True,
                             skip_ctx_queries=last)
            mats = _s5_matrices(ssm_lam_re[j], ssm_lam_im[j], ssm_log_dt[j], ssm_b_re[j], ssm_b_im[j],
                                ssm_c_re[j], ssm_c_im[j], ssm_d[j])
            y = _s5_mixer(u, mats, rows)
            consts = (perm_t, ssm_w_glu[j].astype(BF16), row(ssm_b_glu[j]), e_w_out[j].astype(BF16), g2, w1, w2)
            xs = _out_mlp(xs, att, y, modtabs[i], consts, rows, True, last)
        else:
            w_in = o_w_in[j]
            c1 = MLA_Q_RANK
            c2 = c1 + MLA_KV_RANK
            c3 = c2 + MLA_ROPE
            kr_cols = jnp.pad(w_in[:, c2:c3], ((0, 0), (MLA_NOPE, MLA_PAD - MLA_QK)))
            w_in_p = jnp.concatenate([w_in[:, :c2], kr_cols, w_in[:, c3:]], axis=1).astype(BF16)
            wuq = _pad_heads(mla_w_uq[j], MLA_HEADS, MLA_QK, MLA_PAD).astype(BF16)
            wukv = mla_w_ukv[j].reshape(MLA_KV_RANK, MLA_HEADS, MLA_NOPE + MLA_V)
            wuk = _pad_heads(wukv[:, :, :MLA_NOPE].reshape(MLA_KV_RANK, -1), MLA_HEADS, MLA_NOPE, MLA_PAD).astype(BF16)
            wuv = _pad_heads(wukv[:, :, MLA_NOPE:].reshape(MLA_KV_RANK, -1), MLA_HEADS, MLA_V, V_EXT).astype(BF16)
            gmq = _pad_heads(jnp.tile(mla_g_q[j], MLA_HEADS) * (MLA_QK ** -0.5 * LOG2E), MLA_HEADS, MLA_QK, MLA_PAD)
            gmk = _pad_heads(jnp.tile(mla_g_k[j], MLA_HEADS), MLA_HEADS, MLA_QK, MLA_PAD)
            gnqk = jnp.concatenate([jnp.tile(na_g_q[j], NA_HEADS) * (HEAD_DIM ** -0.5 * LOG2E),
                                    jnp.tile(na_g_k[j], NA_HEADS)])
            consts = (g1, w_in_p, row(mla_g_cq[j]), row(mla_g_ckv[j]), wuq, wuk, wuv, row(gmq), row(gmk),
                      down_o, up_o, row(gnqk), ones_o)
            mq, mk, mv, nq, nk, nv = _odd_in_proj(xs, modtabs[i], consts, mla_tabs, rows)
            mla = _attention(mq, mk, mv, rows, n_kv=MLA_HEADS, group=1, dk=MLA_PAD, tq=TQ, online=False,
                             skip_ctx_queries=last)
            bias = _na_bias_table(na_rpb[j], n_lat // GRID_W)
            na = _neighbourhood_attention(nq, nk, nv, bias, rows, last)
            consts = (o_w_out[j].astype(BF16), g2, w1, w2)
            xs = _out_mlp(xs, mla, na, modtabs[i], consts, rows, False, last)
    return xs.reshape(batch, n_lat, d)
```

```python
import functools
import math

import numpy as np
import jax
import jax.numpy as jnp
from jax import lax
from jax.experimental import pallas as pl
from jax.experimental.pallas import tpu as pltpu

F32 = jnp.float32
BF16 = jnp.bfloat16

D_MODEL = 1024
GRID_W = 64
HEAD_DIM = 64
ROPE_BASE = 10000.0
EPS = 1e-6
N_MOD = 6
D_FF = 4 * D_MODEL
LOG2E = math.log2(math.e)

GQA_Q_HEADS = 12
GQA_KV_HEADS = 4
GQA_GROUP = GQA_Q_HEADS // GQA_KV_HEADS
GQA_Q_W = GQA_Q_HEADS * HEAD_DIM
GQA_KV_W = GQA_KV_HEADS * HEAD_DIM
SSM_WIDTH = 256
SSM_GROUP = 16
SSM_GROUPS = SSM_WIDTH // SSM_GROUP
SSM_STATE = 64
SSM_CHUNK = 8

MLA_HEADS = 8
MLA_Q_RANK = 512
MLA_KV_RANK = 256
MLA_NOPE = 64
MLA_ROPE = 32
MLA_QK = MLA_NOPE + MLA_ROPE
MLA_V = 64
MLA_PAD = 128
V_EXT = 128
NA_HEADS = 8
NA_W = NA_HEADS * HEAD_DIM
NA_WIN_R = 8
NA_WIN_C = 16
NA_ROWS_PER_BLOCK = 4

N_CTX = 256
TM = 512
TQ = 256
KEY_CHUNK = 512
MASK_VALUE = -1e30
VMEM_LIMIT = 52 * 1024 * 1024


def _cparams(n_axes):
    return pltpu.CompilerParams(dimension_semantics=("parallel",) * n_axes, vmem_limit_bytes=VMEM_LIMIT)


def _resident(shape):
    nd = len(shape)
    return pl.BlockSpec(shape, lambda *_: (0,) * nd, pipeline_mode=pl.Buffered(1))


def _dot(a, b):
    return jnp.dot(a, b, preferred_element_type=F32)


def _dot_nt(a, b):
    return lax.dot_general(a, b, (((1,), (1,)), ((), ())), preferred_element_type=F32)


def _split_dot(x, w):
    hi = x.astype(BF16)
    lo = (x - hi.astype(F32)).astype(BF16)
    return _dot(hi, w) + _dot(lo, w)


def _modulate(xf, g, shift, scale):
    ms = jnp.mean(xf * xf, axis=-1, keepdims=True)
    return (xf * lax.rsqrt(ms + EPS) * g) * (1.0 + scale) + shift


def _head_rms(x, down, up, head_dim):
    ss = _split_dot(x * x, down)
    inv = lax.rsqrt(ss * (1.0 / head_dim) + EPS)
    return x * _split_dot(inv, up)


def _rope(x, cos, sin_a, sin_b, shift):
    n = x.shape[-1]
    reps = n // cos.shape[-1]
    cos, sin_a, sin_b = (jnp.tile(t, (1, reps)) for t in (cos, sin_a, sin_b))
    return x * cos + pltpu.roll(x, n - shift, 1) * sin_a + pltpu.roll(x, shift, 1) * sin_b


class _Rows:
    def __init__(self, batch, n_lat):
        assert (batch * N_CTX) % TM == 0 and n_lat % TM == 0 and (batch * N_CTX) % n_lat == 0
        self.batch, self.n_lat = batch, n_lat
        self.n_ctx_rows = batch * N_CTX
        self.n_rows = self.n_ctx_rows + batch * n_lat
        self.ctx_blocks = self.n_ctx_rows // TM
        self.lat_blocks = batch * n_lat // TM
        self.blocks_per_batch = n_lat // TM

    def mod_index(self, blk):
        return jnp.where(blk < self.ctx_blocks, 0, 1 + (blk - self.ctx_blocks) // self.blocks_per_batch)

    def rope_index(self, blk):
        return jnp.where(blk < self.ctx_blocks, 0, 1 + (blk - self.ctx_blocks) % self.blocks_per_batch)

    def query_block(self, b, i, tq):
        cs = N_CTX // tq
        return jnp.where(i < cs, b * cs + i, self.n_ctx_rows // tq + b * (self.n_lat // tq) + i - cs)

    def latent_block(self, b):
        return self.n_ctx_rows // self.n_lat + b


def _mod_kernel(c_ref, w_ref, b_ref, o_ref):
    c = c_ref[...]
    s = (c * jax.nn.sigmoid(c)).astype(BF16)
    o_ref[0] = _dot(s, w_ref[0].astype(BF16)) + b_ref[0]


def _mod_vectors(cond, w_mod, b_mod):
    depth, d, n = w_mod.shape
    rows = cond.shape[0]
    tn = 1536
    return pl.pallas_call(
        _mod_kernel,
        grid=(depth, n // tn),
        in_specs=[pl.BlockSpec((rows, d), lambda l, j: (0, 0)),
                  pl.BlockSpec((1, d, tn), lambda l, j: (l, 0, j)),
                  pl.BlockSpec((1, 1, tn), lambda l, j: (l, 0, j))],
        out_specs=pl.BlockSpec((1, rows, tn), lambda l, j: (l, 0, j)),
        out_shape=jax.ShapeDtypeStruct((depth, rows, n), F32),
        compiler_params=_cparams(2),
        name="mod_vectors",
    )(cond, w_mod, b_mod.reshape(depth, 1, n))


def _even_in_kernel(x_ref, mod_ref, g1_ref, w_ref, down_ref, up_ref, gqk_ref, ones_ref, perm_ref,
                    cos_ref, sa_ref, sb_ref, q_ref, k_ref, v_ref, u_ref):
    m = mod_ref[0]
    a = _modulate(x_ref[...], g1_ref[...], m[:, 0:D_MODEL], m[:, D_MODEL:2 * D_MODEL]).astype(BF16)
    h = _dot(a, w_ref[...])
    nqk = GQA_Q_W + GQA_KV_W
    nv = GQA_KV_HEADS * V_EXT
    qk = _head_rms(h[:, :nqk], down_ref[...], up_ref[...], HEAD_DIM) * gqk_ref[...]
    qk = _rope(qk, cos_ref[...], sa_ref[...], sb_ref[...], HEAD_DIM // 4)
    q_ref[...] = qk[:, :GQA_Q_W].T.astype(BF16)
    k_ref[...] = qk[:, GQA_Q_W:].astype(BF16)
    v_ref[...] = (h[:, nqk:nqk + nv] + ones_ref[...]).T.astype(BF16)
    us = _dot(perm_ref[...], h[:, nqk + nv:].astype(BF16))
    nc = TM // SSM_CHUNK
    u_ref[...] = jnp.concatenate([us[s * nc:(s + 1) * nc] for s in range(SSM_CHUNK)], axis=1).astype(BF16)


def _mod_spec(rows):
    return pl.BlockSpec((1, 1, N_MOD * D_MODEL), lambda i: (rows.mod_index(i), 0, 0))


def _even_in_proj(x, modtab, consts, tabs, rows):
    row_spec = lambda w: pl.BlockSpec((TM, w), lambda i: (i, 0))
    tab_spec = pl.BlockSpec((TM, 128), lambda i: (rows.rope_index(i), 0))
    col_spec = lambda w: pl.BlockSpec((w, TM), lambda i: (0, i))
    v_w = GQA_KV_HEADS * V_EXT
    chunk_rows, chunk_w = TM // SSM_CHUNK, SSM_CHUNK * SSM_WIDTH
    sds = jax.ShapeDtypeStruct
    return pl.pallas_call(
        _even_in_kernel,
        grid=(rows.n_rows // TM,),
        in_specs=[row_spec(D_MODEL), _mod_spec(rows)] + [_resident(c.shape) for c in consts] + [tab_spec] * 3,
        out_specs=[col_spec(GQA_Q_W), row_spec(GQA_KV_W), col_spec(v_w),
                   pl.BlockSpec((chunk_rows, chunk_w), lambda i: (i, 0))],
        out_shape=[sds((GQA_Q_W, rows.n_rows), BF16), sds((rows.n_rows, GQA_KV_W), BF16),
                   sds((v_w, rows.n_rows), BF16), sds((rows.n_rows // SSM_CHUNK, chunk_w), BF16)],
        compiler_params=_cparams(1),
        name="even_in_proj",
    )(x, modtab, *consts, *tabs)


def _odd_in_kernel(x_ref, mod_ref, g1_ref, w_ref, gcq_ref, gckv_ref, wuq_ref, wuk_ref, wuv_ref,
                   gmq_ref, gmk_ref, down_ref, up_ref, gnqk_ref, ones_ref, cos_ref, sa_ref, sb_ref,
                   mq_ref, mk_ref, mv_ref, nq_ref, nk_ref, nv_ref):
    m = mod_ref[0]
    a = _modulate(x_ref[...], g1_ref[...], m[:, 0:D_MODEL], m[:, D_MODEL:2 * D_MODEL]).astype(BF16)
    h = _dot(a, w_ref[...])
    c1 = MLA_Q_RANK
    c2 = c1 + MLA_KV_RANK
    c3 = c2 + MLA_PAD

    def rms(t, g):
        return (t * lax.rsqrt(jnp.mean(t * t, axis=-1, keepdims=True) + EPS) * g).astype(BF16)

    q = _dot(rms(h[:, :c1], gcq_ref[...]), wuq_ref[...])
    ckv = rms(h[:, c1:c2], gckv_ref[...])
    k = _dot(ckv, wuk_ref[...]) + jnp.tile(h[:, c2:c3], (1, MLA_HEADS))
    mv_ref[...] = (_dot(ckv, wuv_ref[...]) + ones_ref[...]).T.astype(BF16)

    def mla_heads(t, g):
        parts = []
        for hh in range(MLA_HEADS):
            th = t[:, hh * MLA_PAD:(hh + 1) * MLA_PAD]
            ss = jnp.sum(th * th, axis=-1, keepdims=True)
            parts.append(th * lax.rsqrt(ss * (1.0 / MLA_QK) + EPS))
        t = jnp.concatenate(parts, axis=1) * g
        return _rope(t, cos_ref[...], sa_ref[...], sb_ref[...], MLA_ROPE // 4)

    mq_ref[...] = mla_heads(q, gmq_ref[...]).T.astype(BF16)
    mk_ref[...] = mla_heads(k, gmk_ref[...]).astype(BF16)

    nqk = _head_rms(h[:, c3:c3 + 2 * NA_W], down_ref[...], up_ref[...], HEAD_DIM) * gnqk_ref[...]
    nqk = nqk.astype(BF16)
    nq_ref[...] = nqk[:, :NA_W]
    nk_ref[...] = nqk[:, NA_W:]
    nv_ref[...] = h[:, c3 + 2 * NA_W:].astype(BF16)


def _odd_in_proj(x, modtab, consts, tabs, rows):
    row_spec = lambda w: pl.BlockSpec((TM, w), lambda i: (i, 0))
    tab_spec = pl.BlockSpec((TM, 128), lambda i: (rows.rope_index(i), 0))
    col_spec = lambda w: pl.BlockSpec((w, TM), lambda i: (0, i))
    mla_w = MLA_HEADS * MLA_PAD
    v_w = MLA_HEADS * V_EXT
    rows_out = lambda w: jax.ShapeDtypeStruct((rows.n_rows, w), BF16)
    cols_out = lambda w: jax.ShapeDtypeStruct((w, rows.n_rows), BF16)
    return pl.pallas_call(
        _odd_in_kernel,
        grid=(rows.n_rows // TM,),
        in_specs=[row_spec(D_MODEL), _mod_spec(rows)] + [_resident(c.shape) for c in consts] + [tab_spec] * 3,
        out_specs=[col_spec(mla_w), row_spec(mla_w), col_spec(v_w), row_spec(NA_W), row_spec(NA_W), row_spec(NA_W)],
        out_shape=[cols_out(mla_w), rows_out(mla_w), cols_out(v_w), rows_out(NA_W), rows_out(NA_W), rows_out(NA_W)],
        compiler_params=_cparams(1),
        name="odd_in_proj",
    )(x, modtab, *consts, *tabs)


def _key_max(s, ways=8):
    n = s.shape[0] // ways
    parts = [jnp.max(s[i * n:(i + 1) * n], axis=0, keepdims=True) for i in range(ways)]
    while len(parts) > 1:
        parts = [jnp.maximum(a, b) for a, b in zip(parts[::2], parts[1::2])]
    return parts[0]


def _attn_kernel(qt_ref, kc_ref, kl_ref, vtc_ref, vtl_ref, o_ref, *, n_heads, group, dk, first_step):
    tq = qt_ref.shape[1]
    n_lat = kl_ref.shape[0]
    k_width = kc_ref.shape[1]
    pad_q = k_width <= 256

    def scores(h, with_latent):
        kv = h // group
        qt = qt_ref[h * dk:(h + 1) * dk, :]
        if pad_q:
            pieces = []
            if kv:
                pieces.append(jnp.zeros((kv * dk, tq), qt.dtype))
            pieces.append(qt)
            if k_width - (kv + 1) * dk:
                pieces.append(jnp.zeros((k_width - (kv + 1) * dk, tq), qt.dtype))
            qt = jnp.concatenate(pieces, axis=0)
            ksl = slice(None)
        else:
            ksl = slice(kv * dk, (kv + 1) * dk)
        parts = [_dot(kc_ref[:, ksl], qt)]
        if with_latent:
            parts += [_dot(kl_ref[c * KEY_CHUNK:(c + 1) * KEY_CHUNK, ksl], qt) for c in range(n_lat // KEY_CHUNK)]
        return parts

    def values(h, with_latent):
        vsl = slice((h // group) * V_EXT, (h // group + 1) * V_EXT)
        parts = [vtc_ref[vsl, :]]
        if with_latent:
            parts += [vtl_ref[vsl, c * KEY_CHUNK:(c + 1) * KEY_CHUNK] for c in range(n_lat // KEY_CHUNK)]
        return parts

    def attend(with_latent):
        outs = []
        nxt = scores(0, with_latent)
        for h in range(n_heads):
            s_parts, nxt = nxt, (scores(h + 1, with_latent) if h + 1 < n_heads else None)
            m = functools.reduce(jnp.maximum, [_key_max(s) for s in s_parts])
            acc = None
            for s, vt in zip(s_parts, values(h, with_latent)):
                part = _dot(vt, jnp.exp2(s - m).astype(BF16))
                acc = part if acc is None else acc + part
            outs.append(acc[:HEAD_DIM] / acc[HEAD_DIM:])
        o_ref[...] = jnp.concatenate(outs, axis=0).T.astype(o_ref.dtype)

    if first_step == 0:
        is_ctx = pl.program_id(1) == 0

        @pl.when(is_ctx)
        def _():
            attend(False)

        @pl.when(jnp.logical_not(is_ctx))
        def _():
            attend(True)
    else:
        attend(True)


def _attention(qt, k, vt, rows, *, n_heads, group, dk, skip_ctx_queries):
    first = 1 if skip_ctx_queries else 0
    kern = functools.partial(_attn_kernel, n_heads=n_heads, group=group, dk=dk, first_step=first)
    wo = n_heads * HEAD_DIM
    return pl.pallas_call(
        kern,
        grid=(rows.batch, rows.n_lat // TQ + 1 - first),
        in_specs=[pl.BlockSpec((qt.shape[0], TQ), lambda b, i: (0, rows.query_block(b, i + first, TQ))),
                  pl.BlockSpec((N_CTX, k.shape[1]), lambda b, i: (b, 0)),
                  pl.BlockSpec((rows.n_lat, k.shape[1]), lambda b, i: (rows.latent_block(b), 0)),
                  pl.BlockSpec((vt.shape[0], N_CTX), lambda b, i: (0, b)),
                  pl.BlockSpec((vt.shape[0], rows.n_lat), lambda b, i: (0, rows.latent_block(b)))],
        out_specs=pl.BlockSpec((TQ, wo), lambda b, i: (rows.query_block(b, i + first, TQ), 0)),
        out_shape=jax.ShapeDtypeStruct((rows.n_rows, wo), BF16),
        compiler_params=_cparams(2),
        name="attention_dk%d" % dk,
    )(qt, k, k, vt, vt)


def _na_geometry(grid_rows):
    rb = NA_ROWS_PER_BLOCK
    wr = min(NA_WIN_R, grid_rows)
    span = min(grid_rows, wr + rb - 1)
    starts, variants, keys = [], [], {}
    for r0 in range(0, grid_rows, rb):
        rs0 = int(np.clip(r0 - wr // 2, 0, grid_rows - wr))
        start = min(rs0, grid_rows - span)
        rel = tuple(int(np.clip(r0 + dr - wr // 2, 0, grid_rows - wr)) - (r0 + dr) for dr in range(rb))
        key = (rel, start - r0)
        variants.append(keys.setdefault(key, len(keys)))
        starts.append(start)
    firsts = [variants.index(v) for v in range(len(keys))]
    return wr, span, starts, variants, firsts


def _na_bias_table(rpb, grid_rows):
    rb = NA_ROWS_PER_BLOCK
    wr, span, starts, variants, firsts = _na_geometry(grid_rows)
    nv = len(firsts)
    idx_r = np.zeros((nv, rb, span), np.int32)
    ok_r = np.zeros((nv, rb, span), bool)
    for v, blk in enumerate(firsts):
        r0, start = blk * rb, starts[blk]
        for dr in range(rb):
            r = r0 + dr
            rs = int(np.clip(r - wr // 2, 0, grid_rows - wr))
            for j in range(span):
                kr = start + j
                ok_r[v, dr, j] = rs <= kr < rs + wr
                idx_r[v, dr, j] = np.clip(kr - r + NA_WIN_R - 1, 0, 2 * NA_WIN_R - 2)
    c = np.arange(GRID_W)
    cs = np.clip(c - NA_WIN_C // 2, 0, GRID_W - NA_WIN_C)
    kc = np.arange(GRID_W)
    ok_c = (kc[None, :] >= cs[:, None]) & (kc[None, :] < cs[:, None] + NA_WIN_C)
    idx_c = np.clip(kc[None, :] - c[:, None] + NA_WIN_C - 1, 0, 2 * NA_WIN_C - 2)
    n_rel_c = 2 * NA_WIN_C - 1
    picked = rpb[:, idx_r.reshape(-1)].reshape(rpb.shape[0], nv, rb, span, n_rel_c)
    onehot = jnp.asarray(idx_c[None, :, :] == np.arange(n_rel_c)[:, None, None], F32)
    bias = jnp.einsum("hvdjx,xck->vhdcjk", picked, onehot, precision=lax.Precision.HIGHEST)
    ok = ok_r[:, None, :, None, :, None] & ok_c[None, None, None, :, None, :]
    bias = jnp.where(ok, bias * LOG2E, MASK_VALUE)
    return bias.reshape(nv, rpb.shape[0], rb * GRID_W, span * GRID_W)


def _na_kernel(start_ref, var_ref, q_ref, kc_ref, kl_ref, vc_ref, vl_ref, bias_ref, o_ref, *, n_loc, first_step):
    i = pl.program_id(1) + first_step

    def softmax_pv(s_list, v_list):
        m = jnp.max(s_list[0], axis=-1, keepdims=True)
        for s in s_list[1:]:
            m = jnp.maximum(m, jnp.max(s, axis=-1, keepdims=True))
        ps = [jnp.exp2(s - m) for s in s_list]
        l = sum(jnp.sum(p, axis=-1, keepdims=True) for p in ps)
        o = sum(_dot(p.astype(BF16), v) for p, v in zip(ps, v_list))
        return o / l

    def heads(scores, values):
        outs = []
        nxt = scores(0)
        for h in range(NA_HEADS):
            s_list, nxt = nxt, (scores(h + 1) if h + 1 < NA_HEADS else None)
            outs.append(softmax_pv(s_list, values(h)))
        o_ref[...] = jnp.concatenate(outs, axis=1).astype(o_ref.dtype)

    def head_lanes(h):
        return slice(h * HEAD_DIM, (h + 1) * HEAD_DIM)

    def context_queries():
        heads(lambda h: [_dot_nt(q_ref[:, head_lanes(h)], kc_ref[:, head_lanes(h)])],
              lambda h: [vc_ref[:, head_lanes(h)]])

    def latent_queries():
        off = pl.multiple_of(start_ref[i - 1] * GRID_W, GRID_W)
        var = var_ref[i - 1]

        def scores(h):
            hs = head_lanes(h)
            return [_dot_nt(q_ref[:, hs], kc_ref[:, hs]),
                    _dot_nt(q_ref[:, hs], kl_ref[pl.ds(off, n_loc), hs]) + bias_ref[var, h]]

        heads(scores, lambda h: [vc_ref[:, head_lanes(h)], vl_ref[pl.ds(off, n_loc), head_lanes(h)]])

    if first_step == 0:
        pl.when(i == 0)(context_queries)
        pl.when(i > 0)(latent_queries)
    else:
        latent_queries()


def _neighbourhood_attention(q, k, v, bias, rows, skip_ctx_queries):
    grid_rows = rows.n_lat // GRID_W
    _, span, starts, variants, _ = _na_geometry(grid_rows)
    assert NA_ROWS_PER_BLOCK * GRID_W == TQ
    first = 1 if skip_ctx_queries else 0
    kern = functools.partial(_na_kernel, n_loc=span * GRID_W, first_step=first)
    qmap = lambda b, i, *_: (rows.query_block(b, i + first, TQ), 0)
    cmap = lambda b, i, *_: (b, 0)
    lmap = lambda b, i, *_: (rows.latent_block(b), 0)
    grid_spec = pltpu.PrefetchScalarGridSpec(
        num_scalar_prefetch=2,
        grid=(rows.batch, rows.n_lat // TQ + 1 - first),
        in_specs=[pl.BlockSpec((TQ, NA_W), qmap),
                  pl.BlockSpec((N_CTX, NA_W), cmap), pl.BlockSpec((rows.n_lat, NA_W), lmap),
                  pl.BlockSpec((N_CTX, NA_W), cmap), pl.BlockSpec((rows.n_lat, NA_W), lmap),
                  pl.BlockSpec(bias.shape, lambda b, i, *_: (0, 0, 0, 0), pipeline_mode=pl.Buffered(1))],
        out_specs=pl.BlockSpec((TQ, NA_W), qmap),
    )
    return pl.pallas_call(
        kern,
        grid_spec=grid_spec,
        out_shape=jax.ShapeDtypeStruct((rows.n_rows, NA_W), BF16),
        compiler_params=_cparams(2),
        name="neighbourhood_attention",
    )(jnp.asarray(starts, jnp.int32), jnp.asarray(variants, jnp.int32), q, k, k, v, v, bias)


def _s5_matrices(lam_re, lam_im, log_dt, b_re, b_im, c_re, c_im, d_skip):
    L, G, P, N = SSM_CHUNK, SSM_GROUPS, SSM_GROUP, SSM_STATE
    hi = lax.Precision.HIGHEST
    dt = jnp.exp(log_dt)[..., None]
    pw = jnp.arange(L + 1, dtype=F32)[:, None, None, None]
    mag = jnp.exp(lam_re * dt * pw)
    e_re = mag * jnp.cos(lam_im * dt * pw)
    e_im = mag * jnp.sin(lam_im * dt * pw)
    a_re, a_im = e_re[1], e_im[1]
    den = jnp.square(lam_re) + jnp.square(lam_im)
    f_re = ((a_re - 1.0) * lam_re + a_im * lam_im) / den
    f_im = (a_im * lam_re - (a_re - 1.0) * lam_im) / den
    bb_re = f_re[..., None] * b_re - f_im[..., None] * b_im
    bb_im = f_re[..., None] * b_im + f_im[..., None] * b_re
    ce_re = c_re[None] * e_re[:, :, :, None, :] - c_im[None] * e_im[:, :, :, None, :]
    ce_im = c_re[None] * e_im[:, :, :, None, :] + c_im[None] * e_re[:, :, :, None, :]
    kk = (jnp.einsum("kdgqn,dgnp->kdgqp", ce_re, bb_re, precision=hi)
          - jnp.einsum("kdgqn,dgnp->kdgqp", ce_im, bb_im, precision=hi))
    lag = np.arange(L)[None, :] - np.arange(L)[:, None]
    kf = kk[np.clip(lag, 0, L - 1), 0]
    kb = kk[np.clip(-lag, 0, L - 1), 1]
    skip = jnp.eye(P, dtype=F32)[None] * d_skip.reshape(G, P)[:, None, :]
    m = lambda cond: jnp.asarray(cond, F32)[:, :, None, None, None]
    kst = m(lag >= 0) * kf + m(lag <= 0) * kb + m(lag == 0) * skip[None, None]
    a_t = jnp.transpose(kst, (0, 2, 4, 1, 3)).reshape(L * G * P, L * P)

    def state_in(pows, d):
        x_re = e_re[pows, d][..., None] * bb_re[d][None] - e_im[pows, d][..., None] * bb_im[d][None]
        x_im = e_re[pows, d][..., None] * bb_im[d][None] + e_im[pows, d][..., None] * bb_re[d][None]
        flat = lambda x: jnp.pad(jnp.transpose(x, (0, 1, 3, 2)).reshape(L * G * P, N), ((0, 0), (0, L * P - N)))
        return [flat(x_re), flat(x_im)]

    a_w1 = jnp.concatenate([a_t] + state_in(np.arange(L)[::-1].copy(), 0) + state_in(np.arange(L), 1), axis=1)
    w1 = _expand_group_blocks(a_w1[None], L * G * P + 4 * G * N, row_shift=4)

    def state_out(x):
        return jnp.transpose(x, (1, 3, 0, 2)).reshape(G * N, L * P)

    pf, pb = np.arange(1, L + 1), np.arange(L, 0, -1)
    a_wc = jnp.stack([jnp.concatenate([state_out(ce_re[pf, 0]), state_out(ce_re[pb, 1])], axis=0),
                      jnp.concatenate([state_out(-ce_im[pf, 0]), state_out(-ce_im[pb, 1])], axis=0)])
    wc = _expand_group_blocks(a_wc, L * G * P, row_shift=6)
    al_re = e_re[L].reshape(2, 1, G * N)
    al_im = e_im[L].reshape(2, 1, G * N)
    return w1[0], wc[0], wc[1], al_re, al_im


def _expand_kernel(a_ref, o_ref, *, row_shift, n_response_tiles):
    j = pl.program_id(1)
    a = a_ref[0].astype(BF16)
    n_rows, tn = o_ref.shape[1], o_ref.shape[2]
    src = lax.broadcasted_iota(jnp.int32, (a.shape[1], tn), 0)
    col = lax.broadcasted_iota(jnp.int32, (a.shape[1], tn), 1)
    row_group = (lax.broadcasted_iota(jnp.int32, (n_rows, tn), 0) >> row_shift) & (SSM_GROUPS - 1)
    out_col = lax.broadcasted_iota(jnp.int32, (n_rows, tn), 1)

    def emit(spread, col_group):
        val = _dot(a, jnp.where(spread, 1.0, 0.0).astype(BF16))
        o_ref[0] = jnp.where(row_group == col_group, val, 0.0).astype(o_ref.dtype)

    @pl.when(j < n_response_tiles)
    def _():
        cg = col + j * tn
        spread = ((src >> 4) == (cg >> 8)) & ((src & 15) == (cg & 15))
        emit(spread, ((out_col + j * tn) >> 4) & (SSM_GROUPS - 1))

    @pl.when(j >= n_response_tiles)
    def _():
        spread = src == (col & (SSM_STATE - 1))
        emit(spread, (out_col >> 6) & (SSM_GROUPS - 1))


def _expand_group_blocks(a, n_cols, *, row_shift):
    k, n_rows, _ = a.shape
    tn = 1024
    n_response_tiles = SSM_CHUNK * SSM_WIDTH // tn
    assert SSM_GROUP == 16 and SSM_STATE == 64 and SSM_CHUNK * SSM_GROUP == 128 and tn % (SSM_GROUPS * SSM_STATE) == 0
    kern = functools.partial(_expand_kernel, row_shift=row_shift, n_response_tiles=n_response_tiles)
    return pl.pallas_call(
        kern,
        grid=(k, n_cols // tn),
        in_specs=[pl.BlockSpec((1, n_rows, 128), lambda d, j: (d, 0, jnp.maximum(j - n_response_tiles + 1, 0)))],
        out_specs=pl.BlockSpec((1, n_rows, tn), lambda d, j: (d, 0, j)),
        out_shape=jax.ShapeDtypeStruct((k, n_rows, n_cols), BF16),
        compiler_params=_cparams(2),
        name="s5_expand",
    )(a)


def _mm_kernel(a_ref, b_ref, o_ref):
    o_ref[...] = _dot(a_ref[...], b_ref[...]).astype(o_ref.dtype)


def _matmul(a, b, tm, tn, out_dtype):
    m, kdim = a.shape
    n = b.shape[1]
    return pl.pallas_call(
        _mm_kernel,
        grid=(n // tn, m // tm),
        in_specs=[pl.BlockSpec((tm, kdim), lambda j, i: (i, 0)),
                  pl.BlockSpec((kdim, tn), lambda j, i: (0, j))],
        out_specs=pl.BlockSpec((tm, tn), lambda j, i: (i, j)),
        out_shape=jax.ShapeDtypeStruct((m, n), out_dtype),
        compiler_params=_cparams(2),
        name="s5_chunk_matmul",
    )(a, b)


def _s5_carry_kernel(sre_ref, sim_ref, are_ref, aim_ref, hre_ref, him_ref, *, ctx_chunks):
    n_chunks = sre_ref.shape[0]
    backward = pl.program_id(0) == 1
    a_re = jnp.broadcast_to(are_ref[0], sre_ref.shape[1:])
    a_im = jnp.broadcast_to(aim_ref[0], sre_ref.shape[1:])

    def step(j, carry):
        h_re, h_im = carry
        cb = jnp.where(j < ctx_chunks, ctx_chunks - 1 - j, n_chunks + ctx_chunks - 1 - j)
        c = jnp.where(backward, cb, j)
        hre_ref[c] = h_re.astype(hre_ref.dtype)
        him_ref[c] = h_im.astype(him_ref.dtype)
        n_re = a_re * h_re - a_im * h_im + sre_ref[c]
        n_im = a_re * h_im + a_im * h_re + sim_ref[c]
        return n_re, n_im

    zero = jnp.zeros(sre_ref.shape[1:], F32)
    lax.fori_loop(0, n_chunks, step, (zero, zero))


def _s5_carry(ys, al_re, al_im, n_chunks, batch, ctx_chunks):
    gn = SSM_GROUPS * SSM_STATE
    tw = 256
    base = SSM_CHUNK * SSM_WIDTH // tw
    per_dir = 2 * gn // tw
    s3 = ys.reshape(n_chunks, batch, ys.shape[1])
    kern = functools.partial(_s5_carry_kernel, ctx_chunks=ctx_chunks)
    blk = (n_chunks, batch, tw)
    return pl.pallas_call(
        kern,
        grid=(2, gn // tw),
        in_specs=[pl.BlockSpec(blk, lambda d, j: (0, 0, base + d * per_dir + j)),
                  pl.BlockSpec(blk, lambda d, j: (0, 0, base + d * per_dir + gn // tw + j)),
                  pl.BlockSpec((1, 1, tw), lambda d, j: (d, 0, j)),
                  pl.BlockSpec((1, 1, tw), lambda d, j: (d, 0, j))],
        out_specs=[pl.BlockSpec(blk, lambda d, j: (0, 0, d * (gn // tw) + j))] * 2,
        out_shape=[jax.ShapeDtypeStruct((n_chunks, batch, 2 * gn), BF16)] * 2,
        compiler_params=_cparams(2),
        name="s5_carry",
    )(s3, s3, al_re, al_im)


def _s5_readout_kernel(hre_ref, him_ref, wre_ref, wim_ref, y_ref, o_ref):
    o_ref[...] = y_ref[...] + _dot(hre_ref[...], wre_ref[...]) + _dot(him_ref[...], wim_ref[...])


def _s5_readout(h_re, h_im, wc_re, wc_im, ys, tm):
    m, kdim = h_re.shape
    n = wc_re.shape[1]
    tn = 1024
    return pl.pallas_call(
        _s5_readout_kernel,
        grid=(n // tn, m // tm),
        in_specs=[pl.BlockSpec((tm, kdim), lambda j, i: (i, 0)),
                  pl.BlockSpec((tm, kdim), lambda j, i: (i, 0)),
                  pl.BlockSpec((kdim, tn), lambda j, i: (0, j)),
                  pl.BlockSpec((kdim, tn), lambda j, i: (0, j)),
                  pl.BlockSpec((tm, tn), lambda j, i: (i, j))],
        out_specs=pl.BlockSpec((tm, tn), lambda j, i: (i, j)),
        out_shape=jax.ShapeDtypeStruct((m, n), F32),
        compiler_params=_cparams(2),
        name="s5_readout",
    )(h_re, h_im, wc_re, wc_im, ys)


def _row_tile(m, target=512):
    t = min(m, target)
    while m % t or t % 16:
        t -= 16
    return t


def _s5_mixer(u, mats, rows):
    w1, wc_re, wc_im, al_re, al_im = mats
    batch = rows.batch
    cw = SSM_CHUNK * SSM_WIDTH
    ctx_chunks = N_CTX // SSM_CHUNK
    n_chunks = ctx_chunks + rows.n_lat // SSM_CHUNK
    n_ctx_chunk_rows = rows.n_ctx_rows // SSM_CHUNK
    uc = jnp.concatenate([u[:n_ctx_chunk_rows].reshape(batch, ctx_chunks, cw),
                          u[n_ctx_chunk_rows:].reshape(batch, n_chunks - ctx_chunks, cw)], axis=1)
    uc = jnp.transpose(uc, (1, 0, 2)).reshape(n_chunks * batch, cw)
    tm = _row_tile(n_chunks * batch)
    ys = _matmul(uc, w1, tm, 1024, F32)
    h_re, h_im = _s5_carry(ys, al_re, al_im, n_chunks, batch, ctx_chunks)
    gn2 = 2 * SSM_GROUPS * SSM_STATE
    y = _s5_readout(h_re.reshape(-1, gn2), h_im.reshape(-1, gn2), wc_re, wc_im, ys, tm)
    y = jnp.transpose(y.reshape(n_chunks, batch, cw), (1, 0, 2))
    return jnp.concatenate([y[:, :ctx_chunks].reshape(n_ctx_chunk_rows, cw),
                            y[:, ctx_chunks:].reshape(-1, cw)], axis=0)


def _gelu_tanh(y):
    return 0.5 * y * (1.0 + jnp.tanh(math.sqrt(2.0 / math.pi) * (y + 0.044715 * (y * y * y))))


def _out_mlp_kernel(*refs, even):
    if even:
        x_ref, m1_ref, m2_ref, mod_ref, perm_ref, wglu_ref, bglu_ref, wo_ref, g2_ref, w1_ref, w2_ref, o_ref = refs
        yc = m2_ref[...]
        ys = jnp.concatenate([yc[:, s * SSM_WIDTH:(s + 1) * SSM_WIDTH] for s in range(SSM_CHUNK)], axis=0)
        hi = ys.astype(BF16)
        lo = (ys - hi.astype(F32)).astype(BF16)
        y = _gelu_tanh(_dot(perm_ref[...], hi) + _dot(perm_ref[...], lo))
        z = _dot(y.astype(BF16), wglu_ref[...]) + bglu_ref[...]
        second = (y * jax.nn.sigmoid(z)).astype(BF16)
    else:
        x_ref, m1_ref, m2_ref, mod_ref, wo_ref, g2_ref, w1_ref, w2_ref, o_ref = refs
        second = m2_ref[...]
    m = mod_ref[0]
    mod = lambda j: m[:, j * D_MODEL:(j + 1) * D_MODEL]
    mix = jnp.concatenate([m1_ref[...], second], axis=1)
    x1 = x_ref[...] + mod(2) * _dot(mix, wo_ref[...])
    a = _modulate(x1, g2_ref[...], mod(3), mod(4)).astype(BF16)
    acc = None
    ck = 1024
    for c in range(D_FF // ck):
        h = jnp.maximum(_dot(a, w1_ref[:, c * ck:(c + 1) * ck]), 0.0)
        part = _dot((h * h).astype(BF16), w2_ref[c * ck:(c + 1) * ck, :])
        acc = part if acc is None else acc + part
    o_ref[...] = x1 + mod(5) * acc


def _out_mlp(x, mix1, mix2, modtab, consts, rows, even, latent_only):
    first = rows.ctx_blocks if latent_only else 0
    row_spec = lambda w: pl.BlockSpec((TM, w), lambda i: (i + first, 0))
    n_blocks = rows.n_rows // TM - first
    mix2_rows = TM // SSM_CHUNK if even else TM
    return pl.pallas_call(
        functools.partial(_out_mlp_kernel, even=even),
        grid=(n_blocks,),
        in_specs=[row_spec(D_MODEL), row_spec(mix1.shape[1]),
                  pl.BlockSpec((mix2_rows, mix2.shape[1]), lambda i: (i + first, 0)),
                  pl.BlockSpec((1, 1, N_MOD * D_MODEL), lambda i: (rows.mod_index(i + first), 0, 0))]
                 + [_resident(c.shape) for c in consts],
        out_specs=pl.BlockSpec((TM, D_MODEL), lambda i: (i, 0)),
        out_shape=jax.ShapeDtypeStruct((n_blocks * TM, D_MODEL), F32),
        compiler_params=_cparams(1),
        name="out_mlp_even" if even else "out_mlp_odd",
    )(x, mix1, mix2, modtab, *consts)


def _rope_tables(n_lat, rot_dim, head_w, lane_off):
    t = jnp.arange(n_lat)
    grid_r = (t // GRID_W).astype(F32)
    grid_c = (t % GRID_W).astype(F32)
    axis_dim = rot_dim // 2
    freqs = ROPE_BASE ** (-jnp.arange(0, axis_dim, 2, dtype=F32) / axis_dim)
    ang_r = grid_r[:, None] * freqs
    ang_c = grid_c[:, None] * freqs
    ang = jnp.concatenate([ang_r, ang_r, ang_c, ang_c], axis=-1)
    cos, sin = jnp.cos(ang), jnp.sin(ang)
    quarter = rot_dim // 4
    first = (np.arange(rot_dim) // quarter) % 2 == 0
    sin_a = jnp.where(first, -sin, 0.0)
    sin_b = jnp.where(first, 0.0, sin)

    def widen(tab, fill):
        full = jnp.full((n_lat, head_w), fill, F32).at[:, lane_off:lane_off + rot_dim].set(tab)
        full = jnp.concatenate([jnp.full((TM, head_w), fill, F32), full], axis=0)
        return jnp.tile(full, (1, 128 // head_w))

    return widen(cos, 1.0), widen(sin_a, 0.0), widen(sin_b, 0.0)


def _head_sum_matrices(n_heads, head_dim):
    down = np.kron(np.eye(n_heads), np.ones((head_dim, 1)))
    pad = (-n_heads) % 128
    down = np.pad(down, ((0, 0), (0, pad)))
    return jnp.asarray(down, BF16), jnp.asarray(down.T, BF16)


def _pad_heads(w, n_heads, width, pad_to):
    lead = w.shape[:-1]
    w = w.reshape(lead + (n_heads, width))
    w = jnp.pad(w, [(0, 0)] * len(lead) + [(0, 0), (0, pad_to - width)])
    return w.reshape(lead + (n_heads * pad_to,))


def kernel(x, c, ctx, c_ctx, w_mod, b_mod, g_norm1, g_norm2, w_ff1, w_ff2, e_w_in, e_w_out, e_g_q, e_g_k, ssm_lam_re, ssm_lam_im, ssm_log_dt, ssm_b_re, ssm_b_im, ssm_c_re, ssm_c_im, ssm_d, ssm_w_glu, ssm_b_glu, o_w_in, o_w_out, mla_g_cq, mla_g_ckv, mla_w_uq, mla_w_ukv, mla_g_q, mla_g_k, na_g_q, na_g_k, na_rpb):
    batch, n_lat, d = x.shape
    depth = w_mod.shape[0]
    assert d == D_MODEL and ctx.shape[1] == N_CTX and n_lat % GRID_W == 0
    rows = _Rows(batch, n_lat)

    pad_rows = (-(batch + 1)) % 8
    cond = jnp.concatenate([c_ctx[None], c, jnp.zeros((pad_rows, d), F32)], axis=0)
    modtabs = _mod_vectors(cond, w_mod, b_mod).reshape(depth, -1, 1, N_MOD * d)

    xs = jnp.concatenate([ctx.reshape(rows.n_ctx_rows, d), x.reshape(batch * n_lat, d)], axis=0)

    even_tabs = _rope_tables(n_lat, HEAD_DIM, HEAD_DIM, 0)
    mla_tabs = _rope_tables(n_lat, MLA_ROPE, MLA_PAD, MLA_NOPE)
    down_e, up_e = _head_sum_matrices(GQA_Q_HEADS + GQA_KV_HEADS, HEAD_DIM)
    down_o, up_o = _head_sum_matrices(2 * NA_HEADS, HEAD_DIM)
    row = lambda v: v.reshape(1, -1)
    ones_half = np.concatenate([np.zeros(HEAD_DIM), np.ones(V_EXT - HEAD_DIM)])
    ones_e = jnp.asarray(np.tile(ones_half, GQA_KV_HEADS)[None], F32)
    ones_o = jnp.asarray(np.tile(ones_half, MLA_HEADS)[None], F32)
    tok = np.arange(TM)
    perm_np = np.zeros((TM, TM), np.float32)
    perm_np[(tok % SSM_CHUNK) * (TM // SSM_CHUNK) + tok // SSM_CHUNK, tok] = 1.0
    perm, perm_t = jnp.asarray(perm_np, BF16), jnp.asarray(perm_np.T, BF16)

    for i in range(depth):
        j = i // 2
        last = i == depth - 1
        g1, g2 = row(g_norm1[i]), row(g_norm2[i])
        w1, w2 = w_ff1[i].astype(BF16), w_ff2[i].astype(BF16)
        if i % 2 == 0:
            gqk = jnp.concatenate([jnp.tile(e_g_q[j], GQA_Q_HEADS) * (HEAD_DIM ** -0.5 * LOG2E),
                                   jnp.tile(e_g_k[j], GQA_KV_HEADS)])
            w_in = e_w_in[j]
            nqk = GQA_Q_W + GQA_KV_W
            w_in_p = jnp.concatenate([w_in[:, :nqk],
                                      _pad_heads(w_in[:, nqk:nqk + GQA_KV_W], GQA_KV_HEADS, HEAD_DIM, V_EXT),
                                      w_in[:, nqk + GQA_KV_W:]], axis=1).astype(BF16)
            q, k, v, u = _even_in_proj(xs, modtabs[i], (g1, w_in_p, down_e, up_e, row(gqk), ones_e, perm),
                                       even_tabs, rows)
            att = _attention(q, k, v, rows, n_heads=GQA_Q_HEADS, group=GQA_GROUP, dk=HEAD_DIM, skip_ctx_queries=last)
            mats = _s5_matrices(ssm_lam_re[j], ssm_lam_im[j], ssm_log_dt[j], ssm_b_re[j], ssm_b_im[j],
                                ssm_c_re[j], ssm_c_im[j], ssm_d[j])
            y = _s5_mixer(u, mats, rows)
            consts = (perm_t, ssm_w_glu[j].astype(BF16), row(ssm_b_glu[j]), e_w_out[j].astype(BF16), g2, w1, w2)
            xs = _out_mlp(xs, att, y, modtabs[i], consts, rows, True, last)
        else:
            w_in = o_w_in[j]
            c1 = MLA_Q_RANK
            c2 = c1 + MLA_KV_RANK
            c3 = c2 + MLA_ROPE
            kr_cols = jnp.pad(w_in[:, c2:c3], ((0, 0), (MLA_NOPE, MLA_PAD - MLA_QK)))
            w_in_p = jnp.concatenate([w_in[:, :c2], kr_cols, w_in[:, c3:]], axis=1).astype(BF16)
            wuq = _pad_heads(mla_w_uq[j], MLA_HEADS, MLA_QK, MLA_PAD).astype(BF16)
            wukv = mla_w_ukv[j].reshape(MLA_KV_RANK, MLA_HEADS, MLA_NOPE + MLA_V)
            wuk = _pad_heads(wukv[:, :, :MLA_NOPE].reshape(MLA_KV_RANK, -1), MLA_HEADS, MLA_NOPE, MLA_PAD).astype(BF16)
            wuv = _pad_heads(wukv[:, :, MLA_NOPE:].reshape(MLA_KV_RANK, -1), MLA_HEADS, MLA_V, V_EXT).astype(BF16)
            gmq = _pad_heads(jnp.tile(mla_g_q[j], MLA_HEADS) * (MLA_QK ** -0.5 * LOG2E), MLA_HEADS, MLA_QK, MLA_PAD)
            gmk = _pad_heads(jnp.tile(mla_g_k[j], MLA_HEADS), MLA_HEADS, MLA_QK, MLA_PAD)
            gnqk = jnp.concatenate([jnp.tile(na_g_q[j], NA_HEADS) * (HEAD_DIM ** -0.5 * LOG2E),
                                    jnp.tile(na_g_k[j], NA_HEADS)])
            consts = (g1, w_in_p, row(mla_g_cq[j]), row(mla_g_ckv[j]), wuq, wuk, wuv, row(gmq), row(gmk),
                      down_o, up_o, row(gnqk), ones_o)
            mq, mk, mv, nq, nk, nv = _odd_in_proj(xs, modtabs[i], consts, mla_tabs, rows)
            mla = _attention(mq, mk, mv, rows, n_heads=MLA_HEADS, group=1, dk=MLA_PAD, skip_ctx_queries=last)
            bias = _na_bias_table(na_rpb[j], n_lat // GRID_W)
            na = _neighbourhood_attention(nq, nk, nv, bias, rows, last)
            consts = (o_w_out[j].astype(BF16), g2, w1, w2)
            xs = _out_mlp(xs, mla, na, modtabs[i], consts, rows, False, last)
    return xs.reshape(batch, n_lat, d)
```

```python
import functools
import math

import numpy as np
import jax
import jax.numpy as jnp
from jax import lax
from jax.experimental import pallas as pl
from jax.experimental.pallas import tpu as pltpu

F32 = jnp.float32
BF16 = jnp.bfloat16

D_MODEL = 1024
GRID_W = 64
HEAD_DIM = 64
ROPE_BASE = 10000.0
EPS = 1e-6
N_MOD = 6
D_FF = 4 * D_MODEL
LOG2E = math.log2(math.e)

GQA_Q_HEADS = 12
GQA_KV_HEADS = 4
GQA_GROUP = GQA_Q_HEADS // GQA_KV_HEADS
GQA_Q_W = GQA_Q_HEADS * HEAD_DIM
GQA_KV_W = GQA_KV_HEADS * HEAD_DIM
SSM_WIDTH = 256
SSM_GROUP = 16
SSM_GROUPS = SSM_WIDTH // SSM_GROUP
SSM_STATE = 64
SSM_CHUNK = 8

MLA_HEADS = 8
MLA_Q_RANK = 512
MLA_KV_RANK = 256
MLA_NOPE = 64
MLA_ROPE = 32
MLA_QK = MLA_NOPE + MLA_ROPE
MLA_V = 64
MLA_PAD = 128
V_EXT = 128
NA_HEADS = 8
NA_W = NA_HEADS * HEAD_DIM
NA_WIN_R = 8
NA_WIN_C = 16
NA_ROWS_PER_BLOCK = 4

N_CTX = 256
TM = 512
SUB_ROWS = 256
TQ = 256
KEY_CHUNK = 512
MASK_VALUE = -1e30
VMEM_LIMIT = 52 * 1024 * 1024


def _cparams(n_axes):
    return pltpu.CompilerParams(dimension_semantics=("parallel",) * n_axes, vmem_limit_bytes=VMEM_LIMIT)


def _resident(shape):
    nd = len(shape)
    return pl.BlockSpec(shape, lambda *_: (0,) * nd, pipeline_mode=pl.Buffered(1))


def _dot(a, b):
    return jnp.dot(a, b, preferred_element_type=F32)


def _dot_nt(a, b):
    return lax.dot_general(a, b, (((1,), (1,)), ((), ())), preferred_element_type=F32)


def _split_dot(x, w):
    hi = x.astype(BF16)
    lo = (x - hi.astype(F32)).astype(BF16)
    return _dot(hi, w) + _dot(lo, w)


def _modulate(xf, g, shift, scale):
    ms = jnp.mean(xf * xf, axis=-1, keepdims=True)
    return (xf * lax.rsqrt(ms + EPS) * g) * (1.0 + scale) + shift


def _head_rms(x, down, up, head_dim):
    ss = _split_dot(x * x, down)
    inv = lax.rsqrt(ss * (1.0 / head_dim) + EPS)
    return x * _split_dot(inv, up)


def _rope(x, cos, sin_a, sin_b, shift):
    w = cos.shape[-1]
    blocks = []
    for j in range(x.shape[-1] // w):
        xb = x[:, j * w:(j + 1) * w]
        blocks.append(xb * cos + pltpu.roll(xb, w - shift, 1) * sin_a + pltpu.roll(xb, shift, 1) * sin_b)
    return jnp.concatenate(blocks, axis=1)


def _issue_ahead(project, finish):
    n = TM // SUB_ROWS
    piece = lambda r: slice(r * SUB_ROWS, (r + 1) * SUB_ROWS)
    nxt = project(piece(0))
    for r in range(n):
        h, nxt = nxt, (project(piece(r + 1)) if r + 1 < n else None)
        finish(r, piece(r), h)


class _Rows:
    def __init__(self, batch, n_lat):
        assert (batch * N_CTX) % TM == 0 and n_lat % TM == 0 and (batch * N_CTX) % n_lat == 0
        self.batch, self.n_lat = batch, n_lat
        self.n_ctx_rows = batch * N_CTX
        self.n_rows = self.n_ctx_rows + batch * n_lat
        self.ctx_blocks = self.n_ctx_rows // TM
        self.lat_blocks = batch * n_lat // TM
        self.blocks_per_batch = n_lat // TM

    def mod_index(self, blk):
        return jnp.where(blk < self.ctx_blocks, 0, 1 + (blk - self.ctx_blocks) // self.blocks_per_batch)

    def rope_index(self, blk):
        return jnp.where(blk < self.ctx_blocks, 0, 1 + (blk - self.ctx_blocks) % self.blocks_per_batch)

    def query_block(self, b, i, tq):
        cs = N_CTX // tq
        return jnp.where(i < cs, b * cs + i, self.n_ctx_rows // tq + b * (self.n_lat // tq) + i - cs)

    def latent_block(self, b):
        return self.n_ctx_rows // self.n_lat + b


def _mod_kernel(c_ref, w_ref, b_ref, o_ref):
    c = c_ref[...]
    s = (c * jax.nn.sigmoid(c)).astype(BF16)
    o_ref[0] = _dot(s, w_ref[0].astype(BF16)) + b_ref[0]


def _mod_vectors(cond, w_mod, b_mod):
    depth, d, n = w_mod.shape
    rows = cond.shape[0]
    tn = 1536
    return pl.pallas_call(
        _mod_kernel,
        grid=(depth, n // tn),
        in_specs=[pl.BlockSpec((rows, d), lambda l, j: (0, 0)),
                  pl.BlockSpec((1, d, tn), lambda l, j: (l, 0, j)),
                  pl.BlockSpec((1, 1, tn), lambda l, j: (l, 0, j))],
        out_specs=pl.BlockSpec((1, rows, tn), lambda l, j: (l, 0, j)),
        out_shape=jax.ShapeDtypeStruct((depth, rows, n), F32),
        compiler_params=_cparams(2),
        name="mod_vectors",
    )(cond, w_mod, b_mod.reshape(depth, 1, n))


def _even_in_kernel(x_ref, mod_ref, g1_ref, w_ref, down_ref, up_ref, gqk_ref, ones_ref, perm_ref,
                    cos_ref, sa_ref, sb_ref, q_ref, k_ref, v_ref, u_ref):
    m = mod_ref[0]
    nqk = GQA_Q_W + GQA_KV_W
    nv = GQA_KV_HEADS * V_EXT
    nc = SUB_ROWS // SSM_CHUNK

    def project(rows):
        a = _modulate(x_ref[rows, :], g1_ref[...], m[:, 0:D_MODEL], m[:, D_MODEL:2 * D_MODEL]).astype(BF16)
        return _dot(a, w_ref[...])

    def finish(r, rows, h):
        qk = _head_rms(h[:, :nqk], down_ref[...], up_ref[...], HEAD_DIM) * gqk_ref[...]
        qk = _rope(qk, cos_ref[rows, :], sa_ref[rows, :], sb_ref[rows, :], HEAD_DIM // 4)
        q_ref[:, rows] = qk[:, :GQA_Q_W].T.astype(BF16)
        k_ref[rows, :] = qk[:, GQA_Q_W:].astype(BF16)
        v_ref[:, rows] = (h[:, nqk:nqk + nv] + ones_ref[...]).T.astype(BF16)
        us = _dot(perm_ref[...], h[:, nqk + nv:].astype(BF16))
        u_ref[r * nc:(r + 1) * nc, :] = jnp.concatenate(
            [us[s * nc:(s + 1) * nc] for s in range(SSM_CHUNK)], axis=1).astype(BF16)

    _issue_ahead(project, finish)


def _mod_spec(rows):
    return pl.BlockSpec((1, 1, N_MOD * D_MODEL), lambda i: (rows.mod_index(i), 0, 0))


def _even_in_proj(x, modtab, consts, tabs, rows):
    row_spec = lambda w: pl.BlockSpec((TM, w), lambda i: (i, 0))
    tab_spec = pl.BlockSpec((TM, 128), lambda i: (rows.rope_index(i), 0))
    col_spec = lambda w: pl.BlockSpec((w, TM), lambda i: (0, i))
    v_w = GQA_KV_HEADS * V_EXT
    chunk_rows, chunk_w = TM // SSM_CHUNK, SSM_CHUNK * SSM_WIDTH
    sds = jax.ShapeDtypeStruct
    return pl.pallas_call(
        _even_in_kernel,
        grid=(rows.n_rows // TM,),
        in_specs=[row_spec(D_MODEL), _mod_spec(rows)] + [_resident(c.shape) for c in consts] + [tab_spec] * 3,
        out_specs=[col_spec(GQA_Q_W), row_spec(GQA_KV_W), col_spec(v_w),
                   pl.BlockSpec((chunk_rows, chunk_w), lambda i: (i, 0))],
        out_shape=[sds((GQA_Q_W, rows.n_rows), BF16), sds((rows.n_rows, GQA_KV_W), BF16),
                   sds((v_w, rows.n_rows), BF16), sds((rows.n_rows // SSM_CHUNK, chunk_w), BF16)],
        compiler_params=_cparams(1),
        name="even_in_proj",
    )(x, modtab, *consts, *tabs)


def _odd_in_kernel(x_ref, mod_ref, g1_ref, w_ref, gcq_ref, gckv_ref, wuq_ref, wuk_ref, wuv_ref,
                   gmq_ref, gmk_ref, down_ref, up_ref, gnqk_ref, ones_ref, cos_ref, sa_ref, sb_ref,
                   mq_ref, mk_ref, mv_ref, nq_ref, nk_ref, nv_ref):
    m = mod_ref[0]
    c1 = MLA_Q_RANK
    c2 = c1 + MLA_KV_RANK
    c3 = c2 + MLA_PAD

    def rms(t, g):
        return (t * lax.rsqrt(jnp.mean(t * t, axis=-1, keepdims=True) + EPS) * g).astype(BF16)

    def project(rows):
        a = _modulate(x_ref[rows, :], g1_ref[...], m[:, 0:D_MODEL], m[:, D_MODEL:2 * D_MODEL]).astype(BF16)
        return _dot(a, w_ref[...])

    def finish(r, rows, h):
        q = _dot(rms(h[:, :c1], gcq_ref[...]), wuq_ref[...])
        ckv = rms(h[:, c1:c2], gckv_ref[...])
        k = _dot(ckv, wuk_ref[...]) + jnp.tile(h[:, c2:c3], (1, MLA_HEADS))
        mv_ref[:, rows] = (_dot(ckv, wuv_ref[...]) + ones_ref[...]).T.astype(BF16)

        def mla_heads(t, g):
            parts = []
            for hh in range(MLA_HEADS):
                th = t[:, hh * MLA_PAD:(hh + 1) * MLA_PAD]
                ss = jnp.sum(th * th, axis=-1, keepdims=True)
                parts.append(th * lax.rsqrt(ss * (1.0 / MLA_QK) + EPS))
            t = jnp.concatenate(parts, axis=1) * g
            return _rope(t, cos_ref[rows, :], sa_ref[rows, :], sb_ref[rows, :], MLA_ROPE // 4)

        mq_ref[:, rows] = mla_heads(q, gmq_ref[...]).T.astype(BF16)
        mk_ref[rows, :] = mla_heads(k, gmk_ref[...]).astype(BF16)

        nqk = _head_rms(h[:, c3:c3 + 2 * NA_W], down_ref[...], up_ref[...], HEAD_DIM) * gnqk_ref[...]
        nqk = nqk.astype(BF16)
        nq_ref[rows, :] = nqk[:, :NA_W]
        nk_ref[rows, :] = nqk[:, NA_W:]
        nv_ref[rows, :] = h[:, c3 + 2 * NA_W:].astype(BF16)

    _issue_ahead(project, finish)


def _odd_in_proj(x, modtab, consts, tabs, rows):
    row_spec = lambda w: pl.BlockSpec((TM, w), lambda i: (i, 0))
    tab_spec = pl.BlockSpec((TM, 128), lambda i: (rows.rope_index(i), 0))
    col_spec = lambda w: pl.BlockSpec((w, TM), lambda i: (0, i))
    mla_w = MLA_HEADS * MLA_PAD
    v_w = MLA_HEADS * V_EXT
    rows_out = lambda w: jax.ShapeDtypeStruct((rows.n_rows, w), BF16)
    cols_out = lambda w: jax.ShapeDtypeStruct((w, rows.n_rows), BF16)
    return pl.pallas_call(
        _odd_in_kernel,
        grid=(rows.n_rows // TM,),
        in_specs=[row_spec(D_MODEL), _mod_spec(rows)] + [_resident(c.shape) for c in consts] + [tab_spec] * 3,
        out_specs=[col_spec(mla_w), row_spec(mla_w), col_spec(v_w), row_spec(NA_W), row_spec(NA_W), row_spec(NA_W)],
        out_shape=[cols_out(mla_w), rows_out(mla_w), cols_out(v_w), rows_out(NA_W), rows_out(NA_W), rows_out(NA_W)],
        compiler_params=_cparams(1),
        name="odd_in_proj",
    )(x, modtab, *consts, *tabs)


def _key_max8(s, ways=4):
    n = s.shape[0] // ways
    parts = [jnp.max(s[i * n:(i + 1) * n].reshape(n // 8, 8, s.shape[1]), axis=0) for i in range(ways)]
    while len(parts) > 1:
        parts = [jnp.maximum(a, b) for a, b in zip(parts[::2], parts[1::2])]
    return parts[0]


def _attn_kernel(qt_ref, kc_ref, kl_ref, vtc_ref, vtl_ref, o_ref, *, n_heads, group, dk, first_step):
    tq = qt_ref.shape[1]
    n_lat = kl_ref.shape[0]
    k_width = kc_ref.shape[1]
    pad_q = k_width <= 256

    def scores(h, with_latent):
        kv = h // group
        qt = qt_ref[h * dk:(h + 1) * dk, :]
        if pad_q:
            pieces = []
            if kv:
                pieces.append(jnp.zeros((kv * dk, tq), qt.dtype))
            pieces.append(qt)
            if k_width - (kv + 1) * dk:
                pieces.append(jnp.zeros((k_width - (kv + 1) * dk, tq), qt.dtype))
            qt = jnp.concatenate(pieces, axis=0)
            ksl = slice(None)
        else:
            ksl = slice(kv * dk, (kv + 1) * dk)
        parts = [_dot(kc_ref[:, ksl], qt)]
        if with_latent:
            parts += [_dot(kl_ref[c * KEY_CHUNK:(c + 1) * KEY_CHUNK, ksl], qt) for c in range(n_lat // KEY_CHUNK)]
        return parts

    def values(h, with_latent):
        vsl = slice((h // group) * V_EXT, (h // group + 1) * V_EXT)
        parts = [vtc_ref[vsl, :]]
        if with_latent:
            parts += [vtl_ref[vsl, c * KEY_CHUNK:(c + 1) * KEY_CHUNK] for c in range(n_lat // KEY_CHUNK)]
        return parts

    def probabilities(s_parts):
        m = jnp.max(functools.reduce(jnp.maximum, [_key_max8(s) for s in s_parts]), axis=0, keepdims=True)
        return [jnp.exp2(s - m).astype(BF16) for s in s_parts]

    def weighted_values(h, p_parts, with_latent):
        acc = None
        for p, vt in zip(p_parts, values(h, with_latent)):
            part = _dot(vt, p)
            acc = part if acc is None else acc + part
        return acc[:HEAD_DIM] / acc[HEAD_DIM:]

    def attend(with_latent):
        outs = []
        s_cur, p_prev = scores(0, with_latent), None
        for h in range(n_heads):
            s_next = scores(h + 1, with_latent) if h + 1 < n_heads else None
            p_cur = probabilities(s_cur)
            if p_prev is not None:
                outs.append(weighted_values(h - 1, p_prev, with_latent))
            s_cur, p_prev = s_next, p_cur
        outs.append(weighted_values(n_heads - 1, p_prev, with_latent))
        o_ref[...] = jnp.concatenate(outs, axis=0).T.astype(o_ref.dtype)

    if first_step == 0:
        is_ctx = pl.program_id(1) == 0

        @pl.when(is_ctx)
        def _():
            attend(False)

        @pl.when(jnp.logical_not(is_ctx))
        def _():
            attend(True)
    else:
        attend(True)


def _attention(qt, k, vt, rows, *, n_heads, group, dk, skip_ctx_queries):
    first = 1 if skip_ctx_queries else 0
    kern = functools.partial(_attn_kernel, n_heads=n_heads, group=group, dk=dk, first_step=first)
    wo = n_heads * HEAD_DIM
    return pl.pallas_call(
        kern,
        grid=(rows.batch, rows.n_lat // TQ + 1 - first),
        in_specs=[pl.BlockSpec((qt.shape[0], TQ), lambda b, i: (0, rows.query_block(b, i + first, TQ))),
                  pl.BlockSpec((N_CTX, k.shape[1]), lambda b, i: (b, 0)),
                  pl.BlockSpec((rows.n_lat, k.shape[1]), lambda b, i: (rows.latent_block(b), 0)),
                  pl.BlockSpec((vt.shape[0], N_CTX), lambda b, i: (0, b)),
                  pl.BlockSpec((vt.shape[0], rows.n_lat), lambda b, i: (0, rows.latent_block(b)))],
        out_specs=pl.BlockSpec((TQ, wo), lambda b, i: (rows.query_block(b, i + first, TQ), 0)),
        out_shape=jax.ShapeDtypeStruct((rows.n_rows, wo), BF16),
        compiler_params=_cparams(2),
        name="attention_dk%d" % dk,
    )(qt, k, k, vt, vt)


def _na_geometry(grid_rows):
    rb = NA_ROWS_PER_BLOCK
    wr = min(NA_WIN_R, grid_rows)
    span = min(grid_rows, wr + rb - 1)
    starts, variants, keys = [], [], {}
    for r0 in range(0, grid_rows, rb):
        rs0 = int(np.clip(r0 - wr // 2, 0, grid_rows - wr))
        start = min(rs0, grid_rows - span)
        rel = tuple(int(np.clip(r0 + dr - wr // 2, 0, grid_rows - wr)) - (r0 + dr) for dr in range(rb))
        key = (rel, start - r0)
        variants.append(keys.setdefault(key, len(keys)))
        starts.append(start)
    firsts = [variants.index(v) for v in range(len(keys))]
    return wr, span, starts, variants, firsts


def _na_bias_table(rpb, grid_rows):
    rb = NA_ROWS_PER_BLOCK
    wr, span, starts, variants, firsts = _na_geometry(grid_rows)
    nv = len(firsts)
    idx_r = np.zeros((nv, rb, span), np.int32)
    ok_r = np.zeros((nv, rb, span), bool)
    for v, blk in enumerate(firsts):
        r0, start = blk * rb, starts[blk]
        for dr in range(rb):
            r = r0 + dr
            rs = int(np.clip(r - wr // 2, 0, grid_rows - wr))
            for j in range(span):
                kr = start + j
                ok_r[v, dr, j] = rs <= kr < rs + wr
                idx_r[v, dr, j] = np.clip(kr - r + NA_WIN_R - 1, 0, 2 * NA_WIN_R - 2)
    c = np.arange(GRID_W)
    cs = np.clip(c - NA_WIN_C // 2, 0, GRID_W - NA_WIN_C)
    kc = np.arange(GRID_W)
    ok_c = (kc[None, :] >= cs[:, None]) & (kc[None, :] < cs[:, None] + NA_WIN_C)
    idx_c = np.clip(kc[None, :] - c[:, None] + NA_WIN_C - 1, 0, 2 * NA_WIN_C - 2)
    n_rel_c = 2 * NA_WIN_C - 1
    picked = rpb[:, idx_r.reshape(-1)].reshape(rpb.shape[0], nv, rb, span, n_rel_c)
    picked = jnp.transpose(picked, (1, 0, 2, 3, 4)) * LOG2E
    onehot = jnp.asarray(idx_c[None, :, :] == np.arange(n_rel_c)[:, None, None], F32)
    bias = jnp.concatenate([jnp.einsum("vhdx,xck->vhdck", picked[:, :, :, j], onehot, precision=lax.Precision.HIGHEST)
                            for j in range(span)], axis=-1)
    ok = (ok_r[:, None, :, None, :, None] & ok_c[None, None, None, :, None, :]).reshape(nv, 1, rb, GRID_W, -1)
    bias = jnp.where(ok, bias, MASK_VALUE)
    return bias.reshape(nv, rpb.shape[0], rb * GRID_W, span * GRID_W)


def _na_kernel(start_ref, var_ref, q_ref, kc_ref, kl_ref, vc_ref, vl_ref, bias_ref, o_ref, *, n_loc, first_step):
    i = pl.program_id(1) + first_step

    def probabilities(s_list):
        m = jnp.max(s_list[0], axis=-1, keepdims=True)
        for s in s_list[1:]:
            m = jnp.maximum(m, jnp.max(s, axis=-1, keepdims=True))
        ps = [jnp.exp2(s - m) for s in s_list]
        l = sum(jnp.sum(p, axis=-1, keepdims=True) for p in ps)
        return [p.astype(BF16) for p in ps], l

    def weighted_values(probs, v_list):
        ps, l = probs
        return sum(_dot(p, v) for p, v in zip(ps, v_list)) / l

    def heads(scores, values):
        outs = []
        s_cur, p_prev = scores(0), None
        for h in range(NA_HEADS):
            s_next = scores(h + 1) if h + 1 < NA_HEADS else None
            p_cur = probabilities(s_cur)
            if p_prev is not None:
                outs.append(weighted_values(p_prev, values(h - 1)))
            s_cur, p_prev = s_next, p_cur
        outs.append(weighted_values(p_prev, values(NA_HEADS - 1)))
        o_ref[...] = jnp.concatenate(outs, axis=1).astype(o_ref.dtype)

    def head_lanes(h):
        return slice(h * HEAD_DIM, (h + 1) * HEAD_DIM)

    def context_queries():
        heads(lambda h: [_dot_nt(q_ref[:, head_lanes(h)], kc_ref[:, head_lanes(h)])],
              lambda h: [vc_ref[:, head_lanes(h)]])

    def latent_queries():
        off = pl.multiple_of(start_ref[i - 1] * GRID_W, GRID_W)
        var = var_ref[i - 1]

        def scores(h):
            hs = head_lanes(h)
            return [_dot_nt(q_ref[:, hs], kc_ref[:, hs]),
                    _dot_nt(q_ref[:, hs], kl_ref[pl.ds(off, n_loc), hs]) + bias_ref[var, h]]

        heads(scores, lambda h: [vc_ref[:, head_lanes(h)], vl_ref[pl.ds(off, n_loc), head_lanes(h)]])

    if first_step == 0:
        pl.when(i == 0)(context_queries)
        pl.when(i > 0)(latent_queries)
    else:
        latent_queries()


def _neighbourhood_attention(q, k, v, bias, rows, skip_ctx_queries):
    grid_rows = rows.n_lat // GRID_W
    _, span, starts, variants, _ = _na_geometry(grid_rows)
    assert NA_ROWS_PER_BLOCK * GRID_W == TQ
    first = 1 if skip_ctx_queries else 0
    kern = functools.partial(_na_kernel, n_loc=span * GRID_W, first_step=first)
    qmap = lambda b, i, *_: (rows.query_block(b, i + first, TQ), 0)
    cmap = lambda b, i, *_: (b, 0)
    lmap = lambda b, i, *_: (rows.latent_block(b), 0)
    grid_spec = pltpu.PrefetchScalarGridSpec(
        num_scalar_prefetch=2,
        grid=(rows.batch, rows.n_lat // TQ + 1 - first),
        in_specs=[pl.BlockSpec((TQ, NA_W), qmap),
                  pl.BlockSpec((N_CTX, NA_W), cmap), pl.BlockSpec((rows.n_lat, NA_W), lmap),
                  pl.BlockSpec((N_CTX, NA_W), cmap), pl.BlockSpec((rows.n_lat, NA_W), lmap),
                  pl.BlockSpec(bias.shape, lambda b, i, *_: (0, 0, 0, 0), pipeline_mode=pl.Buffered(1))],
        out_specs=pl.BlockSpec((TQ, NA_W), qmap),
    )
    return pl.pallas_call(
        kern,
        grid_spec=grid_spec,
        out_shape=jax.ShapeDtypeStruct((rows.n_rows, NA_W), BF16),
        compiler_params=_cparams(2),
        name="neighbourhood_attention",
    )(jnp.asarray(starts, jnp.int32), jnp.asarray(variants, jnp.int32), q, k, k, v, v, bias)


def _s5_matrices(lam_re, lam_im, log_dt, b_re, b_im, c_re, c_im, d_skip):
    L, G, P, N = SSM_CHUNK, SSM_GROUPS, SSM_GROUP, SSM_STATE
    hi = lax.Precision.HIGHEST
    dt = jnp.exp(log_dt)[..., None]
    pw = jnp.arange(L + 1, dtype=F32)[:, None, None, None]
    mag = jnp.exp(lam_re * dt * pw)
    e_re = mag * jnp.cos(lam_im * dt * pw)
    e_im = mag * jnp.sin(lam_im * dt * pw)
    a_re, a_im = e_re[1], e_im[1]
    den = jnp.square(lam_re) + jnp.square(lam_im)
    f_re = ((a_re - 1.0) * lam_re + a_im * lam_im) / den
    f_im = (a_im * lam_re - (a_re - 1.0) * lam_im) / den
    bb_re = f_re[..., None] * b_re - f_im[..., None] * b_im
    bb_im = f_re[..., None] * b_im + f_im[..., None] * b_re
    ce_re = c_re[None] * e_re[:, :, :, None, :] - c_im[None] * e_im[:, :, :, None, :]
    ce_im = c_re[None] * e_im[:, :, :, None, :] + c_im[None] * e_re[:, :, :, None, :]
    kk = (jnp.einsum("kdgqn,dgnp->kdgqp", ce_re, bb_re, precision=hi)
          - jnp.einsum("kdgqn,dgnp->kdgqp", ce_im, bb_im, precision=hi))
    lag = np.arange(L)[None, :] - np.arange(L)[:, None]
    kf = kk[np.clip(lag, 0, L - 1), 0]
    kb = kk[np.clip(-lag, 0, L - 1), 1]
    skip = jnp.eye(P, dtype=F32)[None] * d_skip.reshape(G, P)[:, None, :]
    m = lambda cond: jnp.asarray(cond, F32)[:, :, None, None, None]
    kst = m(lag >= 0) * kf + m(lag <= 0) * kb + m(lag == 0) * skip[None, None]
    a_t = jnp.transpose(kst, (0, 2, 4, 1, 3)).reshape(L * G * P, L * P)

    def state_in(pows, d):
        x_re = e_re[pows, d][..., None] * bb_re[d][None] - e_im[pows, d][..., None] * bb_im[d][None]
        x_im = e_re[pows, d][..., None] * bb_im[d][None] + e_im[pows, d][..., None] * bb_re[d][None]
        flat = lambda x: jnp.pad(jnp.transpose(x, (0, 1, 3, 2)).reshape(L * G * P, N), ((0, 0), (0, L * P - N)))
        return [flat(x_re), flat(x_im)]

    a_w1 = jnp.concatenate([a_t] + state_in(np.arange(L)[::-1].copy(), 0) + state_in(np.arange(L), 1), axis=1)
    w1 = _expand_group_blocks(a_w1[None], L * G * P + 4 * G * N, row_shift=4)

    def state_out(x):
        return jnp.transpose(x, (1, 3, 0, 2)).reshape(G * N, L * P)

    pf, pb = np.arange(1, L + 1), np.arange(L, 0, -1)
    a_wc = jnp.stack([jnp.concatenate([state_out(ce_re[pf, 0]), state_out(ce_re[pb, 1])], axis=0),
                      jnp.concatenate([state_out(-ce_im[pf, 0]), state_out(-ce_im[pb, 1])], axis=0)])
    wc = _expand_group_blocks(a_wc, L * G * P, row_shift=6)
    al_re = e_re[L].reshape(2, 1, G * N)
    al_im = e_im[L].reshape(2, 1, G * N)
    return w1[0], wc[0], wc[1], al_re, al_im


def _expand_kernel(a_ref, o_ref, *, row_shift, n_response_tiles):
    j = pl.program_id(1)
    a = a_ref[0].astype(BF16)
    n_rows, tn = o_ref.shape[1], o_ref.shape[2]
    src = lax.broadcasted_iota(jnp.int32, (a.shape[1], tn), 0)
    col = lax.broadcasted_iota(jnp.int32, (a.shape[1], tn), 1)
    row_group = (lax.broadcasted_iota(jnp.int32, (n_rows, tn), 0) >> row_shift) & (SSM_GROUPS - 1)
    out_col = lax.broadcasted_iota(jnp.int32, (n_rows, tn), 1)

    def emit(spread, col_group):
        val = _dot(a, jnp.where(spread, 1.0, 0.0).astype(BF16))
        o_ref[0] = jnp.where(row_group == col_group, val, 0.0).astype(o_ref.dtype)

    @pl.when(j < n_response_tiles)
    def _():
        cg = col + j * tn
        spread = ((src >> 4) == (cg >> 8)) & ((src & 15) == (cg & 15))
        emit(spread, ((out_col + j * tn) >> 4) & (SSM_GROUPS - 1))

    @pl.when(j >= n_response_tiles)
    def _():
        spread = src == (col & (SSM_STATE - 1))
        emit(spread, (out_col >> 6) & (SSM_GROUPS - 1))


def _expand_group_blocks(a, n_cols, *, row_shift):
    k, n_rows, _ = a.shape
    tn = 1024
    n_response_tiles = SSM_CHUNK * SSM_WIDTH // tn
    assert SSM_GROUP == 16 and SSM_STATE == 64 and SSM_CHUNK * SSM_GROUP == 128 and tn % (SSM_GROUPS * SSM_STATE) == 0
    kern = functools.partial(_expand_kernel, row_shift=row_shift, n_response_tiles=n_response_tiles)
    return pl.pallas_call(
        kern,
        grid=(k, n_cols // tn),
        in_specs=[pl.BlockSpec((1, n_rows, 128), lambda d, j: (d, 0, jnp.maximum(j - n_response_tiles + 1, 0)))],
        out_specs=pl.BlockSpec((1, n_rows, tn), lambda d, j: (d, 0, j)),
        out_shape=jax.ShapeDtypeStruct((k, n_rows, n_cols), BF16),
        compiler_params=_cparams(2),
        name="s5_expand",
    )(a)


def _mm_kernel(a_ref, b_ref, o_ref):
    o_ref[...] = _dot(a_ref[...], b_ref[...]).astype(o_ref.dtype)


def _matmul(a, b, tm, tn, out_dtype):
    m, kdim = a.shape
    n = b.shape[1]
    return pl.pallas_call(
        _mm_kernel,
        grid=(n // tn, m // tm),
        in_specs=[pl.BlockSpec((tm, kdim), lambda j, i: (i, 0)),
                  pl.BlockSpec((kdim, tn), lambda j, i: (0, j))],
        out_specs=pl.BlockSpec((tm, tn), lambda j, i: (i, j)),
        out_shape=jax.ShapeDtypeStruct((m, n), out_dtype),
        compiler_params=_cparams(2),
        name="s5_chunk_matmul",
    )(a, b)


def _s5_carry_kernel(sre_ref, sim_ref, are_ref, aim_ref, hre_ref, him_ref, *, ctx_chunks):
    n_chunks = sre_ref.shape[0]
    backward = pl.program_id(0) == 1
    a_re = jnp.broadcast_to(are_ref[0], sre_ref.shape[1:])
    a_im = jnp.broadcast_to(aim_ref[0], sre_ref.shape[1:])

    def step(j, carry):
        h_re, h_im = carry
        cb = jnp.where(j < ctx_chunks, ctx_chunks - 1 - j, n_chunks + ctx_chunks - 1 - j)
        c = jnp.where(backward, cb, j)
        hre_ref[c] = h_re.astype(hre_ref.dtype)
        him_ref[c] = h_im.astype(him_ref.dtype)
        n_re = a_re * h_re - a_im * h_im + sre_ref[c]
        n_im = a_re * h_im + a_im * h_re + sim_ref[c]
        return n_re, n_im

    zero = jnp.zeros(sre_ref.shape[1:], F32)
    lax.fori_loop(0, n_chunks, step, (zero, zero))


def _s5_carry(ys, al_re, al_im, n_chunks, batch, ctx_chunks):
    gn = SSM_GROUPS * SSM_STATE
    tw = 256
    base = SSM_CHUNK * SSM_WIDTH // tw
    per_dir = 2 * gn // tw
    s3 = ys.reshape(n_chunks, batch, ys.shape[1])
    kern = functools.partial(_s5_carry_kernel, ctx_chunks=ctx_chunks)
    blk = (n_chunks, batch, tw)
    return pl.pallas_call(
        kern,
        grid=(2, gn // tw),
        in_specs=[pl.BlockSpec(blk, lambda d, j: (0, 0, base + d * per_dir + j)),
                  pl.BlockSpec(blk, lambda d, j: (0, 0, base + d * per_dir + gn // tw + j)),
                  pl.BlockSpec((1, 1, tw), lambda d, j: (d, 0, j)),
                  pl.BlockSpec((1, 1, tw), lambda d, j: (d, 0, j))],
        out_specs=[pl.BlockSpec(blk, lambda d, j: (0, 0, d * (gn // tw) + j))] * 2,
        out_shape=[jax.ShapeDtypeStruct((n_chunks, batch, 2 * gn), BF16)] * 2,
        compiler_params=_cparams(2),
        name="s5_carry",
    )(s3, s3, al_re, al_im)


def _s5_readout_kernel(hre_ref, him_ref, wre_ref, wim_ref, y_ref, o_ref):
    o_ref[...] = y_ref[...] + _dot(hre_ref[...], wre_ref[...]) + _dot(him_ref[...], wim_ref[...])


def _s5_readout(h_re, h_im, wc_re, wc_im, ys, tm):
    m, kdim = h_re.shape
    n = wc_re.shape[1]
    tn = 1024
    return pl.pallas_call(
        _s5_readout_kernel,
        grid=(n // tn, m // tm),
        in_specs=[pl.BlockSpec((tm, kdim), lambda j, i: (i, 0)),
                  pl.BlockSpec((tm, kdim), lambda j, i: (i, 0)),
                  pl.BlockSpec((kdim, tn), lambda j, i: (0, j)),
                  pl.BlockSpec((kdim, tn), lambda j, i: (0, j)),
                  pl.BlockSpec((tm, tn), lambda j, i: (i, j))],
        out_specs=pl.BlockSpec((tm, tn), lambda j, i: (i, j)),
        out_shape=jax.ShapeDtypeStruct((m, n), F32),
        compiler_params=_cparams(2),
        name="s5_readout",
    )(h_re, h_im, wc_re, wc_im, ys)


def _row_tile(m, target=512):
    t = min(m, target)
    while m % t or t % 16:
        t -= 16
    return t


def _s5_mixer(u, mats, rows):
    w1, wc_re, wc_im, al_re, al_im = mats
    batch = rows.batch
    cw = SSM_CHUNK * SSM_WIDTH
    ctx_chunks = N_CTX // SSM_CHUNK
    n_chunks = ctx_chunks + rows.n_lat // SSM_CHUNK
    n_ctx_chunk_rows = rows.n_ctx_rows // SSM_CHUNK
    uc = jnp.concatenate([u[:n_ctx_chunk_rows].reshape(batch, ctx_chunks, cw),
                          u[n_ctx_chunk_rows:].reshape(batch, n_chunks - ctx_chunks, cw)], axis=1)
    uc = jnp.transpose(uc, (1, 0, 2)).reshape(n_chunks * batch, cw)
    tm = _row_tile(n_chunks * batch)
    ys = _matmul(uc, w1, tm, 1024, F32)
    h_re, h_im = _s5_carry(ys, al_re, al_im, n_chunks, batch, ctx_chunks)
    gn2 = 2 * SSM_GROUPS * SSM_STATE
    y = _s5_readout(h_re.reshape(-1, gn2), h_im.reshape(-1, gn2), wc_re, wc_im, ys, tm)
    y = jnp.transpose(y.reshape(n_chunks, batch, cw), (1, 0, 2))
    return jnp.concatenate([y[:, :ctx_chunks].reshape(n_ctx_chunk_rows, cw),
                            y[:, ctx_chunks:].reshape(-1, cw)], axis=0)


def _gelu_tanh(y):
    return 0.5 * y * (1.0 + jnp.tanh(math.sqrt(2.0 / math.pi) * (y + 0.044715 * (y * y * y))))


def _out_mlp_kernel(*refs, even):
    if even:
        x_ref, m1_ref, m2_ref, mod_ref, perm_ref, wglu_ref, bglu_ref, wo_ref, g2_ref, w1_ref, w2_ref, o_ref = refs
        yc = m2_ref[...]
        ys = jnp.concatenate([yc[:, s * SSM_WIDTH:(s + 1) * SSM_WIDTH] for s in range(SSM_CHUNK)], axis=0)
        hi = ys.astype(BF16)
        lo = (ys - hi.astype(F32)).astype(BF16)
        y = _gelu_tanh(_dot(perm_ref[...], hi) + _dot(perm_ref[...], lo))
        z = _dot(y.astype(BF16), wglu_ref[...]) + bglu_ref[...]
        second = (y * jax.nn.sigmoid(z)).astype(BF16)
    else:
        x_ref, m1_ref, m2_ref, mod_ref, wo_ref, g2_ref, w1_ref, w2_ref, o_ref = refs
        second = m2_ref[...]
    m = mod_ref[0]
    mod = lambda j: m[:, j * D_MODEL:(j + 1) * D_MODEL]
    mix = jnp.concatenate([m1_ref[...], second], axis=1)
    x1 = x_ref[...] + mod(2) * _dot(mix, wo_ref[...])
    a = _modulate(x1, g2_ref[...], mod(3), mod(4)).astype(BF16)
    acc = None
    ck = 1024
    for c in range(D_FF // ck):
        h = jnp.maximum(_dot(a, w1_ref[:, c * ck:(c + 1) * ck]), 0.0)
        part = _dot((h * h).astype(BF16), w2_ref[c * ck:(c + 1) * ck, :])
        acc = part if acc is None else acc + part
    o_ref[...] = x1 + mod(5) * acc


def _out_mlp(x, mix1, mix2, modtab, consts, rows, even, latent_only):
    first = rows.ctx_blocks if latent_only else 0
    row_spec = lambda w: pl.BlockSpec((TM, w), lambda i: (i + first, 0))
    n_blocks = rows.n_rows // TM - first
    mix2_rows = TM // SSM_CHUNK if even else TM
    return pl.pallas_call(
        functools.partial(_out_mlp_kernel, even=even),
        grid=(n_blocks,),
        in_specs=[row_spec(D_MODEL), row_spec(mix1.shape[1]),
                  pl.BlockSpec((mix2_rows, mix2.shape[1]), lambda i: (i + first, 0)),
                  pl.BlockSpec((1, 1, N_MOD * D_MODEL), lambda i: (rows.mod_index(i + first), 0, 0))]
                 + [_resident(c.shape) for c in consts],
        out_specs=pl.BlockSpec((TM, D_MODEL), lambda i: (i, 0)),
        out_shape=jax.ShapeDtypeStruct((n_blocks * TM, D_MODEL), F32),
        compiler_params=_cparams(1),
        name="out_mlp_even" if even else "out_mlp_odd",
    )(x, mix1, mix2, modtab, *consts)


def _rope_tables(n_lat, rot_dim, head_w, lane_off):
    t = jnp.arange(n_lat)
    grid_r = (t // GRID_W).astype(F32)
    grid_c = (t % GRID_W).astype(F32)
    axis_dim = rot_dim // 2
    freqs = ROPE_BASE ** (-jnp.arange(0, axis_dim, 2, dtype=F32) / axis_dim)
    ang_r = grid_r[:, None] * freqs
    ang_c = grid_c[:, None] * freqs
    ang = jnp.concatenate([ang_r, ang_r, ang_c, ang_c], axis=-1)
    cos, sin = jnp.cos(ang), jnp.sin(ang)
    quarter = rot_dim // 4
    first = (np.arange(rot_dim) // quarter) % 2 == 0
    sin_a = jnp.where(first, -sin, 0.0)
    sin_b = jnp.where(first, 0.0, sin)

    def widen(tab, fill):
        full = jnp.full((n_lat, head_w), fill, F32).at[:, lane_off:lane_off + rot_dim].set(tab)
        full = jnp.concatenate([jnp.full((TM, head_w), fill, F32), full], axis=0)
        return jnp.tile(full, (1, 128 // head_w))

    return widen(cos, 1.0), widen(sin_a, 0.0), widen(sin_b, 0.0)


def _head_sum_matrices(n_heads, head_dim):
    down = np.kron(np.eye(n_heads), np.ones((head_dim, 1)))
    pad = (-n_heads) % 128
    down = np.pad(down, ((0, 0), (0, pad)))
    return jnp.asarray(down, BF16), jnp.asarray(down.T, BF16)


def _pad_heads(w, n_heads, width, pad_to):
    lead = w.shape[:-1]
    w = w.reshape(lead + (n_heads, width))
    w = jnp.pad(w, [(0, 0)] * len(lead) + [(0, 0), (0, pad_to - width)])
    return w.reshape(lead + (n_heads * pad_to,))


def kernel(x, c, ctx, c_ctx, w_mod, b_mod, g_norm1, g_norm2, w_ff1, w_ff2, e_w_in, e_w_out, e_g_q, e_g_k, ssm_lam_re, ssm_lam_im, ssm_log_dt, ssm_b_re, ssm_b_im, ssm_c_re, ssm_c_im, ssm_d, ssm_w_glu, ssm_b_glu, o_w_in, o_w_out, mla_g_cq, mla_g_ckv, mla_w_uq, mla_w_ukv, mla_g_q, mla_g_k, na_g_q, na_g_k, na_rpb):
    batch, n_lat, d = x.shape
    depth = w_mod.shape[0]
    assert d == D_MODEL and ctx.shape[1] == N_CTX and n_lat % GRID_W == 0
    rows = _Rows(batch, n_lat)

    pad_rows = (-(batch + 1)) % 8
    cond = jnp.concatenate([c_ctx[None], c, jnp.zeros((pad_rows, d), F32)], axis=0)
    modtabs = _mod_vectors(cond, w_mod, b_mod).reshape(depth, -1, 1, N_MOD * d)

    xs = jnp.concatenate([ctx.reshape(rows.n_ctx_rows, d), x.reshape(batch * n_lat, d)], axis=0)

    even_tabs = _rope_tables(n_lat, HEAD_DIM, HEAD_DIM, 0)
    mla_tabs = _rope_tables(n_lat, MLA_ROPE, MLA_PAD, MLA_NOPE)
    down_e, up_e = _head_sum_matrices(GQA_Q_HEADS + GQA_KV_HEADS, HEAD_DIM)
    down_o, up_o = _head_sum_matrices(2 * NA_HEADS, HEAD_DIM)
    row = lambda v: v.reshape(1, -1)
    ones_half = np.concatenate([np.zeros(HEAD_DIM), np.ones(V_EXT - HEAD_DIM)])
    ones_e = jnp.asarray(np.tile(ones_half, GQA_KV_HEADS)[None], F32)
    ones_o = jnp.asarray(np.tile(ones_half, MLA_HEADS)[None], F32)
    def chunk_perm(n):
        tok = np.arange(n)
        mat = np.zeros((n, n), np.float32)
        mat[(tok % SSM_CHUNK) * (n // SSM_CHUNK) + tok // SSM_CHUNK, tok] = 1.0
        return mat

    perm = jnp.asarray(chunk_perm(SUB_ROWS), BF16)
    perm_t = jnp.asarray(chunk_perm(TM).T, BF16)

    for i in range(depth):
        j = i // 2
        last = i == depth - 1
        g1, g2 = row(g_norm1[i]), row(g_norm2[i])
        w1, w2 = w_ff1[i].astype(BF16), w_ff2[i].astype(BF16)
        if i % 2 == 0:
            gqk = jnp.concatenate([jnp.tile(e_g_q[j], GQA_Q_HEADS) * (HEAD_DIM ** -0.5 * LOG2E),
                                   jnp.tile(e_g_k[j], GQA_KV_HEADS)])
            w_in = e_w_in[j]
            nqk = GQA_Q_W + GQA_KV_W
            w_in_p = jnp.concatenate([w_in[:, :nqk],
                                      _pad_heads(w_in[:, nqk:nqk + GQA_KV_W], GQA_KV_HEADS, HEAD_DIM, V_EXT),
                                      w_in[:, nqk + GQA_KV_W:]], axis=1).astype(BF16)
            q, k, v, u = _even_in_proj(xs, modtabs[i], (g1, w_in_p, down_e, up_e, row(gqk), ones_e, perm),
                                       even_tabs, rows)
            att = _attention(q, k, v, rows, n_heads=GQA_Q_HEADS, group=GQA_GROUP, dk=HEAD_DIM, skip_ctx_queries=last)
            mats = _s5_matrices(ssm_lam_re[j], ssm_lam_im[j], ssm_log_dt[j], ssm_b_re[j], ssm_b_im[j],
                                ssm_c_re[j], ssm_c_im[j], ssm_d[j])
            y = _s5_mixer(u, mats, rows)
            consts = (perm_t, ssm_w_glu[j].astype(BF16), row(ssm_b_glu[j]), e_w_out[j].astype(BF16), g2, w1, w2)
            xs = _out_mlp(xs, att, y, modtabs[i], consts, rows, True, last)
        else:
            w_in = o_w_in[j]
            c1 = MLA_Q_RANK
            c2 = c1 + MLA_KV_RANK
            c3 = c2 + MLA_ROPE
            kr_cols = jnp.pad(w_in[:, c2:c3], ((0, 0), (MLA_NOPE, MLA_PAD - MLA_QK)))
            w_in_p = jnp.concatenate([w_in[:, :c2], kr_cols, w_in[:, c3:]], axis=1).astype(BF16)
            wuq = _pad_heads(mla_w_uq[j], MLA_HEADS, MLA_QK, MLA_PAD).astype(BF16)
            wukv = mla_w_ukv[j].reshape(MLA_KV_RANK, MLA_HEADS, MLA_NOPE + MLA_V)
            wuk = _pad_heads(wukv[:, :, :MLA_NOPE].reshape(MLA_KV_RANK, -1), MLA_HEADS, MLA_NOPE, MLA_PAD).astype(BF16)
            wuv = _pad_heads(wukv[:, :, MLA_NOPE:].reshape(MLA_KV_RANK, -1), MLA_HEADS, MLA_V, V_EXT).astype(BF16)
            gmq = _pad_heads(jnp.tile(mla_g_q[j], MLA_HEADS) * (MLA_QK ** -0.5 * LOG2E), MLA_HEADS, MLA_QK, MLA_PAD)
            gmk = _pad_heads(jnp.tile(mla_g_k[j], MLA_HEADS), MLA_HEADS, MLA_QK, MLA_PAD)
            gnqk = jnp.concatenate([jnp.tile(na_g_q[j], NA_HEADS) * (HEAD_DIM ** -0.5 * LOG2E),
                                    jnp.tile(na_g_k[j], NA_HEADS)])
            consts = (g1, w_in_p, row(mla_g_cq[j]), row(mla_g_ckv[j]), wuq, wuk, wuv, row(gmq), row(gmk),
                      down_o, up_o, row(gnqk), ones_o)
            mq, mk, mv, nq, nk, nv = _odd_in_proj(xs, modtabs[i], consts, mla_tabs, rows)
            mla = _attention(mq, mk, mv, rows, n_heads=MLA_HEADS, group=1, dk=MLA_PAD, skip_ctx_queries=last)
            bias = _na_bias_table(na_rpb[j], n_lat // GRID_W)
            na = _neighbourhood_attention(nq, nk, nv, bias, rows, last)
            consts = (o_w_out[j].astype(BF16), g2, w1, w2)
            xs = _out_mlp(xs, mla, na, modtabs[i], consts, rows, False, last)
    return xs.reshape(batch, n_lat, d)
```

```python
import functools
import math

import numpy as np
import jax
import jax.numpy as jnp
from jax import lax
from jax.experimental import pallas as pl
from jax.experimental.pallas import tpu as pltpu

F32 = jnp.float32
BF16 = jnp.bfloat16

D_MODEL = 1024
GRID_W = 64
HEAD_DIM = 64
ROPE_BASE = 10000.0
EPS = 1e-6
N_MOD = 6
D_FF = 4 * D_MODEL
LOG2E = math.log2(math.e)

GQA_Q_HEADS = 12
GQA_KV_HEADS = 4
GQA_GROUP = GQA_Q_HEADS // GQA_KV_HEADS
GQA_Q_W = GQA_Q_HEADS * HEAD_DIM
GQA_KV_W = GQA_KV_HEADS * HEAD_DIM
SSM_WIDTH = 256
SSM_GROUP = 16
SSM_GROUPS = SSM_WIDTH // SSM_GROUP
SSM_STATE = 64
SSM_CHUNK = 8

MLA_HEADS = 8
MLA_Q_RANK = 512
MLA_KV_RANK = 256
MLA_NOPE = 64
MLA_ROPE = 32
MLA_QK = MLA_NOPE + MLA_ROPE
MLA_V = 64
MLA_PAD = 128
V_EXT = 128
NA_HEADS = 8
NA_W = NA_HEADS * HEAD_DIM
NA_WIN_R = 8
NA_WIN_C = 16
NA_ROWS_PER_BLOCK = 4

N_CTX = 256
TM = 512
SUB_ROWS = 256
TQ = 256
KEY_CHUNK = 512
MASK_VALUE = -1e30
VMEM_LIMIT = 52 * 1024 * 1024


def _cparams(n_axes):
    return pltpu.CompilerParams(dimension_semantics=("parallel",) * n_axes, vmem_limit_bytes=VMEM_LIMIT)


def _resident(shape):
    nd = len(shape)
    return pl.BlockSpec(shape, lambda *_: (0,) * nd, pipeline_mode=pl.Buffered(1))


def _dot(a, b):
    return jnp.dot(a, b, preferred_element_type=F32)


def _dot_nt(a, b):
    return lax.dot_general(a, b, (((1,), (1,)), ((), ())), preferred_element_type=F32)


def _split_dot(x, w):
    hi = x.astype(BF16)
    lo = (x - hi.astype(F32)).astype(BF16)
    return _dot(hi, w) + _dot(lo, w)


def _modulate(xf, g, shift, scale):
    ms = jnp.mean(xf * xf, axis=-1, keepdims=True)
    return (xf * lax.rsqrt(ms + EPS) * g) * (1.0 + scale) + shift


def _head_rms(x, down, up, head_dim):
    ss = _split_dot(x * x, down)
    inv = lax.rsqrt(ss * (1.0 / head_dim) + EPS)
    return x * _split_dot(inv, up)


def _rope(x, cos, sin_a, sin_b, shift):
    w = cos.shape[-1]
    blocks = []
    for j in range(x.shape[-1] // w):
        xb = x[:, j * w:(j + 1) * w]
        blocks.append(xb * cos + pltpu.roll(xb, w - shift, 1) * sin_a + pltpu.roll(xb, shift, 1) * sin_b)
    return jnp.concatenate(blocks, axis=1)


def _issue_ahead(project, finish):
    n = TM // SUB_ROWS
    piece = lambda r: slice(r * SUB_ROWS, (r + 1) * SUB_ROWS)
    nxt = project(piece(0))
    for r in range(n):
        h, nxt = nxt, (project(piece(r + 1)) if r + 1 < n else None)
        finish(r, piece(r), h)


class _Rows:
    def __init__(self, batch, n_lat):
        assert (batch * N_CTX) % TM == 0 and n_lat % TM == 0 and (batch * N_CTX) % n_lat == 0
        self.batch, self.n_lat = batch, n_lat
        self.n_ctx_rows = batch * N_CTX
        self.n_rows = self.n_ctx_rows + batch * n_lat
        self.ctx_blocks = self.n_ctx_rows // TM
        self.lat_blocks = batch * n_lat // TM
        self.blocks_per_batch = n_lat // TM

    def mod_index(self, blk):
        return jnp.where(blk < self.ctx_blocks, 0, 1 + (blk - self.ctx_blocks) // self.blocks_per_batch)

    def rope_index(self, blk):
        return jnp.where(blk < self.ctx_blocks, 0, 1 + (blk - self.ctx_blocks) % self.blocks_per_batch)

    def query_block(self, b, i, tq):
        cs = N_CTX // tq
        return jnp.where(i < cs, b * cs + i, self.n_ctx_rows // tq + b * (self.n_lat // tq) + i - cs)

    def latent_block(self, b):
        return self.n_ctx_rows // self.n_lat + b


def _mod_kernel(c_ref, w_ref, b_ref, o_ref):
    c = c_ref[...]
    s = (c * jax.nn.sigmoid(c)).astype(BF16)
    o_ref[0] = _dot(s, w_ref[0].astype(BF16)) + b_ref[0]


def _mod_vectors(cond, w_mod, b_mod):
    depth, d, n = w_mod.shape
    rows = cond.shape[0]
    tn = 1536
    return pl.pallas_call(
        _mod_kernel,
        grid=(depth, n // tn),
        in_specs=[pl.BlockSpec((rows, d), lambda l, j: (0, 0)),
                  pl.BlockSpec((1, d, tn), lambda l, j: (l, 0, j)),
                  pl.BlockSpec((1, 1, tn), lambda l, j: (l, 0, j))],
        out_specs=pl.BlockSpec((1, rows, tn), lambda l, j: (l, 0, j)),
        out_shape=jax.ShapeDtypeStruct((depth, rows, n), F32),
        compiler_params=_cparams(2),
        name="mod_vectors",
    )(cond, w_mod, b_mod.reshape(depth, 1, n))


def _even_in_kernel(x_ref, mod_ref, g1_ref, w_ref, down_ref, up_ref, gqk_ref, ones_ref, perm_ref,
                    cos_ref, sa_ref, sb_ref, q_ref, k_ref, v_ref, u_ref):
    m = mod_ref[0]
    nqk = GQA_Q_W + GQA_KV_W
    nv = GQA_KV_HEADS * V_EXT
    nc = SUB_ROWS // SSM_CHUNK

    def project(rows):
        a = _modulate(x_ref[rows, :], g1_ref[...], m[:, 0:D_MODEL], m[:, D_MODEL:2 * D_MODEL]).astype(BF16)
        return _dot(a, w_ref[...])

    def finish(r, rows, h):
        qk = _head_rms(h[:, :nqk], down_ref[...], up_ref[...], HEAD_DIM) * gqk_ref[...]
        qk = _rope(qk, cos_ref[rows, :], sa_ref[rows, :], sb_ref[rows, :], HEAD_DIM // 4)
        q_ref[:, rows] = qk[:, :GQA_Q_W].T.astype(BF16)
        k_ref[rows, :] = qk[:, GQA_Q_W:].astype(BF16)
        v_ref[:, rows] = (h[:, nqk:nqk + nv] + ones_ref[...]).T.astype(BF16)
        us = _dot(perm_ref[...], h[:, nqk + nv:].astype(BF16))
        u_ref[r * nc:(r + 1) * nc, :] = jnp.concatenate(
            [us[s * nc:(s + 1) * nc] for s in range(SSM_CHUNK)], axis=1).astype(BF16)

    _issue_ahead(project, finish)


def _mod_spec(rows):
    return pl.BlockSpec((1, 1, N_MOD * D_MODEL), lambda i: (rows.mod_index(i), 0, 0))


def _even_in_proj(x, modtab, consts, tabs, rows):
    row_spec = lambda w: pl.BlockSpec((TM, w), lambda i: (i, 0))
    tab_spec = pl.BlockSpec((TM, 128), lambda i: (rows.rope_index(i), 0))
    col_spec = lambda w: pl.BlockSpec((w, TM), lambda i: (0, i))
    v_w = GQA_KV_HEADS * V_EXT
    chunk_rows, chunk_w = TM // SSM_CHUNK, SSM_CHUNK * SSM_WIDTH
    sds = jax.ShapeDtypeStruct
    return pl.pallas_call(
        _even_in_kernel,
        grid=(rows.n_rows // TM,),
        in_specs=[row_spec(D_MODEL), _mod_spec(rows)] + [_resident(c.shape) for c in consts] + [tab_spec] * 3,
        out_specs=[col_spec(GQA_Q_W), row_spec(GQA_KV_W), col_spec(v_w),
                   pl.BlockSpec((chunk_rows, chunk_w), lambda i: (i, 0))],
        out_shape=[sds((GQA_Q_W, rows.n_rows), BF16), sds((rows.n_rows, GQA_KV_W), BF16),
                   sds((v_w, rows.n_rows), BF16), sds((rows.n_rows // SSM_CHUNK, chunk_w), BF16)],
        compiler_params=_cparams(1),
        name="even_in_proj",
    )(x, modtab, *consts, *tabs)


def _odd_in_kernel(x_ref, mod_ref, g1_ref, w_ref, gcq_ref, gckv_ref, wuq_ref, wuk_ref, wuv_ref,
                   gmq_ref, gmk_ref, down_ref, up_ref, gnqk_ref, ones_ref, cos_ref, sa_ref, sb_ref,
                   mq_ref, mk_ref, mv_ref, nq_ref, nk_ref, nv_ref):
    m = mod_ref[0]
    c1 = MLA_Q_RANK
    c2 = c1 + MLA_KV_RANK
    c3 = c2 + MLA_PAD

    def rms(t, g):
        return (t * lax.rsqrt(jnp.mean(t * t, axis=-1, keepdims=True) + EPS) * g).astype(BF16)

    def project(rows):
        a = _modulate(x_ref[rows, :], g1_ref[...], m[:, 0:D_MODEL], m[:, D_MODEL:2 * D_MODEL]).astype(BF16)
        return _dot(a, w_ref[...])

    def finish(r, rows, h):
        q = _dot(rms(h[:, :c1], gcq_ref[...]), wuq_ref[...])
        ckv = rms(h[:, c1:c2], gckv_ref[...])
        k = _dot(ckv, wuk_ref[...]) + jnp.tile(h[:, c2:c3], (1, MLA_HEADS))
        mv_ref[:, rows] = (_dot(ckv, wuv_ref[...]) + ones_ref[...]).T.astype(BF16)

        def mla_heads(t, g):
            parts = []
            for hh in range(MLA_HEADS):
                th = t[:, hh * MLA_PAD:(hh + 1) * MLA_PAD]
                ss = jnp.sum(th * th, axis=-1, keepdims=True)
                parts.append(th * lax.rsqrt(ss * (1.0 / MLA_QK) + EPS))
            t = jnp.concatenate(parts, axis=1) * g
            return _rope(t, cos_ref[rows, :], sa_ref[rows, :], sb_ref[rows, :], MLA_ROPE // 4)

        mq_ref[:, rows] = mla_heads(q, gmq_ref[...]).T.astype(BF16)
        mk_ref[rows, :] = mla_heads(k, gmk_ref[...]).astype(BF16)

        nqk = _head_rms(h[:, c3:c3 + 2 * NA_W], down_ref[...], up_ref[...], HEAD_DIM) * gnqk_ref[...]
        nq_ref[:, rows] = nqk[:, :NA_W].T.astype(BF16)
        nk_ref[rows, :] = nqk[:, NA_W:].astype(BF16)
        nvt = h[:, c3 + 2 * NA_W:].T
        ones = jnp.ones((V_EXT - HEAD_DIM, nvt.shape[1]), F32)
        pieces = []
        for hh in range(NA_HEADS):
            pieces += [nvt[hh * HEAD_DIM:(hh + 1) * HEAD_DIM], ones]
        nv_ref[:, rows] = jnp.concatenate(pieces, axis=0).astype(BF16)

    _issue_ahead(project, finish)


def _odd_in_proj(x, modtab, consts, tabs, rows):
    row_spec = lambda w: pl.BlockSpec((TM, w), lambda i: (i, 0))
    tab_spec = pl.BlockSpec((TM, 128), lambda i: (rows.rope_index(i), 0))
    col_spec = lambda w: pl.BlockSpec((w, TM), lambda i: (0, i))
    mla_w = MLA_HEADS * MLA_PAD
    v_w = MLA_HEADS * V_EXT
    rows_out = lambda w: jax.ShapeDtypeStruct((rows.n_rows, w), BF16)
    cols_out = lambda w: jax.ShapeDtypeStruct((w, rows.n_rows), BF16)
    return pl.pallas_call(
        _odd_in_kernel,
        grid=(rows.n_rows // TM,),
        in_specs=[row_spec(D_MODEL), _mod_spec(rows)] + [_resident(c.shape) for c in consts] + [tab_spec] * 3,
        out_specs=[col_spec(mla_w), row_spec(mla_w), col_spec(v_w),
                   col_spec(NA_W), row_spec(NA_W), col_spec(NA_HEADS * V_EXT)],
        out_shape=[cols_out(mla_w), rows_out(mla_w), cols_out(v_w),
                   cols_out(NA_W), rows_out(NA_W), cols_out(NA_HEADS * V_EXT)],
        compiler_params=_cparams(1),
        name="odd_in_proj",
    )(x, modtab, *consts, *tabs)


def _key_max8(s, ways=4):
    n = s.shape[0] // ways
    parts = [jnp.max(s[i * n:(i + 1) * n].reshape(n // 8, 8, s.shape[1]), axis=0) for i in range(ways)]
    while len(parts) > 1:
        parts = [jnp.maximum(a, b) for a, b in zip(parts[::2], parts[1::2])]
    return parts[0]


def _softmax_heads(n_heads, scores, values, o_ref):
    def probabilities(s_parts):
        m = jnp.max(functools.reduce(jnp.maximum, [_key_max8(s) for s in s_parts]), axis=0, keepdims=True)
        return [jnp.exp2(s - m).astype(BF16) for s in s_parts]

    def weighted_values(h, p_parts):
        acc = None
        for p, vt in zip(p_parts, values(h)):
            part = _dot(vt, p)
            acc = part if acc is None else acc + part
        return acc[:HEAD_DIM] / acc[HEAD_DIM:]

    outs = []
    s_cur, p_prev = scores(0), None
    for h in range(n_heads):
        s_next = scores(h + 1) if h + 1 < n_heads else None
        p_cur = probabilities(s_cur)
        if p_prev is not None:
            outs.append(weighted_values(h - 1, p_prev))
        s_cur, p_prev = s_next, p_cur
    outs.append(weighted_values(n_heads - 1, p_prev))
    o_ref[...] = jnp.concatenate(outs, axis=0).T.astype(o_ref.dtype)


def _attn_kernel(qt_ref, kc_ref, kl_ref, vtc_ref, vtl_ref, o_ref, *, n_heads, group, dk, first_step):
    tq = qt_ref.shape[1]
    n_lat = kl_ref.shape[0]
    k_width = kc_ref.shape[1]
    pad_q = k_width <= 256

    def scores(h, with_latent):
        kv = h // group
        qt = qt_ref[h * dk:(h + 1) * dk, :]
        if pad_q:
            pieces = []
            if kv:
                pieces.append(jnp.zeros((kv * dk, tq), qt.dtype))
            pieces.append(qt)
            if k_width - (kv + 1) * dk:
                pieces.append(jnp.zeros((k_width - (kv + 1) * dk, tq), qt.dtype))
            qt = jnp.concatenate(pieces, axis=0)
            ksl = slice(None)
        else:
            ksl = slice(kv * dk, (kv + 1) * dk)
        parts = [_dot(kc_ref[:, ksl], qt)]
        if with_latent:
            parts += [_dot(kl_ref[c * KEY_CHUNK:(c + 1) * KEY_CHUNK, ksl], qt) for c in range(n_lat // KEY_CHUNK)]
        return parts

    def values(h, with_latent):
        vsl = slice((h // group) * V_EXT, (h // group + 1) * V_EXT)
        parts = [vtc_ref[vsl, :]]
        if with_latent:
            parts += [vtl_ref[vsl, c * KEY_CHUNK:(c + 1) * KEY_CHUNK] for c in range(n_lat // KEY_CHUNK)]
        return parts

    def attend(with_latent):
        _softmax_heads(n_heads, lambda h: scores(h, with_latent), lambda h: values(h, with_latent), o_ref)

    if first_step == 0:
        is_ctx = pl.program_id(1) == 0

        @pl.when(is_ctx)
        def _():
            attend(False)

        @pl.when(jnp.logical_not(is_ctx))
        def _():
            attend(True)
    else:
        attend(True)


def _attention(qt, k, vt, rows, *, n_heads, group, dk, skip_ctx_queries):
    first = 1 if skip_ctx_queries else 0
    kern = functools.partial(_attn_kernel, n_heads=n_heads, group=group, dk=dk, first_step=first)
    wo = n_heads * HEAD_DIM
    return pl.pallas_call(
        kern,
        grid=(rows.batch, rows.n_lat // TQ + 1 - first),
        in_specs=[pl.BlockSpec((qt.shape[0], TQ), lambda b, i: (0, rows.query_block(b, i + first, TQ))),
                  pl.BlockSpec((N_CTX, k.shape[1]), lambda b, i: (b, 0)),
                  pl.BlockSpec((rows.n_lat, k.shape[1]), lambda b, i: (rows.latent_block(b), 0)),
                  pl.BlockSpec((vt.shape[0], N_CTX), lambda b, i: (0, b)),
                  pl.BlockSpec((vt.shape[0], rows.n_lat), lambda b, i: (0, rows.latent_block(b)))],
        out_specs=pl.BlockSpec((TQ, wo), lambda b, i: (rows.query_block(b, i + first, TQ), 0)),
        out_shape=jax.ShapeDtypeStruct((rows.n_rows, wo), BF16),
        compiler_params=_cparams(2),
        name="attention_dk%d" % dk,
    )(qt, k, k, vt, vt)


def _na_geometry(grid_rows):
    rb = NA_ROWS_PER_BLOCK
    wr = min(NA_WIN_R, grid_rows)
    assert grid_rows % 2 == 0 and GRID_W * 2 == 128
    span = min(grid_rows, wr + rb - 1 + (wr + rb - 1) % 2)
    starts, variants, keys = [], [], {}
    for r0 in range(0, grid_rows, rb):
        rs0 = int(np.clip(r0 - wr // 2, 0, grid_rows - wr))
        start = min(rs0, grid_rows - span)
        start -= start % 2
        rel = tuple(int(np.clip(r0 + dr - wr // 2, 0, grid_rows - wr)) - (r0 + dr) for dr in range(rb))
        key = (rel, start - r0)
        variants.append(keys.setdefault(key, len(keys)))
        starts.append(start)
    firsts = [variants.index(v) for v in range(len(keys))]
    return wr, span, starts, variants, firsts


def _na_bias_table(rpb, grid_rows):
    rb = NA_ROWS_PER_BLOCK
    wr, span, starts, variants, firsts = _na_geometry(grid_rows)
    nv = len(firsts)
    idx_r = np.zeros((nv, rb, span), np.int32)
    ok_r = np.zeros((nv, rb, span), bool)
    for v, blk in enumerate(firsts):
        r0, start = blk * rb, starts[blk]
        for dr in range(rb):
            r = r0 + dr
            rs = int(np.clip(r - wr // 2, 0, grid_rows - wr))
            for j in range(span):
                kr = start + j
                ok_r[v, dr, j] = rs <= kr < rs + wr
                idx_r[v, dr, j] = np.clip(kr - r + NA_WIN_R - 1, 0, 2 * NA_WIN_R - 2)
    c = np.arange(GRID_W)
    cs = np.clip(c - NA_WIN_C // 2, 0, GRID_W - NA_WIN_C)
    kc = np.arange(GRID_W)
    ok_c = (kc[None, :] >= cs[:, None]) & (kc[None, :] < cs[:, None] + NA_WIN_C)
    idx_c = np.clip(kc[None, :] - c[:, None] + NA_WIN_C - 1, 0, 2 * NA_WIN_C - 2)
    n_rel_c = 2 * NA_WIN_C - 1
    picked = rpb[:, idx_r.reshape(-1)].reshape(rpb.shape[0], nv, rb, span, n_rel_c)
    picked = jnp.transpose(picked, (1, 0, 2, 3, 4)) * LOG2E
    onehot = jnp.asarray(idx_c.T[None, :, :] == np.arange(n_rel_c)[:, None, None], F32)
    bias = jnp.concatenate([jnp.einsum("vhjx,xkc->vhjkc", picked[:, :, d], onehot, precision=lax.Precision.HIGHEST)
                            for d in range(rb)], axis=-1)
    ok = ok_r[:, None, :, None, :, None] & ok_c[None, None, None, :, None, :]
    ok = np.transpose(ok, (0, 1, 4, 5, 2, 3)).reshape(nv, 1, span, GRID_W, rb * GRID_W)
    bias = jnp.where(ok, bias, MASK_VALUE)
    return bias.reshape(nv, rpb.shape[0], span * GRID_W, rb * GRID_W)


def _na_kernel(start_ref, var_ref, qt_ref, kc_ref, kl_ref, vtc_ref, vtl_ref, bias_ref, o_ref, *, n_loc, first_step):
    i = pl.program_id(1) + first_step
    tq = qt_ref.shape[1]

    def padded_query(h):
        qt = qt_ref[h * HEAD_DIM:(h + 1) * HEAD_DIM, :]
        zeros = jnp.zeros((HEAD_DIM, tq), qt.dtype)
        return jnp.concatenate([qt, zeros] if h % 2 == 0 else [zeros, qt], axis=0)

    def pair_lanes(h):
        return slice((h // 2) * 2 * HEAD_DIM, (h // 2 + 1) * 2 * HEAD_DIM)

    def value_rows(h):
        return slice(h * V_EXT, (h + 1) * V_EXT)

    def context_queries():
        _softmax_heads(NA_HEADS, lambda h: [_dot(kc_ref[:, pair_lanes(h)], padded_query(h))],
                       lambda h: [vtc_ref[value_rows(h), :]], o_ref)

    def latent_queries():
        off = pl.multiple_of(start_ref[i - 1] * GRID_W, 2 * GRID_W)
        var = var_ref[i - 1]

        def scores(h):
            qt = padded_query(h)
            return [_dot(kc_ref[:, pair_lanes(h)], qt),
                    _dot(kl_ref[pl.ds(off, n_loc), pair_lanes(h)], qt) + bias_ref[var, h]]

        _softmax_heads(NA_HEADS, scores,
                       lambda h: [vtc_ref[value_rows(h), :], vtl_ref[value_rows(h), pl.ds(off, n_loc)]], o_ref)

    if first_step == 0:
        pl.when(i == 0)(context_queries)
        pl.when(i > 0)(latent_queries)
    else:
        latent_queries()


def _neighbourhood_attention(qt, k, vt, bias, rows, skip_ctx_queries):
    grid_rows = rows.n_lat // GRID_W
    _, span, starts, variants, _ = _na_geometry(grid_rows)
    assert NA_ROWS_PER_BLOCK * GRID_W == TQ
    first = 1 if skip_ctx_queries else 0
    kern = functools.partial(_na_kernel, n_loc=span * GRID_W, first_step=first)
    qblock = lambda b, i: rows.query_block(b, i + first, TQ)
    v_w = vt.shape[0]
    grid_spec = pltpu.PrefetchScalarGridSpec(
        num_scalar_prefetch=2,
        grid=(rows.batch, rows.n_lat // TQ + 1 - first),
        in_specs=[pl.BlockSpec((NA_W, TQ), lambda b, i, *_: (0, qblock(b, i))),
                  pl.BlockSpec((N_CTX, NA_W), lambda b, i, *_: (b, 0)),
                  pl.BlockSpec((rows.n_lat, NA_W), lambda b, i, *_: (rows.latent_block(b), 0)),
                  pl.BlockSpec((v_w, N_CTX), lambda b, i, *_: (0, b)),
                  pl.BlockSpec((v_w, rows.n_lat), lambda b, i, *_: (0, rows.latent_block(b))),
                  pl.BlockSpec(bias.shape, lambda b, i, *_: (0, 0, 0, 0), pipeline_mode=pl.Buffered(1))],
        out_specs=pl.BlockSpec((TQ, NA_W), lambda b, i, *_: (qblock(b, i), 0)),
    )
    return pl.pallas_call(
        kern,
        grid_spec=grid_spec,
        out_shape=jax.ShapeDtypeStruct((rows.n_rows, NA_W), BF16),
        compiler_params=_cparams(2),
        name="neighbourhood_attention",
    )(jnp.asarray(starts, jnp.int32), jnp.asarray(variants, jnp.int32), qt, k, k, vt, vt, bias)


def _s5_matrices(lam_re, lam_im, log_dt, b_re, b_im, c_re, c_im, d_skip):
    L, G, P, N = SSM_CHUNK, SSM_GROUPS, SSM_GROUP, SSM_STATE
    hi = lax.Precision.HIGHEST
    dt = jnp.exp(log_dt)[..., None]
    pw = jnp.arange(L + 1, dtype=F32)[:, None, None, None]
    mag = jnp.exp(lam_re * dt * pw)
    e_re = mag * jnp.cos(lam_im * dt * pw)
    e_im = mag * jnp.sin(lam_im * dt * pw)
    a_re, a_im = e_re[1], e_im[1]
    den = jnp.square(lam_re) + jnp.square(lam_im)
    f_re = ((a_re - 1.0) * lam_re + a_im * lam_im) / den
    f_im = (a_im * lam_re - (a_re - 1.0) * lam_im) / den
    bb_re = f_re[..., None] * b_re - f_im[..., None] * b_im
    bb_im = f_re[..., None] * b_im + f_im[..., None] * b_re
    ce_re = c_re[None] * e_re[:, :, :, None, :] - c_im[None] * e_im[:, :, :, None, :]
    ce_im = c_re[None] * e_im[:, :, :, None, :] + c_im[None] * e_re[:, :, :, None, :]
    kk = (jnp.einsum("kdgqn,dgnp->kdgqp", ce_re, bb_re, precision=hi)
          - jnp.einsum("kdgqn,dgnp->kdgqp", ce_im, bb_im, precision=hi))
    lag = np.arange(L)[None, :] - np.arange(L)[:, None]
    kf = kk[np.clip(lag, 0, L - 1), 0]
    kb = kk[np.clip(-lag, 0, L - 1), 1]
    skip = jnp.eye(P, dtype=F32)[None] * d_skip.reshape(G, P)[:, None, :]
    m = lambda cond: jnp.asarray(cond, F32)[:, :, None, None, None]
    kst = m(lag >= 0) * kf + m(lag <= 0) * kb + m(lag == 0) * skip[None, None]
    a_t = jnp.transpose(kst, (0, 2, 4, 1, 3)).reshape(L * G * P, L * P)

    def state_in(pows, d):
        x_re = e_re[pows, d][..., None] * bb_re[d][None] - e_im[pows, d][..., None] * bb_im[d][None]
        x_im = e_re[pows, d][..., None] * bb_im[d][None] + e_im[pows, d][..., None] * bb_re[d][None]
        flat = lambda x: jnp.pad(jnp.transpose(x, (0, 1, 3, 2)).reshape(L * G * P, N), ((0, 0), (0, L * P - N)))
        return [flat(x_re), flat(x_im)]

    a_w1 = jnp.concatenate([a_t] + state_in(np.arange(L)[::-1].copy(), 0) + state_in(np.arange(L), 1), axis=1)
    w1 = _expand_group_blocks(a_w1[None], L * G * P + 4 * G * N, row_shift=4)

    def state_out(x):
        return jnp.transpose(x, (1, 3, 0, 2)).reshape(G * N, L * P)

    pf, pb = np.arange(1, L + 1), np.arange(L, 0, -1)
    a_wc = jnp.stack([jnp.concatenate([state_out(ce_re[pf, 0]), state_out(ce_re[pb, 1])], axis=0),
                      jnp.concatenate([state_out(-ce_im[pf, 0]), state_out(-ce_im[pb, 1])], axis=0)])
    wc = _expand_group_blocks(a_wc, L * G * P, row_shift=6)
    al_re = e_re[L].reshape(2, 1, G * N)
    al_im = e_im[L].reshape(2, 1, G * N)
    return w1[0], wc[0], wc[1], al_re, al_im


def _expand_kernel(a_ref, o_ref, *, row_shift, n_response_tiles):
    j = pl.program_id(1)
    a = a_ref[0].astype(BF16)
    n_rows, tn = o_ref.shape[1], o_ref.shape[2]
    src = lax.broadcasted_iota(jnp.int32, (a.shape[1], tn), 0)
    col = lax.broadcasted_iota(jnp.int32, (a.shape[1], tn), 1)
    row_group = (lax.broadcasted_iota(jnp.int32, (n_rows, tn), 0) >> row_shift) & (SSM_GROUPS - 1)
    out_col = lax.broadcasted_iota(jnp.int32, (n_rows, tn), 1)

    def emit(spread, col_group):
        val = _dot(a, jnp.where(spread, 1.0, 0.0).astype(BF16))
        o_ref[0] = jnp.where(row_group == col_group, val, 0.0).astype(o_ref.dtype)

    @pl.when(j < n_response_tiles)
    def _():
        cg = col + j * tn
        spread = ((src >> 4) == (cg >> 8)) & ((src & 15) == (cg & 15))
        emit(spread, ((out_col + j * tn) >> 4) & (SSM_GROUPS - 1))

    @pl.when(j >= n_response_tiles)
    def _():
        spread = src == (col & (SSM_STATE - 1))
        emit(spread, (out_col >> 6) & (SSM_GROUPS - 1))


def _expand_group_blocks(a, n_cols, *, row_shift):
    k, n_rows, _ = a.shape
    tn = 1024
    n_response_tiles = SSM_CHUNK * SSM_WIDTH // tn
    assert SSM_GROUP == 16 and SSM_STATE == 64 and SSM_CHUNK * SSM_GROUP == 128 and tn % (SSM_GROUPS * SSM_STATE) == 0
    kern = functools.partial(_expand_kernel, row_shift=row_shift, n_response_tiles=n_response_tiles)
    return pl.pallas_call(
        kern,
        grid=(k, n_cols // tn),
        in_specs=[pl.BlockSpec((1, n_rows, 128), lambda d, j: (d, 0, jnp.maximum(j - n_response_tiles + 1, 0)))],
        out_specs=pl.BlockSpec((1, n_rows, tn), lambda d, j: (d, 0, j)),
        out_shape=jax.ShapeDtypeStruct((k, n_rows, n_cols), BF16),
        compiler_params=_cparams(2),
        name="s5_expand",
    )(a)


def _mm_kernel(a_ref, b_ref, o_ref):
    o_ref[...] = _dot(a_ref[...], b_ref[...]).astype(o_ref.dtype)


def _matmul(a, b, tm, tn, out_dtype):
    m, kdim = a.shape
    n = b.shape[1]
    return pl.pallas_call(
        _mm_kernel,
        grid=(n // tn, m // tm),
        in_specs=[pl.BlockSpec((tm, kdim), lambda j, i: (i, 0)),
                  pl.BlockSpec((kdim, tn), lambda j, i: (0, j))],
        out_specs=pl.BlockSpec((tm, tn), lambda j, i: (i, j)),
        out_shape=jax.ShapeDtypeStruct((m, n), out_dtype),
        compiler_params=_cparams(2),
        name="s5_chunk_matmul",
    )(a, b)


def _s5_carry_kernel(sre_ref, sim_ref, are_ref, aim_ref, hre_ref, him_ref, *, ctx_chunks):
    n_chunks = sre_ref.shape[0]
    backward = pl.program_id(0) == 1
    a_re = jnp.broadcast_to(are_ref[0], sre_ref.shape[1:])
    a_im = jnp.broadcast_to(aim_ref[0], sre_ref.shape[1:])

    def step(j, carry):
        h_re, h_im = carry
        cb = jnp.where(j < ctx_chunks, ctx_chunks - 1 - j, n_chunks + ctx_chunks - 1 - j)
        c = jnp.where(backward, cb, j)
        hre_ref[c] = h_re.astype(hre_ref.dtype)
        him_ref[c] = h_im.astype(him_ref.dtype)
        n_re = a_re * h_re - a_im * h_im + sre_ref[c]
        n_im = a_re * h_im + a_im * h_re + sim_ref[c]
        return n_re, n_im

    zero = jnp.zeros(sre_ref.shape[1:], F32)
    lax.fori_loop(0, n_chunks, step, (zero, zero))


def _s5_carry(ys, al_re, al_im, n_chunks, batch, ctx_chunks):
    gn = SSM_GROUPS * SSM_STATE
    tw = 256
    base = SSM_CHUNK * SSM_WIDTH // tw
    per_dir = 2 * gn // tw
    s3 = ys.reshape(n_chunks, batch, ys.shape[1])
    kern = functools.partial(_s5_carry_kernel, ctx_chunks=ctx_chunks)
    blk = (n_chunks, batch, tw)
    return pl.pallas_call(
        kern,
        grid=(2, gn // tw),
        in_specs=[pl.BlockSpec(blk, lambda d, j: (0, 0, base + d * per_dir + j)),
                  pl.BlockSpec(blk, lambda d, j: (0, 0, base + d * per_dir + gn // tw + j)),
                  pl.BlockSpec((1, 1, tw), lambda d, j: (d, 0, j)),
                  pl.BlockSpec((1, 1, tw), lambda d, j: (d, 0, j))],
        out_specs=[pl.BlockSpec(blk, lambda d, j: (0, 0, d * (gn // tw) + j))] * 2,
        out_shape=[jax.ShapeDtypeStruct((n_chunks, batch, 2 * gn), BF16)] * 2,
        compiler_params=_cparams(2),
        name="s5_carry",
    )(s3, s3, al_re, al_im)


def _s5_readout_kernel(hre_ref, him_ref, wre_ref, wim_ref, y_ref, o_ref):
    o_ref[...] = y_ref[...] + _dot(hre_ref[...], wre_ref[...]) + _dot(him_ref[...], wim_ref[...])


def _s5_readout(h_re, h_im, wc_re, wc_im, ys, tm):
    m, kdim = h_re.shape
    n = wc_re.shape[1]
    tn = 1024
    return pl.pallas_call(
        _s5_readout_kernel,
        grid=(n // tn, m // tm),
        in_specs=[pl.BlockSpec((tm, kdim), lambda j, i: (i, 0)),
                  pl.BlockSpec((tm, kdim), lambda j, i: (i, 0)),
                  pl.BlockSpec((kdim, tn), lambda j, i: (0, j)),
                  pl.BlockSpec((kdim, tn), lambda j, i: (0, j)),
                  pl.BlockSpec((tm, tn), lambda j, i: (i, j))],
        out_specs=pl.BlockSpec((tm, tn), lambda j, i: (i, j)),
        out_shape=jax.ShapeDtypeStruct((m, n), F32),
        compiler_params=_cparams(2),
        name="s5_readout",
    )(h_re, h_im, wc_re, wc_im, ys)


def _row_tile(m, target=512):
    t = min(m, target)
    while m % t or t % 16:
        t -= 16
    return t


def _s5_mixer(u, mats, rows):
    w1, wc_re, wc_im, al_re, al_im = mats
    batch = rows.batch
    cw = SSM_CHUNK * SSM_WIDTH
    ctx_chunks = N_CTX // SSM_CHUNK
    n_chunks = ctx_chunks + rows.n_lat // SSM_CHUNK
    n_ctx_chunk_rows = rows.n_ctx_rows // SSM_CHUNK
    uc = jnp.concatenate([u[:n_ctx_chunk_rows].reshape(batch, ctx_chunks, cw),
                          u[n_ctx_chunk_rows:].reshape(batch, n_chunks - ctx_chunks, cw)], axis=1)
    uc = jnp.transpose(uc, (1, 0, 2)).reshape(n_chunks * batch, cw)
    tm = _row_tile(n_chunks * batch)
    ys = _matmul(uc, w1, tm, 1024, F32)
    h_re, h_im = _s5_carry(ys, al_re, al_im, n_chunks, batch, ctx_chunks)
    gn2 = 2 * SSM_GROUPS * SSM_STATE
    y = _s5_readout(h_re.reshape(-1, gn2), h_im.reshape(-1, gn2), wc_re, wc_im, ys, tm)
    y = jnp.transpose(y.reshape(n_chunks, batch, cw), (1, 0, 2))
    return jnp.concatenate([y[:, :ctx_chunks].reshape(n_ctx_chunk_rows, cw),
                            y[:, ctx_chunks:].reshape(-1, cw)], axis=0)


def _gelu_tanh(y):
    return 0.5 * y * (1.0 + jnp.tanh(math.sqrt(2.0 / math.pi) * (y + 0.044715 * (y * y * y))))


def _out_mlp_kernel(*refs, even):
    if even:
        x_ref, m1_ref, m2_ref, mod_ref, perm_ref, wglu_ref, bglu_ref, wo_ref, g2_ref, w1_ref, w2_ref, o_ref = refs
        yc = m2_ref[...]
        ys = jnp.concatenate([yc[:, s * SSM_WIDTH:(s + 1) * SSM_WIDTH] for s in range(SSM_CHUNK)], axis=0)
        hi = ys.astype(BF16)
        lo = (ys - hi.astype(F32)).astype(BF16)
        y = _gelu_tanh(_dot(perm_ref[...], hi) + _dot(perm_ref[...], lo))
        z = _dot(y.astype(BF16), wglu_ref[...]) + bglu_ref[...]
        second = (y * jax.nn.sigmoid(z)).astype(BF16)
    else:
        x_ref, m1_ref, m2_ref, mod_ref, wo_ref, g2_ref, w1_ref, w2_ref, o_ref = refs
        second = m2_ref[...]
    m = mod_ref[0]
    mod = lambda j: m[:, j * D_MODEL:(j + 1) * D_MODEL]
    mix = jnp.concatenate([m1_ref[...], second], axis=1)
    x1 = x_ref[...] + mod(2) * _dot(mix, wo_ref[...])
    a = _modulate(x1, g2_ref[...], mod(3), mod(4)).astype(BF16)
    acc = None
    ck = 1024
    for c in range(D_FF // ck):
        h = jnp.maximum(_dot(a, w1_ref[:, c * ck:(c + 1) * ck]), 0.0)
        part = _dot((h * h).astype(BF16), w2_ref[c * ck:(c + 1) * ck, :])
        acc = part if acc is None else acc + part
    o_ref[...] = x1 + mod(5) * acc


def _out_mlp(x, mix1, mix2, modtab, consts, rows, even, latent_only):
    first = rows.ctx_blocks if latent_only else 0
    row_spec = lambda w: pl.BlockSpec((TM, w), lambda i: (i + first, 0))
    n_blocks = rows.n_rows // TM - first
    mix2_rows = TM // SSM_CHUNK if even else TM
    return pl.pallas_call(
        functools.partial(_out_mlp_kernel, even=even),
        grid=(n_blocks,),
        in_specs=[row_spec(D_MODEL), row_spec(mix1.shape[1]),
                  pl.BlockSpec((mix2_rows, mix2.shape[1]), lambda i: (i + first, 0)),
                  pl.BlockSpec((1, 1, N_MOD * D_MODEL), lambda i: (rows.mod_index(i + first), 0, 0))]
                 + [_resident(c.shape) for c in consts],
        out_specs=pl.BlockSpec((TM, D_MODEL), lambda i: (i, 0)),
        out_shape=jax.ShapeDtypeStruct((n_blocks * TM, D_MODEL), F32),
        compiler_params=_cparams(1),
        name="out_mlp_even" if even else "out_mlp_odd",
    )(x, mix1, mix2, modtab, *consts)


def _rope_tables(n_lat, rot_dim, head_w, lane_off):
    t = jnp.arange(n_lat)
    grid_r = (t // GRID_W).astype(F32)
    grid_c = (t % GRID_W).astype(F32)
    axis_dim = rot_dim // 2
    freqs = ROPE_BASE ** (-jnp.arange(0, axis_dim, 2, dtype=F32) / axis_dim)
    ang_r = grid_r[:, None] * freqs
    ang_c = grid_c[:, None] * freqs
    ang = jnp.concatenate([ang_r, ang_r, ang_c, ang_c], axis=-1)
    cos, sin = jnp.cos(ang), jnp.sin(ang)
    quarter = rot_dim // 4
    first = (np.arange(rot_dim) // quarter) % 2 == 0
    sin_a = jnp.where(first, -sin, 0.0)
    sin_b = jnp.where(first, 0.0, sin)

    def widen(tab, fill):
        full = jnp.full((n_lat, head_w), fill, F32).at[:, lane_off:lane_off + rot_dim].set(tab)
        full = jnp.concatenate([jnp.full((TM, head_w), fill, F32), full], axis=0)
        return jnp.tile(full, (1, 128 // head_w))

    return widen(cos, 1.0), widen(sin_a, 0.0), widen(sin_b, 0.0)


def _head_sum_matrices(n_heads, head_dim):
    down = np.kron(np.eye(n_heads), np.ones((head_dim, 1)))
    pad = (-n_heads) % 128
    down = np.pad(down, ((0, 0), (0, pad)))
    return jnp.asarray(down, BF16), jnp.asarray(down.T, BF16)


def _pad_heads(w, n_heads, width, pad_to):
    lead = w.shape[:-1]
    w = w.reshape(lead + (n_heads, width))
    w = jnp.pad(w, [(0, 0)] * len(lead) + [(0, 0), (0, pad_to - width)])
    return w.reshape(lead + (n_heads * pad_to,))


def kernel(x, c, ctx, c_ctx, w_mod, b_mod, g_norm1, g_norm2, w_ff1, w_ff2, e_w_in, e_w_out, e_g_q, e_g_k, ssm_lam_re, ssm_lam_im, ssm_log_dt, ssm_b_re, ssm_b_im, ssm_c_re, ssm_c_im, ssm_d, ssm_w_glu, ssm_b_glu, o_w_in, o_w_out, mla_g_cq, mla_g_ckv, mla_w_uq, mla_w_ukv, mla_g_q, mla_g_k, na_g_q, na_g_k, na_rpb):
    batch, n_lat, d = x.shape
    depth = w_mod.shape[0]
    assert d == D_MODEL and ctx.shape[1] == N_CTX and n_lat % GRID_W == 0
    rows = _Rows(batch, n_lat)

    pad_rows = (-(batch + 1)) % 8
    cond = jnp.concatenate([c_ctx[None], c, jnp.zeros((pad_rows, d), F32)], axis=0)
    modtabs = _mod_vectors(cond, w_mod, b_mod).reshape(depth, -1, 1, N_MOD * d)

    xs = jnp.concatenate([ctx.reshape(rows.n_ctx_rows, d), x.reshape(batch * n_lat, d)], axis=0)

    even_tabs = _rope_tables(n_lat, HEAD_DIM, HEAD_DIM, 0)
    mla_tabs = _rope_tables(n_lat, MLA_ROPE, MLA_PAD, MLA_NOPE)
    down_e, up_e = _head_sum_matrices(GQA_Q_HEADS + GQA_KV_HEADS, HEAD_DIM)
    down_o, up_o = _head_sum_matrices(2 * NA_HEADS, HEAD_DIM)
    row = lambda v: v.reshape(1, -1)
    ones_half = np.concatenate([np.zeros(HEAD_DIM), np.ones(V_EXT - HEAD_DIM)])
    ones_e = jnp.asarray(np.tile(ones_half, GQA_KV_HEADS)[None], F32)
    ones_o = jnp.asarray(np.tile(ones_half, MLA_HEADS)[None], F32)
    def chunk_perm(n):
        tok = np.arange(n)
        mat = np.zeros((n, n), np.float32)
        mat[(tok % SSM_CHUNK) * (n // SSM_CHUNK) + tok // SSM_CHUNK, tok] = 1.0
        return mat

    perm = jnp.asarray(chunk_perm(SUB_ROWS), BF16)
    perm_t = jnp.asarray(chunk_perm(TM).T, BF16)

    for i in range(depth):
        j = i // 2
        last = i == depth - 1
        g1, g2 = row(g_norm1[i]), row(g_norm2[i])
        w1, w2 = w_ff1[i].astype(BF16), w_ff2[i].astype(BF16)
        if i % 2 == 0:
            gqk = jnp.concatenate([jnp.tile(e_g_q[j], GQA_Q_HEADS) * (HEAD_DIM ** -0.5 * LOG2E),
                                   jnp.tile(e_g_k[j], GQA_KV_HEADS)])
            w_in = e_w_in[j]
            nqk = GQA_Q_W + GQA_KV_W
            w_in_p = jnp.concatenate([w_in[:, :nqk],
                                      _pad_heads(w_in[:, nqk:nqk + GQA_KV_W], GQA_KV_HEADS, HEAD_DIM, V_EXT),
                                      w_in[:, nqk + GQA_KV_W:]], axis=1).astype(BF16)
            q, k, v, u = _even_in_proj(xs, modtabs[i], (g1, w_in_p, down_e, up_e, row(gqk), ones_e, perm),
                                       even_tabs, rows)
            att = _attention(q, k, v, rows, n_heads=GQA_Q_HEADS, group=GQA_GROUP, dk=HEAD_DIM, skip_ctx_queries=last)
            mats = _s5_matrices(ssm_lam_re[j], ssm_lam_im[j], ssm_log_dt[j], ssm_b_re[j], ssm_b_im[j],
                                ssm_c_re[j], ssm_c_im[j], ssm_d[j])
            y = _s5_mixer(u, mats, rows)
            consts = (perm_t, ssm_w_glu[j].astype(BF16), row(ssm_b_glu[j]), e_w_out[j].astype(BF16), g2, w1, w2)
            xs = _out_mlp(xs, att, y, modtabs[i], consts, rows, True, last)
        else:
            w_in = o_w_in[j]
            c1 = MLA_Q_RANK
            c2 = c1 + MLA_KV_RANK
            c3 = c2 + MLA_ROPE
            kr_cols = jnp.pad(w_in[:, c2:c3], ((0, 0), (MLA_NOPE, MLA_PAD - MLA_QK)))
            w_in_p = jnp.concatenate([w_in[:, :c2], kr_cols, w_in[:, c3:]], axis=1).astype(BF16)
            wuq = _pad_heads(mla_w_uq[j], MLA_HEADS, MLA_QK, MLA_PAD).astype(BF16)
            wukv = mla_w_ukv[j].reshape(MLA_KV_RANK, MLA_HEADS, MLA_NOPE + MLA_V)
            wuk = _pad_heads(wukv[:, :, :MLA_NOPE].reshape(MLA_KV_RANK, -1), MLA_HEADS, MLA_NOPE, MLA_PAD).astype(BF16)
            wuv = _pad_heads(wukv[:, :, MLA_NOPE:].reshape(MLA_KV_RANK, -1), MLA_HEADS, MLA_V, V_EXT).astype(BF16)
            gmq = _pad_heads(jnp.tile(mla_g_q[j], MLA_HEADS) * (MLA_QK ** -0.5 * LOG2E), MLA_HEADS, MLA_QK, MLA_PAD)
            gmk = _pad_heads(jnp.tile(mla_g_k[j], MLA_HEADS), MLA_HEADS, MLA_QK, MLA_PAD)
            gnqk = jnp.concatenate([jnp.tile(na_g_q[j], NA_HEADS) * (HEAD_DIM ** -0.5 * LOG2E),
                                    jnp.tile(na_g_k[j], NA_HEADS)])
            consts = (g1, w_in_p, row(mla_g_cq[j]), row(mla_g_ckv[j]), wuq, wuk, wuv, row(gmq), row(gmk),
                      down_o, up_o, row(gnqk), ones_o)
            mq, mk, mv, nq, nk, nv = _odd_in_proj(xs, modtabs[i], consts, mla_tabs, rows)
            mla = _attention(mq, mk, mv, rows, n_heads=MLA_HEADS, group=1, dk=MLA_PAD, skip_ctx_queries=last)
            bias = _na_bias_table(na_rpb[j], n_lat // GRID_W)
            na = _neighbourhood_attention(nq, nk, nv, bias, rows, last)
            consts = (o_w_out[j].astype(BF16), g2, w1, w2)
            xs = _out_mlp(xs, mla, na, modtabs[i], consts, rows, False, last)
    return xs.reshape(batch, n_lat, d)
```

```python
import functools
import math

import numpy as np
import jax
import jax.numpy as jnp
from jax import lax
from jax.experimental import pallas as pl
from jax.experimental.pallas import tpu as pltpu

F32 = jnp.float32
BF16 = jnp.bfloat16

D_MODEL = 1024
GRID_W = 64
HEAD_DIM = 64
ROPE_BASE = 10000.0
EPS = 1e-6
N_MOD = 6
D_FF = 4 * D_MODEL
LOG2E = math.log2(math.e)

GQA_Q_HEADS = 12
GQA_KV_HEADS = 4
GQA_GROUP = GQA_Q_HEADS // GQA_KV_HEADS
GQA_Q_W = GQA_Q_HEADS * HEAD_DIM
GQA_KV_W = GQA_KV_HEADS * HEAD_DIM
SSM_WIDTH = 256
SSM_GROUP = 16
SSM_GROUPS = SSM_WIDTH // SSM_GROUP
SSM_STATE = 64
SSM_CHUNK = 8

MLA_HEADS = 8
MLA_Q_RANK = 512
MLA_KV_RANK = 256
MLA_NOPE = 64
MLA_ROPE = 32
MLA_QK = MLA_NOPE + MLA_ROPE
MLA_V = 64
MLA_PAD = 128
V_EXT = 128
NA_HEADS = 8
NA_W = NA_HEADS * HEAD_DIM
NA_WIN_R = 8
NA_WIN_C = 16
NA_ROWS_PER_BLOCK = 4

N_CTX = 256
TM = 512
SUB_ROWS = 256
TQ = 256
KEY_CHUNK = 512
MASK_VALUE = -1e30
VMEM_LIMIT = 52 * 1024 * 1024


def _cparams(n_axes):
    return pltpu.CompilerParams(dimension_semantics=("parallel",) * n_axes, vmem_limit_bytes=VMEM_LIMIT)


def _resident(shape):
    nd = len(shape)
    return pl.BlockSpec(shape, lambda *_: (0,) * nd, pipeline_mode=pl.Buffered(1))


def _dot(a, b):
    return jnp.dot(a, b, preferred_element_type=F32)


def _dot_nt(a, b):
    return lax.dot_general(a, b, (((1,), (1,)), ((), ())), preferred_element_type=F32)


def _split_dot(x, w):
    hi = x.astype(BF16)
    lo = (x - hi.astype(F32)).astype(BF16)
    return _dot(hi, w) + _dot(lo, w)


def _modulate(xf, g, shift, scale):
    ms = jnp.mean(xf * xf, axis=-1, keepdims=True)
    return (xf * lax.rsqrt(ms + EPS) * g) * (1.0 + scale) + shift


def _head_rms(x, down, up, head_dim):
    ss = _split_dot(x * x, down)
    inv = lax.rsqrt(ss * (1.0 / head_dim) + EPS)
    return x * _split_dot(inv, up)


def _rope(x, cos, sin_a, sin_b, shift):
    w = cos.shape[-1]
    blocks = []
    for j in range(x.shape[-1] // w):
        xb = x[:, j * w:(j + 1) * w]
        blocks.append(xb * cos + pltpu.roll(xb, w - shift, 1) * sin_a + pltpu.roll(xb, shift, 1) * sin_b)
    return jnp.concatenate(blocks, axis=1)


def _issue_ahead(project, finish):
    n = TM // SUB_ROWS
    piece = lambda r: slice(r * SUB_ROWS, (r + 1) * SUB_ROWS)
    nxt = project(piece(0))
    for r in range(n):
        h, nxt = nxt, (project(piece(r + 1)) if r + 1 < n else None)
        finish(r, piece(r), h)


class _Rows:
    def __init__(self, batch, n_lat):
        assert (batch * N_CTX) % TM == 0 and n_lat % TM == 0 and (batch * N_CTX) % n_lat == 0
        self.batch, self.n_lat = batch, n_lat
        self.n_ctx_rows = batch * N_CTX
        self.n_rows = self.n_ctx_rows + batch * n_lat
        self.ctx_blocks = self.n_ctx_rows // TM
        self.lat_blocks = batch * n_lat // TM
        self.blocks_per_batch = n_lat // TM

    def mod_index(self, blk):
        return jnp.where(blk < self.ctx_blocks, 0, 1 + (blk - self.ctx_blocks) // self.blocks_per_batch)

    def rope_index(self, blk):
        return jnp.where(blk < self.ctx_blocks, 0, 1 + (blk - self.ctx_blocks) % self.blocks_per_batch)

    def query_block(self, b, i, tq):
        cs = N_CTX // tq
        return jnp.where(i < cs, b * cs + i, self.n_ctx_rows // tq + b * (self.n_lat // tq) + i - cs)

    def latent_block(self, b):
        return self.n_ctx_rows // self.n_lat + b


def _mod_kernel(c_ref, w_ref, b_ref, o_ref):
    c = c_ref[...]
    s = (c * jax.nn.sigmoid(c)).astype(BF16)
    o_ref[0] = _dot(s, w_ref[0].astype(BF16)) + b_ref[0]


def _mod_vectors(cond, w_mod, b_mod):
    depth, d, n = w_mod.shape
    rows = cond.shape[0]
    tn = 1536
    return pl.pallas_call(
        _mod_kernel,
        grid=(depth, n // tn),
        in_specs=[pl.BlockSpec((rows, d), lambda l, j: (0, 0)),
                  pl.BlockSpec((1, d, tn), lambda l, j: (l, 0, j)),
                  pl.BlockSpec((1, 1, tn), lambda l, j: (l, 0, j))],
        out_specs=pl.BlockSpec((1, rows, tn), lambda l, j: (l, 0, j)),
        out_shape=jax.ShapeDtypeStruct((depth, rows, n), F32),
        compiler_params=_cparams(2),
        name="mod_vectors",
    )(cond, w_mod, b_mod.reshape(depth, 1, n))


def _values_with_ones_t(v, n_heads):
    vt = v.T
    ones = jnp.ones((V_EXT - HEAD_DIM, vt.shape[1]), F32)
    pieces = []
    for h in range(n_heads):
        pieces += [vt[h * HEAD_DIM:(h + 1) * HEAD_DIM], ones]
    return jnp.concatenate(pieces, axis=0)


def _stream_reader(refs, ctx_blocks):
    if ctx_blocks is None:
        return (lambda rows: refs[0][rows, :]), refs[1:]
    is_ctx = pl.program_id(0) < ctx_blocks
    return (lambda rows: jnp.where(is_ctx, refs[0][rows, :], refs[1][rows, :])), refs[2:]


def _stream_specs(x, rows):
    if not isinstance(x, tuple):
        return (x,), [pl.BlockSpec((TM, D_MODEL), lambda i: (i, 0))], None
    cb = rows.ctx_blocks
    return x, [pl.BlockSpec((TM, D_MODEL), lambda i: (jnp.minimum(i, cb - 1), 0)),
               pl.BlockSpec((TM, D_MODEL), lambda i: (jnp.maximum(i - cb, 0), 0))], cb


def _even_in_kernel(*refs, ctx_blocks):
    read_x, refs = _stream_reader(refs, ctx_blocks)
    (mod_ref, g1_ref, w_ref, down_ref, up_ref, gqk_ref, perm_ref, cos_ref, sa_ref, sb_ref,
     q_ref, k_ref, v_ref, u_ref) = refs
    m = mod_ref[0]
    nqk = GQA_Q_W + GQA_KV_W
    nv = GQA_KV_W
    nc = SUB_ROWS // SSM_CHUNK

    def project(rows):
        a = _modulate(read_x(rows), g1_ref[...], m[:, 0:D_MODEL], m[:, D_MODEL:2 * D_MODEL]).astype(BF16)
        return _dot(a, w_ref[...])

    def finish(r, rows, h):
        qk = _head_rms(h[:, :nqk], down_ref[...], up_ref[...], HEAD_DIM) * gqk_ref[...]
        qk = _rope(qk, cos_ref[rows, :], sa_ref[rows, :], sb_ref[rows, :], HEAD_DIM // 4)
        q_ref[:, rows] = qk[:, :GQA_Q_W].T.astype(BF16)
        k_ref[rows, :] = qk[:, GQA_Q_W:].astype(BF16)
        v_ref[:, rows] = _values_with_ones_t(h[:, nqk:nqk + nv], GQA_KV_HEADS).astype(BF16)
        us = _dot(perm_ref[...], h[:, nqk + nv:].astype(BF16))
        u_ref[r * nc:(r + 1) * nc, :] = jnp.concatenate(
            [us[s * nc:(s + 1) * nc] for s in range(SSM_CHUNK)], axis=1).astype(BF16)

    _issue_ahead(project, finish)


def _mod_spec(rows):
    return pl.BlockSpec((1, 1, N_MOD * D_MODEL), lambda i: (rows.mod_index(i), 0, 0))


def _even_in_proj(x, modtab, consts, tabs, rows):
    row_spec = lambda w: pl.BlockSpec((TM, w), lambda i: (i, 0))
    tab_spec = pl.BlockSpec((TM, 128), lambda i: (rows.rope_index(i), 0))
    col_spec = lambda w: pl.BlockSpec((w, TM), lambda i: (0, i))
    v_w = GQA_KV_HEADS * V_EXT
    chunk_rows, chunk_w = TM // SSM_CHUNK, SSM_CHUNK * SSM_WIDTH
    sds = jax.ShapeDtypeStruct
    x_parts, x_specs, ctx_blocks = _stream_specs(x, rows)
    return pl.pallas_call(
        functools.partial(_even_in_kernel, ctx_blocks=ctx_blocks),
        grid=(rows.n_rows // TM,),
        in_specs=x_specs + [_mod_spec(rows)] + [_resident(c.shape) for c in consts] + [tab_spec] * 3,
        out_specs=[col_spec(GQA_Q_W), row_spec(GQA_KV_W), col_spec(v_w),
                   pl.BlockSpec((chunk_rows, chunk_w), lambda i: (i, 0))],
        out_shape=[sds((GQA_Q_W, rows.n_rows), BF16), sds((rows.n_rows, GQA_KV_W), BF16),
                   sds((v_w, rows.n_rows), BF16), sds((rows.n_rows // SSM_CHUNK, chunk_w), BF16)],
        compiler_params=_cparams(1),
        name="even_in_proj",
    )(*x_parts, modtab, *consts, *tabs)


def _odd_in_kernel(x_ref, mod_ref, g1_ref, w_ref, gcq_ref, gckv_ref, wuq_ref, wuk_ref, wuv_ref,
                   gmq_ref, gmk_ref, down_ref, up_ref, gnqk_ref, cos_ref, sa_ref, sb_ref,
                   mq_ref, mk_ref, mv_ref, nq_ref, nk_ref, nv_ref):
    m = mod_ref[0]
    c1 = MLA_Q_RANK
    c2 = c1 + MLA_KV_RANK
    c3 = c2 + MLA_PAD

    def rms(t, g):
        return (t * lax.rsqrt(jnp.mean(t * t, axis=-1, keepdims=True) + EPS) * g).astype(BF16)

    def project(rows):
        a = _modulate(x_ref[rows, :], g1_ref[...], m[:, 0:D_MODEL], m[:, D_MODEL:2 * D_MODEL]).astype(BF16)
        return _dot(a, w_ref[...])

    def finish(r, rows, h):
        q = _dot(rms(h[:, :c1], gcq_ref[...]), wuq_ref[...])
        ckv = rms(h[:, c1:c2], gckv_ref[...])
        k = _dot(ckv, wuk_ref[...]) + jnp.tile(h[:, c2:c3], (1, MLA_HEADS))
        mv_ref[:, rows] = _values_with_ones_t(_dot(ckv, wuv_ref[...]), MLA_HEADS).astype(BF16)

        def mla_heads(t, g):
            parts = []
            for hh in range(MLA_HEADS):
                th = t[:, hh * MLA_PAD:(hh + 1) * MLA_PAD]
                ss = jnp.sum(th * th, axis=-1, keepdims=True)
                parts.append(th * lax.rsqrt(ss * (1.0 / MLA_QK) + EPS))
            t = jnp.concatenate(parts, axis=1) * g
            return _rope(t, cos_ref[rows, :], sa_ref[rows, :], sb_ref[rows, :], MLA_ROPE // 4)

        mq_ref[:, rows] = mla_heads(q, gmq_ref[...]).T.astype(BF16)
        mk_ref[rows, :] = mla_heads(k, gmk_ref[...]).astype(BF16)

        nqk = _head_rms(h[:, c3:c3 + 2 * NA_W], down_ref[...], up_ref[...], HEAD_DIM) * gnqk_ref[...]
        nq_ref[:, rows] = nqk[:, :NA_W].T.astype(BF16)
        nk_ref[rows, :] = nqk[:, NA_W:].astype(BF16)
        nv_ref[:, rows] = _values_with_ones_t(h[:, c3 + 2 * NA_W:], NA_HEADS).astype(BF16)

    _issue_ahead(project, finish)


def _odd_in_proj(x, modtab, consts, tabs, rows):
    row_spec = lambda w: pl.BlockSpec((TM, w), lambda i: (i, 0))
    tab_spec = pl.BlockSpec((TM, 128), lambda i: (rows.rope_index(i), 0))
    col_spec = lambda w: pl.BlockSpec((w, TM), lambda i: (0, i))
    mla_w = MLA_HEADS * MLA_PAD
    v_w = MLA_HEADS * V_EXT
    rows_out = lambda w: jax.ShapeDtypeStruct((rows.n_rows, w), BF16)
    cols_out = lambda w: jax.ShapeDtypeStruct((w, rows.n_rows), BF16)
    return pl.pallas_call(
        _odd_in_kernel,
        grid=(rows.n_rows // TM,),
        in_specs=[row_spec(D_MODEL), _mod_spec(rows)] + [_resident(c.shape) for c in consts] + [tab_spec] * 3,
        out_specs=[col_spec(mla_w), row_spec(mla_w), col_spec(v_w),
                   col_spec(NA_W), row_spec(NA_W), col_spec(NA_HEADS * V_EXT)],
        out_shape=[cols_out(mla_w), rows_out(mla_w), cols_out(v_w),
                   cols_out(NA_W), rows_out(NA_W), cols_out(NA_HEADS * V_EXT)],
        compiler_params=_cparams(1),
        name="odd_in_proj",
    )(x, modtab, *consts, *tabs)


def _key_max8(s, ways=4):
    n = s.shape[0] // ways
    parts = [jnp.max(s[i * n:(i + 1) * n].reshape(n // 8, 8, s.shape[1]), axis=0) for i in range(ways)]
    while len(parts) > 1:
        parts = [jnp.maximum(a, b) for a, b in zip(parts[::2], parts[1::2])]
    return parts[0]


def _softmax_heads(n_heads, scores, values, o_ref):
    def probabilities(s_parts):
        m = jnp.max(functools.reduce(jnp.maximum, [_key_max8(s) for s in s_parts]), axis=0, keepdims=True)
        return [jnp.exp2(s - m).astype(BF16) for s in s_parts]

    def weighted_values(h, p_parts):
        acc = None
        for p, vt in zip(p_parts, values(h)):
            part = _dot(vt, p)
            acc = part if acc is None else acc + part
        return acc[:HEAD_DIM] / acc[HEAD_DIM:]

    outs = []
    s_cur, p_prev = scores(0), None
    for h in range(n_heads):
        s_next = scores(h + 1) if h + 1 < n_heads else None
        p_cur = probabilities(s_cur)
        if p_prev is not None:
            outs.append(weighted_values(h - 1, p_prev))
        s_cur, p_prev = s_next, p_cur
    outs.append(weighted_values(n_heads - 1, p_prev))
    o_ref[...] = jnp.concatenate(outs, axis=0).T.astype(o_ref.dtype)


def _attn_kernel(qt_ref, kc_ref, kl_ref, vtc_ref, vtl_ref, o_ref, *, n_heads, group, dk, first_step):
    tq = qt_ref.shape[1]
    n_lat = kl_ref.shape[0]
    k_width = kc_ref.shape[1]
    pad_q = k_width <= 256

    def scores(h, with_latent):
        kv = h // group
        qt = qt_ref[h * dk:(h + 1) * dk, :]
        if pad_q:
            pieces = []
            if kv:
                pieces.append(jnp.zeros((kv * dk, tq), qt.dtype))
            pieces.append(qt)
            if k_width - (kv + 1) * dk:
                pieces.append(jnp.zeros((k_width - (kv + 1) * dk, tq), qt.dtype))
            qt = jnp.concatenate(pieces, axis=0)
            ksl = slice(None)
        else:
            ksl = slice(kv * dk, (kv + 1) * dk)
        parts = [_dot(kc_ref[:, ksl], qt)]
        if with_latent:
            parts += [_dot(kl_ref[c * KEY_CHUNK:(c + 1) * KEY_CHUNK, ksl], qt) for c in range(n_lat // KEY_CHUNK)]
        return parts

    def values(h, with_latent):
        vsl = slice((h // group) * V_EXT, (h // group + 1) * V_EXT)
        parts = [vtc_ref[vsl, :]]
        if with_latent:
            parts += [vtl_ref[vsl, c * KEY_CHUNK:(c + 1) * KEY_CHUNK] for c in range(n_lat // KEY_CHUNK)]
        return parts

    def attend(with_latent):
        _softmax_heads(n_heads, lambda h: scores(h, with_latent), lambda h: values(h, with_latent), o_ref)

    if first_step == 0:
        is_ctx = pl.program_id(1) == 0

        @pl.when(is_ctx)
        def _():
            attend(False)

        @pl.when(jnp.logical_not(is_ctx))
        def _():
            attend(True)
    else:
        attend(True)


def _attention(qt, k, vt, rows, *, n_heads, group, dk, skip_ctx_queries):
    first = 1 if skip_ctx_queries else 0
    kern = functools.partial(_attn_kernel, n_heads=n_heads, group=group, dk=dk, first_step=first)
    wo = n_heads * HEAD_DIM
    return pl.pallas_call(
        kern,
        grid=(rows.batch, rows.n_lat // TQ + 1 - first),
        in_specs=[pl.BlockSpec((qt.shape[0], TQ), lambda b, i: (0, rows.query_block(b, i + first, TQ))),
                  pl.BlockSpec((N_CTX, k.shape[1]), lambda b, i: (b, 0)),
                  pl.BlockSpec((rows.n_lat, k.shape[1]), lambda b, i: (rows.latent_block(b), 0)),
                  pl.BlockSpec((vt.shape[0], N_CTX), lambda b, i: (0, b)),
                  pl.BlockSpec((vt.shape[0], rows.n_lat), lambda b, i: (0, rows.latent_block(b)))],
        out_specs=pl.BlockSpec((TQ, wo), lambda b, i: (rows.query_block(b, i + first, TQ), 0)),
        out_shape=jax.ShapeDtypeStruct((rows.n_rows, wo), BF16),
        compiler_params=_cparams(2),
        name="attention_dk%d" % dk,
    )(qt, k, k, vt, vt)


def _na_geometry(grid_rows):
    rb = NA_ROWS_PER_BLOCK
    wr = min(NA_WIN_R, grid_rows)
    assert grid_rows % 2 == 0 and GRID_W * 2 == 128
    span = min(grid_rows, wr + rb - 1 + (wr + rb - 1) % 2)
    starts, variants, keys = [], [], {}
    for r0 in range(0, grid_rows, rb):
        rs0 = int(np.clip(r0 - wr // 2, 0, grid_rows - wr))
        start = min(rs0, grid_rows - span)
        start -= start % 2
        rel = tuple(int(np.clip(r0 + dr - wr // 2, 0, grid_rows - wr)) - (r0 + dr) for dr in range(rb))
        key = (rel, start - r0)
        variants.append(keys.setdefault(key, len(keys)))
        starts.append(start)
    firsts = [variants.index(v) for v in range(len(keys))]
    return wr, span, starts, variants, firsts


def _na_bias_table(rpb, grid_rows):
    rb = NA_ROWS_PER_BLOCK
    wr, span, starts, variants, firsts = _na_geometry(grid_rows)
    nv = len(firsts)
    idx_r = np.zeros((nv, rb, span), np.int32)
    ok_r = np.zeros((nv, rb, span), bool)
    for v, blk in enumerate(firsts):
        r0, start = blk * rb, starts[blk]
        for dr in range(rb):
            r = r0 + dr
            rs = int(np.clip(r - wr // 2, 0, grid_rows - wr))
            for j in range(span):
                kr = start + j
                ok_r[v, dr, j] = rs <= kr < rs + wr
                idx_r[v, dr, j] = np.clip(kr - r + NA_WIN_R - 1, 0, 2 * NA_WIN_R - 2)
    c = np.arange(GRID_W)
    cs = np.clip(c - NA_WIN_C // 2, 0, GRID_W - NA_WIN_C)
    kc = np.arange(GRID_W)
    ok_c = (kc[None, :] >= cs[:, None]) & (kc[None, :] < cs[:, None] + NA_WIN_C)
    idx_c = np.clip(kc[None, :] - c[:, None] + NA_WIN_C - 1, 0, 2 * NA_WIN_C - 2)
    n_rel_c = 2 * NA_WIN_C - 1
    picked = rpb[:, idx_r.reshape(-1)].reshape(rpb.shape[0], nv, rb, span, n_rel_c)
    picked = jnp.transpose(picked, (1, 0, 3, 2, 4)).reshape(nv, rpb.shape[0], span, rb * n_rel_c) * LOG2E
    col_hit = idx_c.T[None, :, :] == np.arange(n_rel_c)[:, None, None]
    spread = (np.eye(rb, dtype=bool)[:, None, None, :, None] & col_hit[None, :, :, None, :])
    spread = jnp.asarray(spread.reshape(rb * n_rel_c, GRID_W, rb, GRID_W), F32)
    bias = jnp.einsum("vhjy,ykdc->vhjkdc", picked, spread, precision=lax.Precision.HIGHEST)
    ok = ok_r[:, None, :, None, :, None] & ok_c[None, None, None, :, None, :]
    ok = np.transpose(ok, (0, 1, 4, 5, 2, 3))
    bias = jnp.where(ok, bias, MASK_VALUE)
    return bias.reshape(nv, rpb.shape[0], span * GRID_W, rb * GRID_W)


def _na_kernel(start_ref, var_ref, qt_ref, kc_ref, kl_ref, vtc_ref, vtl_ref, bias_ref, o_ref, *, n_loc, first_step):
    i = pl.program_id(1) + first_step
    tq = qt_ref.shape[1]

    def padded_query(h):
        qt = qt_ref[h * HEAD_DIM:(h + 1) * HEAD_DIM, :]
        zeros = jnp.zeros((HEAD_DIM, tq), qt.dtype)
        return jnp.concatenate([qt, zeros] if h % 2 == 0 else [zeros, qt], axis=0)

    def pair_lanes(h):
        return slice((h // 2) * 2 * HEAD_DIM, (h // 2 + 1) * 2 * HEAD_DIM)

    def value_rows(h):
        return slice(h * V_EXT, (h + 1) * V_EXT)

    def context_queries():
        _softmax_heads(NA_HEADS, lambda h: [_dot(kc_ref[:, pair_lanes(h)], padded_query(h))],
                       lambda h: [vtc_ref[value_rows(h), :]], o_ref)

    def latent_queries():
        off = pl.multiple_of(start_ref[i - 1] * GRID_W, 2 * GRID_W)
        var = var_ref[i - 1]

        def scores(h):
            qt = padded_query(h)
            return [_dot(kc_ref[:, pair_lanes(h)], qt),
                    _dot(kl_ref[pl.ds(off, n_loc), pair_lanes(h)], qt) + bias_ref[var, h]]

        _softmax_heads(NA_HEADS, scores,
                       lambda h: [vtc_ref[value_rows(h), :], vtl_ref[value_rows(h), pl.ds(off, n_loc)]], o_ref)

    if first_step == 0:
        pl.when(i == 0)(context_queries)
        pl.when(i > 0)(latent_queries)
    else:
        latent_queries()


def _neighbourhood_attention(qt, k, vt, bias, rows, skip_ctx_queries):
    grid_rows = rows.n_lat // GRID_W
    _, span, starts, variants, _ = _na_geometry(grid_rows)
    assert NA_ROWS_PER_BLOCK * GRID_W == TQ
    first = 1 if skip_ctx_queries else 0
    kern = functools.partial(_na_kernel, n_loc=span * GRID_W, first_step=first)
    qblock = lambda b, i: rows.query_block(b, i + first, TQ)
    v_w = vt.shape[0]
    grid_spec = pltpu.PrefetchScalarGridSpec(
        num_scalar_prefetch=2,
        grid=(rows.batch, rows.n_lat // TQ + 1 - first),
        in_specs=[pl.BlockSpec((NA_W, TQ), lambda b, i, *_: (0, qblock(b, i))),
                  pl.BlockSpec((N_CTX, NA_W), lambda b, i, *_: (b, 0)),
                  pl.BlockSpec((rows.n_lat, NA_W), lambda b, i, *_: (rows.latent_block(b), 0)),
                  pl.BlockSpec((v_w, N_CTX), lambda b, i, *_: (0, b)),
                  pl.BlockSpec((v_w, rows.n_lat), lambda b, i, *_: (0, rows.latent_block(b))),
                  pl.BlockSpec(bias.shape, lambda b, i, *_: (0, 0, 0, 0), pipeline_mode=pl.Buffered(1))],
        out_specs=pl.BlockSpec((TQ, NA_W), lambda b, i, *_: (qblock(b, i), 0)),
    )
    return pl.pallas_call(
        kern,
        grid_spec=grid_spec,
        out_shape=jax.ShapeDtypeStruct((rows.n_rows, NA_W), BF16),
        compiler_params=_cparams(2),
        name="neighbourhood_attention",
    )(jnp.asarray(starts, jnp.int32), jnp.asarray(variants, jnp.int32), qt, k, k, vt, vt, bias)


def _s5_matrices(lam_re, lam_im, log_dt, b_re, b_im, c_re, c_im, d_skip):
    L, G, P, N = SSM_CHUNK, SSM_GROUPS, SSM_GROUP, SSM_STATE
    hi = lax.Precision.HIGHEST
    dt = jnp.exp(log_dt)[..., None]
    pw = jnp.arange(L + 1, dtype=F32)[:, None, None, None]
    mag = jnp.exp(lam_re * dt * pw)
    e_re = mag * jnp.cos(lam_im * dt * pw)
    e_im = mag * jnp.sin(lam_im * dt * pw)
    a_re, a_im = e_re[1], e_im[1]
    den = jnp.square(lam_re) + jnp.square(lam_im)
    f_re = ((a_re - 1.0) * lam_re + a_im * lam_im) / den
    f_im = (a_im * lam_re - (a_re - 1.0) * lam_im) / den
    bb_re = f_re[..., None] * b_re - f_im[..., None] * b_im
    bb_im = f_re[..., None] * b_im + f_im[..., None] * b_re
    ce_re = c_re[None] * e_re[:, :, :, None, :] - c_im[None] * e_im[:, :, :, None, :]
    ce_im = c_re[None] * e_im[:, :, :, None, :] + c_im[None] * e_re[:, :, :, None, :]
    kk = (jnp.einsum("kdgqn,dgnp->kdgqp", ce_re, bb_re, precision=hi)
          - jnp.einsum("kdgqn,dgnp->kdgqp", ce_im, bb_im, precision=hi))
    lag = np.arange(L)[None, :] - np.arange(L)[:, None]
    kf = kk[np.clip(lag, 0, L - 1), 0]
    kb = kk[np.clip(-lag, 0, L - 1), 1]
    skip = jnp.eye(P, dtype=F32)[None] * d_skip.reshape(G, P)[:, None, :]
    m = lambda cond: jnp.asarray(cond, F32)[:, :, None, None, None]
    kst = m(lag >= 0) * kf + m(lag <= 0) * kb + m(lag == 0) * skip[None, None]
    a_t = jnp.transpose(kst, (0, 2, 4, 1, 3)).reshape(L * G * P, L * P)

    def state_in(pows, d):
        x_re = e_re[pows, d][..., None] * bb_re[d][None] - e_im[pows, d][..., None] * bb_im[d][None]
        x_im = e_re[pows, d][..., None] * bb_im[d][None] + e_im[pows, d][..., None] * bb_re[d][None]
        flat = lambda x: jnp.pad(jnp.transpose(x, (0, 1, 3, 2)).reshape(L * G * P, N), ((0, 0), (0, L * P - N)))
        return [flat(x_re), flat(x_im)]

    a_w1 = jnp.concatenate([a_t] + state_in(np.arange(L)[::-1].copy(), 0) + state_in(np.arange(L), 1), axis=1)
    w1 = _expand_group_blocks(a_w1[None], L * G * P + 4 * G * N, row_shift=4)

    def state_out(x):
        return jnp.transpose(x, (1, 3, 0, 2)).reshape(G * N, L * P)

    pf, pb = np.arange(1, L + 1), np.arange(L, 0, -1)
    a_wc = jnp.stack([jnp.concatenate([state_out(ce_re[pf, 0]), state_out(ce_re[pb, 1])], axis=0),
                      jnp.concatenate([state_out(-ce_im[pf, 0]), state_out(-ce_im[pb, 1])], axis=0)])
    wc = _expand_group_blocks(a_wc, L * G * P, row_shift=6)
    al_re = e_re[L].reshape(2, 1, G * N)
    al_im = e_im[L].reshape(2, 1, G * N)
    return w1[0], wc[0], wc[1], al_re, al_im


def _expand_kernel(a_ref, o_ref, *, row_shift, n_response_tiles):
    j = pl.program_id(1)
    a = a_ref[0].astype(BF16)
    n_rows, tn = o_ref.shape[1], o_ref.shape[2]
    src = lax.broadcasted_iota(jnp.int32, (a.shape[1], tn), 0)
    col = lax.broadcasted_iota(jnp.int32, (a.shape[1], tn), 1)
    row_group = (lax.broadcasted_iota(jnp.int32, (n_rows, tn), 0) >> row_shift) & (SSM_GROUPS - 1)
    out_col = lax.broadcasted_iota(jnp.int32, (n_rows, tn), 1)

    def emit(spread, col_group):
        val = _dot(a, jnp.where(spread, 1.0, 0.0).astype(BF16))
        o_ref[0] = jnp.where(row_group == col_group, val, 0.0).astype(o_ref.dtype)

    @pl.when(j < n_response_tiles)
    def _():
        cg = col + j * tn
        spread = ((src >> 4) == (cg >> 8)) & ((src & 15) == (cg & 15))
        emit(spread, ((out_col + j * tn) >> 4) & (SSM_GROUPS - 1))

    @pl.when(j >= n_response_tiles)
    def _():
        spread = src == (col & (SSM_STATE - 1))
        emit(spread, (out_col >> 6) & (SSM_GROUPS - 1))


def _expand_group_blocks(a, n_cols, *, row_shift):
    k, n_rows, _ = a.shape
    tn = 1024
    n_response_tiles = SSM_CHUNK * SSM_WIDTH // tn
    assert SSM_GROUP == 16 and SSM_STATE == 64 and SSM_CHUNK * SSM_GROUP == 128 and tn % (SSM_GROUPS * SSM_STATE) == 0
    kern = functools.partial(_expand_kernel, row_shift=row_shift, n_response_tiles=n_response_tiles)
    return pl.pallas_call(
        kern,
        grid=(k, n_cols // tn),
        in_specs=[pl.BlockSpec((1, n_rows, 128), lambda d, j: (d, 0, jnp.maximum(j - n_response_tiles + 1, 0)))],
        out_specs=pl.BlockSpec((1, n_rows, tn), lambda d, j: (d, 0, j)),
        out_shape=jax.ShapeDtypeStruct((k, n_rows, n_cols), BF16),
        compiler_params=_cparams(2),
        name="s5_expand",
    )(a)


def _mm_kernel(a_ref, b_ref, o_ref):
    o_ref[...] = _dot(a_ref[...], b_ref[...]).astype(o_ref.dtype)


def _matmul(a, b, tm, tn, out_dtype):
    m, kdim = a.shape
    n = b.shape[1]
    return pl.pallas_call(
        _mm_kernel,
        grid=(n // tn, m // tm),
        in_specs=[pl.BlockSpec((tm, kdim), lambda j, i: (i, 0)),
                  pl.BlockSpec((kdim, tn), lambda j, i: (0, j))],
        out_specs=pl.BlockSpec((tm, tn), lambda j, i: (i, j)),
        out_shape=jax.ShapeDtypeStruct((m, n), out_dtype),
        compiler_params=_cparams(2),
        name="s5_chunk_matmul",
    )(a, b)


def _s5_carry_kernel(sre_ref, sim_ref, are_ref, aim_ref, hre_ref, him_ref, *, ctx_chunks):
    n_chunks = sre_ref.shape[0]
    backward = pl.program_id(0) == 1
    a_re = jnp.broadcast_to(are_ref[0], sre_ref.shape[1:])
    a_im = jnp.broadcast_to(aim_ref[0], sre_ref.shape[1:])

    def step(j, carry):
        h_re, h_im = carry
        cb = jnp.where(j < ctx_chunks, ctx_chunks - 1 - j, n_chunks + ctx_chunks - 1 - j)
        c = jnp.where(backward, cb, j)
        hre_ref[c] = h_re.astype(hre_ref.dtype)
        him_ref[c] = h_im.astype(him_ref.dtype)
        n_re = a_re * h_re - a_im * h_im + sre_ref[c]
        n_im = a_re * h_im + a_im * h_re + sim_ref[c]
        return n_re, n_im

    zero = jnp.zeros(sre_ref.shape[1:], F32)
    lax.fori_loop(0, n_chunks, step, (zero, zero))


def _s5_carry(ys, al_re, al_im, n_chunks, batch, ctx_chunks):
    gn = SSM_GROUPS * SSM_STATE
    tw = 256
    base = SSM_CHUNK * SSM_WIDTH // tw
    per_dir = 2 * gn // tw
    s3 = ys.reshape(n_chunks, batch, ys.shape[1])
    kern = functools.partial(_s5_carry_kernel, ctx_chunks=ctx_chunks)
    blk = (n_chunks, batch, tw)
    return pl.pallas_call(
        kern,
        grid=(2, gn // tw),
        in_specs=[pl.BlockSpec(blk, lambda d, j: (0, 0, base + d * per_dir + j)),
                  pl.BlockSpec(blk, lambda d, j: (0, 0, base + d * per_dir + gn // tw + j)),
                  pl.BlockSpec((1, 1, tw), lambda d, j: (d, 0, j)),
                  pl.BlockSpec((1, 1, tw), lambda d, j: (d, 0, j))],
        out_specs=[pl.BlockSpec(blk, lambda d, j: (0, 0, d * (gn // tw) + j))] * 2,
        out_shape=[jax.ShapeDtypeStruct((n_chunks, batch, 2 * gn), BF16)] * 2,
        compiler_params=_cparams(2),
        name="s5_carry",
    )(s3, s3, al_re, al_im)


def _s5_readout_kernel(hre_ref, him_ref, wre_ref, wim_ref, y_ref, o_ref):
    o_ref[...] = y_ref[...] + _dot(hre_ref[...], wre_ref[...]) + _dot(him_ref[...], wim_ref[...])


def _s5_readout(h_re, h_im, wc_re, wc_im, ys, tm):
    m, kdim = h_re.shape
    n = wc_re.shape[1]
    tn = 1024
    return pl.pallas_call(
        _s5_readout_kernel,
        grid=(n // tn, m // tm),
        in_specs=[pl.BlockSpec((tm, kdim), lambda j, i: (i, 0)),
                  pl.BlockSpec((tm, kdim), lambda j, i: (i, 0)),
                  pl.BlockSpec((kdim, tn), lambda j, i: (0, j)),
                  pl.BlockSpec((kdim, tn), lambda j, i: (0, j)),
                  pl.BlockSpec((tm, tn), lambda j, i: (i, j))],
        out_specs=pl.BlockSpec((tm, tn), lambda j, i: (i, j)),
        out_shape=jax.ShapeDtypeStruct((m, n), F32),
        compiler_params=_cparams(2),
        name="s5_readout",
    )(h_re, h_im, wc_re, wc_im, ys)


def _row_tile(m, target=512):
    t = min(m, target)
    while m % t or t % 16:
        t -= 16
    return t


def _s5_mixer(u, mats, rows):
    w1, wc_re, wc_im, al_re, al_im = mats
    batch = rows.batch
    cw = SSM_CHUNK * SSM_WIDTH
    ctx_chunks = N_CTX // SSM_CHUNK
    n_chunks = ctx_chunks + rows.n_lat // SSM_CHUNK
    n_ctx_chunk_rows = rows.n_ctx_rows // SSM_CHUNK
    uc = jnp.concatenate([u[:n_ctx_chunk_rows].reshape(batch, ctx_chunks, cw),
                          u[n_ctx_chunk_rows:].reshape(batch, n_chunks - ctx_chunks, cw)], axis=1)
    uc = jnp.transpose(uc, (1, 0, 2)).reshape(n_chunks * batch, cw)
    tm = _row_tile(n_chunks * batch)
    ys = _matmul(uc, w1, tm, 1024, F32)
    h_re, h_im = _s5_carry(ys, al_re, al_im, n_chunks, batch, ctx_chunks)
    gn2 = 2 * SSM_GROUPS * SSM_STATE
    y = _s5_readout(h_re.reshape(-1, gn2), h_im.reshape(-1, gn2), wc_re, wc_im, ys, tm)
    y = jnp.transpose(y.reshape(n_chunks, batch, cw), (1, 0, 2))
    return jnp.concatenate([y[:, :ctx_chunks].reshape(n_ctx_chunk_rows, cw),
                            y[:, ctx_chunks:].reshape(-1, cw)], axis=0)


def _gelu_tanh(y):
    return 0.5 * y * (1.0 + jnp.tanh(math.sqrt(2.0 / math.pi) * (y + 0.044715 * (y * y * y))))


def _out_mlp_kernel(*refs, even, ctx_blocks):
    read_x, refs = _stream_reader(refs, ctx_blocks)
    if even:
        m1_ref, m2_ref, mod_ref, perm_ref, wglu_ref, bglu_ref, wo_ref, g2_ref, w1_ref, w2_ref, o_ref = refs
        yc = m2_ref[...]
        ys = jnp.concatenate([yc[:, s * SSM_WIDTH:(s + 1) * SSM_WIDTH] for s in range(SSM_CHUNK)], axis=0)
        hi = ys.astype(BF16)
        lo = (ys - hi.astype(F32)).astype(BF16)
        y = _gelu_tanh(_dot(perm_ref[...], hi) + _dot(perm_ref[...], lo))
        z = _dot(y.astype(BF16), wglu_ref[...]) + bglu_ref[...]
        second = (y * jax.nn.sigmoid(z)).astype(BF16)
    else:
        m1_ref, m2_ref, mod_ref, wo_ref, g2_ref, w1_ref, w2_ref, o_ref = refs
        second = m2_ref[...]
    m = mod_ref[0]
    mod = lambda j: m[:, j * D_MODEL:(j + 1) * D_MODEL]
    mix = jnp.concatenate([m1_ref[...], second], axis=1)
    x1 = read_x(slice(None)) + mod(2) * _dot(mix, wo_ref[...])
    a = _modulate(x1, g2_ref[...], mod(3), mod(4)).astype(BF16)
    acc = None
    ck = 1024
    for c in range(D_FF // ck):
        h = jnp.maximum(_dot(a, w1_ref[:, c * ck:(c + 1) * ck]), 0.0)
        part = _dot((h * h).astype(BF16), w2_ref[c * ck:(c + 1) * ck, :])
        acc = part if acc is None else acc + part
    o_ref[...] = x1 + mod(5) * acc


def _out_mlp(x, mix1, mix2, modtab, consts, rows, even, latent_only):
    first = rows.ctx_blocks if latent_only else 0
    row_spec = lambda w: pl.BlockSpec((TM, w), lambda i: (i + first, 0))
    n_blocks = rows.n_rows // TM - first
    mix2_rows = TM // SSM_CHUNK if even else TM
    if isinstance(x, tuple):
        assert not latent_only
        x_parts, x_specs, ctx_blocks = _stream_specs(x, rows)
    else:
        x_parts, x_specs, ctx_blocks = (x,), [row_spec(D_MODEL)], None
    return pl.pallas_call(
        functools.partial(_out_mlp_kernel, even=even, ctx_blocks=ctx_blocks),
        grid=(n_blocks,),
        in_specs=x_specs + [row_spec(mix1.shape[1]),
                  pl.BlockSpec((mix2_rows, mix2.shape[1]), lambda i: (i + first, 0)),
                  pl.BlockSpec((1, 1, N_MOD * D_MODEL), lambda i: (rows.mod_index(i + first), 0, 0))]
                 + [_resident(c.shape) for c in consts],
        out_specs=pl.BlockSpec((TM, D_MODEL), lambda i: (i, 0)),
        out_shape=jax.ShapeDtypeStruct((n_blocks * TM, D_MODEL), F32),
        compiler_params=_cparams(1),
        name="out_mlp_even" if even else "out_mlp_odd",
    )(*x_parts, mix1, mix2, modtab, *consts)


def _rope_tables(n_lat, rot_dim, head_w, lane_off):
    t = jnp.arange(n_lat)
    grid_r = (t // GRID_W).astype(F32)
    grid_c = (t % GRID_W).astype(F32)
    axis_dim = rot_dim // 2
    freqs = ROPE_BASE ** (-jnp.arange(0, axis_dim, 2, dtype=F32) / axis_dim)
    ang_r = grid_r[:, None] * freqs
    ang_c = grid_c[:, None] * freqs
    ang = jnp.concatenate([ang_r, ang_r, ang_c, ang_c], axis=-1)
    cos, sin = jnp.cos(ang), jnp.sin(ang)
    quarter = rot_dim // 4
    first = (np.arange(rot_dim) // quarter) % 2 == 0
    sin_a = jnp.where(first, -sin, 0.0)
    sin_b = jnp.where(first, 0.0, sin)

    def widen(tab, fill):
        full = jnp.full((n_lat, head_w), fill, F32).at[:, lane_off:lane_off + rot_dim].set(tab)
        full = jnp.concatenate([jnp.full((TM, head_w), fill, F32), full], axis=0)
        return jnp.tile(full, (1, 128 // head_w))

    return widen(cos, 1.0), widen(sin_a, 0.0), widen(sin_b, 0.0)


def _head_sum_matrices(n_heads, head_dim):
    down = np.kron(np.eye(n_heads), np.ones((head_dim, 1)))
    pad = (-n_heads) % 128
    down = np.pad(down, ((0, 0), (0, pad)))
    return jnp.asarray(down, BF16), jnp.asarray(down.T, BF16)


def _pad_heads(w, n_heads, width, pad_to):
    lead = w.shape[:-1]
    w = w.reshape(lead + (n_heads, width))
    w = jnp.pad(w, [(0, 0)] * len(lead) + [(0, 0), (0, pad_to - width)])
    return w.reshape(lead + (n_heads * pad_to,))


def kernel(x, c, ctx, c_ctx, w_mod, b_mod, g_norm1, g_norm2, w_ff1, w_ff2, e_w_in, e_w_out, e_g_q, e_g_k, ssm_lam_re, ssm_lam_im, ssm_log_dt, ssm_b_re, ssm_b_im, ssm_c_re, ssm_c_im, ssm_d, ssm_w_glu, ssm_b_glu, o_w_in, o_w_out, mla_g_cq, mla_g_ckv, mla_w_uq, mla_w_ukv, mla_g_q, mla_g_k, na_g_q, na_g_k, na_rpb):
    batch, n_lat, d = x.shape
    depth = w_mod.shape[0]
    assert d == D_MODEL and ctx.shape[1] == N_CTX and n_lat % GRID_W == 0
    rows = _Rows(batch, n_lat)

    pad_rows = (-(batch + 1)) % 8
    cond = jnp.concatenate([c_ctx[None], c, jnp.zeros((pad_rows, d), F32)], axis=0)
    modtabs = _mod_vectors(cond, w_mod, b_mod).reshape(depth, -1, 1, N_MOD * d)

    xs = (ctx.reshape(rows.n_ctx_rows, d), x.reshape(batch * n_lat, d))

    even_tabs = _rope_tables(n_lat, HEAD_DIM, HEAD_DIM, 0)
    mla_tabs = _rope_tables(n_lat, MLA_ROPE, MLA_PAD, MLA_NOPE)
    down_e, up_e = _head_sum_matrices(GQA_Q_HEADS + GQA_KV_HEADS, HEAD_DIM)
    down_o, up_o = _head_sum_matrices(2 * NA_HEADS, HEAD_DIM)
    row = lambda v: v.reshape(1, -1)

    def chunk_perm(n):
        tok = np.arange(n)
        mat = np.zeros((n, n), np.float32)
        mat[(tok % SSM_CHUNK) * (n // SSM_CHUNK) + tok // SSM_CHUNK, tok] = 1.0
        return mat

    perm = jnp.asarray(chunk_perm(SUB_ROWS), BF16)
    perm_t = jnp.asarray(chunk_perm(TM).T, BF16)

    for i in range(depth):
        j = i // 2
        last = i == depth - 1
        g1, g2 = row(g_norm1[i]), row(g_norm2[i])
        w1, w2 = w_ff1[i].astype(BF16), w_ff2[i].astype(BF16)
        if i % 2 == 0:
            gqk = jnp.concatenate([jnp.tile(e_g_q[j], GQA_Q_HEADS) * (HEAD_DIM ** -0.5 * LOG2E),
                                   jnp.tile(e_g_k[j], GQA_KV_HEADS)])
            q, k, v, u = _even_in_proj(xs, modtabs[i], (g1, e_w_in[j].astype(BF16), down_e, up_e, row(gqk), perm),
                                       even_tabs, rows)
            att = _attention(q, k, v, rows, n_heads=GQA_Q_HEADS, group=GQA_GROUP, dk=HEAD_DIM, skip_ctx_queries=last)
            mats = _s5_matrices(ssm_lam_re[j], ssm_lam_im[j], ssm_log_dt[j], ssm_b_re[j], ssm_b_im[j],
                                ssm_c_re[j], ssm_c_im[j], ssm_d[j])
            y = _s5_mixer(u, mats, rows)
            consts = (perm_t, ssm_w_glu[j].astype(BF16), row(ssm_b_glu[j]), e_w_out[j].astype(BF16), g2, w1, w2)
            xs = _out_mlp(xs, att, y, modtabs[i], consts, rows, True, last)
        else:
            w_in = o_w_in[j]
            c1 = MLA_Q_RANK
            c2 = c1 + MLA_KV_RANK
            c3 = c2 + MLA_ROPE
            kr_cols = jnp.pad(w_in[:, c2:c3], ((0, 0), (MLA_NOPE, MLA_PAD - MLA_QK)))
            w_in_p = jnp.concatenate([w_in[:, :c2], kr_cols, w_in[:, c3:]], axis=1).astype(BF16)
            wuq = _pad_heads(mla_w_uq[j], MLA_HEADS, MLA_QK, MLA_PAD).astype(BF16)
            wukv = mla_w_ukv[j].reshape(MLA_KV_RANK, MLA_HEADS, MLA_NOPE + MLA_V)
            wuk = _pad_heads(wukv[:, :, :MLA_NOPE].reshape(MLA_KV_RANK, -1), MLA_HEADS, MLA_NOPE, MLA_PAD).astype(BF16)
            wuv = wukv[:, :, MLA_NOPE:].reshape(MLA_KV_RANK, -1).astype(BF16)
            gmq = _pad_heads(jnp.tile(mla_g_q[j], MLA_HEADS) * (MLA_QK ** -0.5 * LOG2E), MLA_HEADS, MLA_QK, MLA_PAD)
            gmk = _pad_heads(jnp.tile(mla_g_k[j], MLA_HEADS), MLA_HEADS, MLA_QK, MLA_PAD)
            gnqk = jnp.concatenate([jnp.tile(na_g_q[j], NA_HEADS) * (HEAD_DIM ** -0.5 * LOG2E),
                                    jnp.tile(na_g_k[j], NA_HEADS)])
            consts = (g1, w_in_p, row(mla_g_cq[j]), row(mla_g_ckv[j]), wuq, wuk, wuv, row(gmq), row(gmk),
                      down_o, up_o, row(gnqk))
            mq, mk, mv, nq, nk, nv = _odd_in_proj(xs, modtabs[i], consts, mla_tabs, rows)
            mla = _attention(mq, mk, mv, rows, n_heads=MLA_HEADS, group=1, dk=MLA_PAD, skip_ctx_queries=last)
            bias = _na_bias_table(na_rpb[j], n_lat // GRID_W)
            na = _neighbourhood_attention(nq, nk, nv, bias, rows, last)
            consts = (o_w_out[j].astype(BF16), g2, w1, w2)
            xs = _out_mlp(xs, mla, na, modtabs[i], consts, rows, False, last)
    return xs.reshape(batch, n_lat, d)
```

```python
import functools
import math

import numpy as np
import jax
import jax.numpy as jnp
from jax import lax
from jax.experimental import pallas as pl
from jax.experimental.pallas import tpu as pltpu

F32 = jnp.float32
BF16 = jnp.bfloat16

D_MODEL = 1024
GRID_W = 64
HEAD_DIM = 64
ROPE_BASE = 10000.0
EPS = 1e-6
N_MOD = 6
D_FF = 4 * D_MODEL
LOG2E = math.log2(math.e)

GQA_Q_HEADS = 12
GQA_KV_HEADS = 4
GQA_GROUP = GQA_Q_HEADS // GQA_KV_HEADS
GQA_Q_W = GQA_Q_HEADS * HEAD_DIM
GQA_KV_W = GQA_KV_HEADS * HEAD_DIM
SSM_WIDTH = 256
SSM_GROUP = 16
SSM_GROUPS = SSM_WIDTH // SSM_GROUP
SSM_STATE = 64
SSM_CHUNK = 8

MLA_HEADS = 8
MLA_Q_RANK = 512
MLA_KV_RANK = 256
MLA_NOPE = 64
MLA_ROPE = 32
MLA_QK = MLA_NOPE + MLA_ROPE
MLA_V = 64
MLA_PAD = 128
V_EXT = 128
NA_HEADS = 8
NA_W = NA_HEADS * HEAD_DIM
NA_WIN_R = 8
NA_WIN_C = 16
NA_ROWS_PER_BLOCK = 4

N_CTX = 256
TM = 512
EVEN_SUB_ROWS = 256
ODD_SUB_ROWS = 128
TQ = 256
KEY_CHUNK = 512
MASK_VALUE = -1e30
VMEM_LIMIT = 52 * 1024 * 1024


def _cparams(n_axes):
    return pltpu.CompilerParams(dimension_semantics=("parallel",) * n_axes, vmem_limit_bytes=VMEM_LIMIT)


def _resident(shape):
    nd = len(shape)
    return pl.BlockSpec(shape, lambda *_: (0,) * nd, pipeline_mode=pl.Buffered(1))


def _dot(a, b):
    return jnp.dot(a, b, preferred_element_type=F32)


def _dot_nt(a, b):
    return lax.dot_general(a, b, (((1,), (1,)), ((), ())), preferred_element_type=F32)


def _split_dot(x, w):
    hi = x.astype(BF16)
    lo = (x - hi.astype(F32)).astype(BF16)
    return _dot(hi, w) + _dot(lo, w)


def _modulate(xf, g, shift, scale):
    ms = jnp.mean(xf * xf, axis=-1, keepdims=True)
    return (xf * lax.rsqrt(ms + EPS)) * (g * (1.0 + scale)) + shift


def _head_inv_rms(x, down, up, head_dim):
    ss = _split_dot(x * x, down)
    return _split_dot(lax.rsqrt(ss * (1.0 / head_dim) + EPS), up)


def _head_rms(x, down, up, head_dim):
    return x * _head_inv_rms(x, down, up, head_dim)


def _rope(x, cos, sin_a, sin_b, shift):
    w = cos.shape[-1]
    blocks = []
    for j in range(x.shape[-1] // w):
        xb = x[:, j * w:(j + 1) * w]
        blocks.append(xb * cos + pltpu.roll(xb, w - shift, 1) * sin_a + pltpu.roll(xb, shift, 1) * sin_b)
    return jnp.concatenate(blocks, axis=1)


def _issue_ahead(project, finish, sub_rows):
    n = TM // sub_rows
    piece = lambda r: slice(r * sub_rows, (r + 1) * sub_rows)
    nxt = project(piece(0))
    for r in range(n):
        h, nxt = nxt, (project(piece(r + 1)) if r + 1 < n else None)
        finish(r, piece(r), h)


class _Rows:
    def __init__(self, batch, n_lat):
        assert (batch * N_CTX) % TM == 0 and n_lat % TM == 0 and (batch * N_CTX) % n_lat == 0
        self.batch, self.n_lat = batch, n_lat
        self.n_ctx_rows = batch * N_CTX
        self.n_rows = self.n_ctx_rows + batch * n_lat
        self.ctx_blocks = self.n_ctx_rows // TM
        self.lat_blocks = batch * n_lat // TM
        self.blocks_per_batch = n_lat // TM

    def mod_index(self, blk):
        return jnp.where(blk < self.ctx_blocks, 0, 1 + (blk - self.ctx_blocks) // self.blocks_per_batch)

    def rope_index(self, blk):
        return jnp.where(blk < self.ctx_blocks, 0, 1 + (blk - self.ctx_blocks) % self.blocks_per_batch)

    def query_block(self, b, i, tq):
        cs = N_CTX // tq
        return jnp.where(i < cs, b * cs + i, self.n_ctx_rows // tq + b * (self.n_lat // tq) + i - cs)

    def latent_block(self, b):
        return self.n_ctx_rows // self.n_lat + b


def _mod_kernel(c_ref, w_ref, b_ref, o_ref):
    c = c_ref[...]
    s = (c * jax.nn.sigmoid(c)).astype(BF16)
    o_ref[0] = _dot(s, w_ref[0].astype(BF16)) + b_ref[0]


def _mod_vectors(cond, w_mod, b_mod):
    depth, d, n = w_mod.shape
    rows = cond.shape[0]
    tn = 1536
    return pl.pallas_call(
        _mod_kernel,
        grid=(depth, n // tn),
        in_specs=[pl.BlockSpec((rows, d), lambda l, j: (0, 0)),
                  pl.BlockSpec((1, d, tn), lambda l, j: (l, 0, j)),
                  pl.BlockSpec((1, 1, tn), lambda l, j: (l, 0, j))],
        out_specs=pl.BlockSpec((1, rows, tn), lambda l, j: (l, 0, j)),
        out_shape=jax.ShapeDtypeStruct((depth, rows, n), F32),
        compiler_params=_cparams(2),
        name="mod_vectors",
    )(cond, w_mod, b_mod.reshape(depth, 1, n))


def _values_with_ones_t(v, n_heads):
    vt = v.T
    ones = jnp.ones((V_EXT - HEAD_DIM, vt.shape[1]), F32)
    pieces = []
    for h in range(n_heads):
        pieces += [vt[h * HEAD_DIM:(h + 1) * HEAD_DIM], ones]
    return jnp.concatenate(pieces, axis=0)


def _stream_reader(refs, ctx_blocks):
    if ctx_blocks is None:
        return (lambda rows: refs[0][rows, :]), refs[1:]
    is_ctx = pl.program_id(0) < ctx_blocks
    return (lambda rows: jnp.where(is_ctx, refs[0][rows, :], refs[1][rows, :])), refs[2:]


def _stream_specs(x, rows):
    if not isinstance(x, tuple):
        return (x,), [pl.BlockSpec((TM, D_MODEL), lambda i: (i, 0))], None
    cb = rows.ctx_blocks
    return x, [pl.BlockSpec((TM, D_MODEL), lambda i: (jnp.minimum(i, cb - 1), 0)),
               pl.BlockSpec((TM, D_MODEL), lambda i: (jnp.maximum(i - cb, 0), 0))], cb


def _even_in_kernel(*refs, ctx_blocks):
    read_x, refs = _stream_reader(refs, ctx_blocks)
    (mod_ref, g1_ref, w_ref, down_ref, up_ref, perm_ref, cq_ref, saq_ref, sbq_ref, ck_ref, sak_ref, sbk_ref,
     q_ref, k_ref, v_ref, u_ref) = refs
    m = mod_ref[0]
    nqk = GQA_Q_W + GQA_KV_W
    nv = GQA_KV_W
    nc = EVEN_SUB_ROWS // SSM_CHUNK

    def project(rows):
        a = _modulate(read_x(rows), g1_ref[...], m[:, 0:D_MODEL], m[:, D_MODEL:2 * D_MODEL]).astype(BF16)
        return _dot(a, w_ref[...])

    def finish(r, rows, h):
        qk = h[:, :nqk]
        inv = _head_inv_rms(qk, down_ref[...], up_ref[...], HEAD_DIM)
        q = _rope(qk[:, :GQA_Q_W], cq_ref[rows, :], saq_ref[rows, :], sbq_ref[rows, :], HEAD_DIM // 4)
        k = _rope(qk[:, GQA_Q_W:], ck_ref[rows, :], sak_ref[rows, :], sbk_ref[rows, :], HEAD_DIM // 4)
        q_ref[:, rows] = (q * inv[:, :GQA_Q_W]).T.astype(BF16)
        k_ref[rows, :] = (k * inv[:, GQA_Q_W:]).astype(BF16)
        v_ref[:, rows] = _values_with_ones_t(h[:, nqk:nqk + nv], GQA_KV_HEADS).astype(BF16)
        us = _dot(perm_ref[...], h[:, nqk + nv:].astype(BF16))
        u_ref[r * nc:(r + 1) * nc, :] = jnp.concatenate(
            [us[s * nc:(s + 1) * nc] for s in range(SSM_CHUNK)], axis=1).astype(BF16)

    _issue_ahead(project, finish, EVEN_SUB_ROWS)


def _mod_spec(rows):
    return pl.BlockSpec((1, 1, N_MOD * D_MODEL), lambda i: (rows.mod_index(i), 0, 0))


def _even_in_proj(x, modtab, consts, tabs, rows):
    row_spec = lambda w: pl.BlockSpec((TM, w), lambda i: (i, 0))
    tab_spec = pl.BlockSpec((TM, 128), lambda i: (rows.rope_index(i), 0))
    col_spec = lambda w: pl.BlockSpec((w, TM), lambda i: (0, i))
    v_w = GQA_KV_HEADS * V_EXT
    chunk_rows, chunk_w = TM // SSM_CHUNK, SSM_CHUNK * SSM_WIDTH
    sds = jax.ShapeDtypeStruct
    x_parts, x_specs, ctx_blocks = _stream_specs(x, rows)
    return pl.pallas_call(
        functools.partial(_even_in_kernel, ctx_blocks=ctx_blocks),
        grid=(rows.n_rows // TM,),
        in_specs=x_specs + [_mod_spec(rows)] + [_resident(c.shape) for c in consts] + [tab_spec] * len(tabs),
        out_specs=[col_spec(GQA_Q_W), row_spec(GQA_KV_W), col_spec(v_w),
                   pl.BlockSpec((chunk_rows, chunk_w), lambda i: (i, 0))],
        out_shape=[sds((GQA_Q_W, rows.n_rows), BF16), sds((rows.n_rows, GQA_KV_W), BF16),
                   sds((v_w, rows.n_rows), BF16), sds((rows.n_rows // SSM_CHUNK, chunk_w), BF16)],
        compiler_params=_cparams(1),
        name="even_in_proj",
    )(*x_parts, modtab, *consts, *tabs)


def _odd_in_kernel(x_ref, mod_ref, g1_ref, w_ref, gcq_ref, gckv_ref, wuq_ref, wuk_ref, wuv_ref,
                   gmq_ref, gmk_ref, down_ref, up_ref, gnqk_ref, cos_ref, sa_ref, sb_ref,
                   mq_ref, mk_ref, mv_ref, nq_ref, nk_ref, nv_ref):
    m = mod_ref[0]
    c1 = MLA_Q_RANK
    c2 = c1 + MLA_KV_RANK
    c3 = c2 + MLA_PAD

    def rms(t, g):
        return (t * lax.rsqrt(jnp.mean(t * t, axis=-1, keepdims=True) + EPS) * g).astype(BF16)

    def project(rows):
        a = _modulate(x_ref[rows, :], g1_ref[...], m[:, 0:D_MODEL], m[:, D_MODEL:2 * D_MODEL]).astype(BF16)
        return _dot(a, w_ref[...])

    def finish(r, rows, h):
        q = _dot(rms(h[:, :c1], gcq_ref[...]), wuq_ref[...])
        ckv = rms(h[:, c1:c2], gckv_ref[...])
        k = _dot(ckv, wuk_ref[...]) + jnp.tile(h[:, c2:c3], (1, MLA_HEADS))
        mv_ref[:, rows] = _values_with_ones_t(_dot(ckv, wuv_ref[...]), MLA_HEADS).astype(BF16)

        def mla_heads(t, g):
            parts = []
            for hh in range(MLA_HEADS):
                th = t[:, hh * MLA_PAD:(hh + 1) * MLA_PAD]
                ss = jnp.sum(th * th, axis=-1, keepdims=True)
                parts.append(th * lax.rsqrt(ss * (1.0 / MLA_QK) + EPS))
            t = jnp.concatenate(parts, axis=1) * g
            return _rope(t, cos_ref[rows, :], sa_ref[rows, :], sb_ref[rows, :], MLA_ROPE // 4)

        mq_ref[:, rows] = mla_heads(q, gmq_ref[...]).T.astype(BF16)
        mk_ref[rows, :] = mla_heads(k, gmk_ref[...]).astype(BF16)

        nqk = _head_rms(h[:, c3:c3 + 2 * NA_W], down_ref[...], up_ref[...], HEAD_DIM) * gnqk_ref[...]
        nq_ref[:, rows] = nqk[:, :NA_W].T.astype(BF16)
        nk_ref[rows, :] = nqk[:, NA_W:].astype(BF16)
        nv_ref[:, rows] = _values_with_ones_t(h[:, c3 + 2 * NA_W:], NA_HEADS).astype(BF16)

    _issue_ahead(project, finish, ODD_SUB_ROWS)


def _odd_in_proj(x, modtab, consts, tabs, rows):
    row_spec = lambda w: pl.BlockSpec((TM, w), lambda i: (i, 0))
    tab_spec = pl.BlockSpec((TM, 128), lambda i: (rows.rope_index(i), 0))
    col_spec = lambda w: pl.BlockSpec((w, TM), lambda i: (0, i))
    mla_w = MLA_HEADS * MLA_PAD
    v_w = MLA_HEADS * V_EXT
    rows_out = lambda w: jax.ShapeDtypeStruct((rows.n_rows, w), BF16)
    cols_out = lambda w: jax.ShapeDtypeStruct((w, rows.n_rows), BF16)
    return pl.pallas_call(
        _odd_in_kernel,
        grid=(rows.n_rows // TM,),
        in_specs=[row_spec(D_MODEL), _mod_spec(rows)] + [_resident(c.shape) for c in consts]
                 + [tab_spec] * len(tabs),
        out_specs=[col_spec(mla_w), row_spec(mla_w), col_spec(v_w),
                   col_spec(NA_W), row_spec(NA_W), col_spec(NA_HEADS * V_EXT)],
        out_shape=[cols_out(mla_w), rows_out(mla_w), cols_out(v_w),
                   cols_out(NA_W), rows_out(NA_W), cols_out(NA_HEADS * V_EXT)],
        compiler_params=_cparams(1),
        name="odd_in_proj",
    )(x, modtab, *consts, *tabs)


def _key_max8(s, ways=4):
    n = s.shape[0] // ways
    parts = [jnp.max(s[i * n:(i + 1) * n].reshape(n // 8, 8, s.shape[1]), axis=0) for i in range(ways)]
    while len(parts) > 1:
        parts = [jnp.maximum(a, b) for a, b in zip(parts[::2], parts[1::2])]
    return parts[0]


def _softmax_heads(n_heads, scores, values, o_ref):
    def probabilities(s_parts):
        m = jnp.max(functools.reduce(jnp.maximum, [_key_max8(s) for s in s_parts]), axis=0, keepdims=True)
        return [jnp.exp2(s - m).astype(BF16) for s in s_parts]

    def weighted_values(h, p_parts):
        acc = None
        for p, vt in zip(p_parts, values(h)):
            part = _dot(vt, p)
            acc = part if acc is None else acc + part
        return acc[:HEAD_DIM] / acc[HEAD_DIM:]

    outs = []
    s_cur, p_prev = scores(0), None
    for h in range(n_heads):
        s_next = scores(h + 1) if h + 1 < n_heads else None
        p_cur = probabilities(s_cur)
        if p_prev is not None:
            outs.append(weighted_values(h - 1, p_prev))
        s_cur, p_prev = s_next, p_cur
    outs.append(weighted_values(n_heads - 1, p_prev))
    o_ref[...] = jnp.concatenate(outs, axis=0).T.astype(o_ref.dtype)


def _attn_kernel(qt_ref, kc_ref, kl_ref, vtc_ref, vtl_ref, o_ref, *, n_heads, group, dk, first_step):
    tq = qt_ref.shape[1]
    n_lat = kl_ref.shape[0]
    k_width = kc_ref.shape[1]
    pad_q = k_width <= 256

    def scores(h, with_latent):
        kv = h // group
        qt = qt_ref[h * dk:(h + 1) * dk, :]
        if pad_q:
            pieces = []
            if kv:
                pieces.append(jnp.zeros((kv * dk, tq), qt.dtype))
            pieces.append(qt)
            if k_width - (kv + 1) * dk:
                pieces.append(jnp.zeros((k_width - (kv + 1) * dk, tq), qt.dtype))
            qt = jnp.concatenate(pieces, axis=0)
            ksl = slice(None)
        else:
            ksl = slice(kv * dk, (kv + 1) * dk)
        parts = [_dot(kc_ref[:, ksl], qt)]
        if with_latent:
            parts += [_dot(kl_ref[c * KEY_CHUNK:(c + 1) * KEY_CHUNK, ksl], qt) for c in range(n_lat // KEY_CHUNK)]
        return parts

    def values(h, with_latent):
        vsl = slice((h // group) * V_EXT, (h // group + 1) * V_EXT)
        parts = [vtc_ref[vsl, :]]
        if with_latent:
            parts += [vtl_ref[vsl, c * KEY_CHUNK:(c + 1) * KEY_CHUNK] for c in range(n_lat // KEY_CHUNK)]
        return parts

    def attend(with_latent):
        _softmax_heads(n_heads, lambda h: scores(h, with_latent), lambda h: values(h, with_latent), o_ref)

    if first_step == 0:
        is_ctx = pl.program_id(1) == 0

        @pl.when(is_ctx)
        def _():
            attend(False)

        @pl.when(jnp.logical_not(is_ctx))
        def _():
            attend(True)
    else:
        attend(True)


def _attention(qt, k, vt, rows, *, n_heads, group, dk, skip_ctx_queries):
    first = 1 if skip_ctx_queries else 0
    kern = functools.partial(_attn_kernel, n_heads=n_heads, group=group, dk=dk, first_step=first)
    wo = n_heads * HEAD_DIM
    return pl.pallas_call(
        kern,
        grid=(rows.batch, rows.n_lat // TQ + 1 - first),
        in_specs=[pl.BlockSpec((qt.shape[0], TQ), lambda b, i: (0, rows.query_block(b, i + first, TQ))),
                  pl.BlockSpec((N_CTX, k.shape[1]), lambda b, i: (b, 0)),
                  pl.BlockSpec((rows.n_lat, k.shape[1]), lambda b, i: (rows.latent_block(b), 0)),
                  pl.BlockSpec((vt.shape[0], N_CTX), lambda b, i: (0, b)),
                  pl.BlockSpec((vt.shape[0], rows.n_lat), lambda b, i: (0, rows.latent_block(b)))],
        out_specs=pl.BlockSpec((TQ, wo), lambda b, i: (rows.query_block(b, i + first, TQ), 0)),
        out_shape=jax.ShapeDtypeStruct((rows.n_rows, wo), BF16),
        compiler_params=_cparams(2),
        name="attention_dk%d" % dk,
    )(qt, k, k, vt, vt)


def _na_geometry(grid_rows):
    rb = NA_ROWS_PER_BLOCK
    wr = min(NA_WIN_R, grid_rows)
    assert grid_rows % 2 == 0 and GRID_W * 2 == 128
    span = min(grid_rows, wr + rb - 1 + (wr + rb - 1) % 2)
    starts, variants, keys = [], [], {}
    for r0 in range(0, grid_rows, rb):
        rs0 = int(np.clip(r0 - wr // 2, 0, grid_rows - wr))
        start = min(rs0, grid_rows - span)
        start -= start % 2
        rel = tuple(int(np.clip(r0 + dr - wr // 2, 0, grid_rows - wr)) - (r0 + dr) for dr in range(rb))
        key = (rel, start - r0)
        variants.append(keys.setdefault(key, len(keys)))
        starts.append(start)
    firsts = [variants.index(v) for v in range(len(keys))]
    return wr, span, starts, variants, firsts


def _na_bias_table(rpb, grid_rows):
    rb = NA_ROWS_PER_BLOCK
    wr, span, starts, variants, firsts = _na_geometry(grid_rows)
    nv = len(firsts)
    idx_r = np.zeros((nv, rb, span), np.int32)
    ok_r = np.zeros((nv, rb, span), bool)
    for v, blk in enumerate(firsts):
        r0, start = blk * rb, starts[blk]
        for dr in range(rb):
            r = r0 + dr
            rs = int(np.clip(r - wr // 2, 0, grid_rows - wr))
            for j in range(span):
                kr = start + j
                ok_r[v, dr, j] = rs <= kr < rs + wr
                idx_r[v, dr, j] = np.clip(kr - r + NA_WIN_R - 1, 0, 2 * NA_WIN_R - 2)
    c = np.arange(GRID_W)
    cs = np.clip(c - NA_WIN_C // 2, 0, GRID_W - NA_WIN_C)
    kc = np.arange(GRID_W)
    ok_c = (kc[None, :] >= cs[:, None]) & (kc[None, :] < cs[:, None] + NA_WIN_C)
    idx_c = np.clip(kc[None, :] - c[:, None] + NA_WIN_C - 1, 0, 2 * NA_WIN_C - 2)
    n_rel_c = 2 * NA_WIN_C - 1
    picked = rpb[:, idx_r.reshape(-1)].reshape(rpb.shape[0], nv, rb, span, n_rel_c)
    picked = jnp.transpose(picked, (1, 0, 3, 2, 4)).reshape(nv, rpb.shape[0], span, rb * n_rel_c) * LOG2E
    col_hit = idx_c.T[None, :, :] == np.arange(n_rel_c)[:, None, None]
    spread = (np.eye(rb, dtype=bool)[:, None, None, :, None] & col_hit[None, :, :, None, :])
    spread = jnp.asarray(spread.reshape(rb * n_rel_c, GRID_W, rb, GRID_W), F32)
    bias = jnp.einsum("vhjy,ykdc->vhjkdc", picked, spread, precision=lax.Precision.HIGHEST)
    ok = ok_r[:, None, :, None, :, None] & ok_c[None, None, None, :, None, :]
    ok = np.transpose(ok, (0, 1, 4, 5, 2, 3))
    bias = jnp.where(ok, bias, MASK_VALUE)
    return bias.reshape(nv, rpb.shape[0], span * GRID_W, rb * GRID_W)


def _na_kernel(start_ref, var_ref, qt_ref, kc_ref, kl_ref, vtc_ref, vtl_ref, bias_ref, o_ref, *, n_loc, first_step):
    i = pl.program_id(1) + first_step
    tq = qt_ref.shape[1]

    def padded_query(h):
        qt = qt_ref[h * HEAD_DIM:(h + 1) * HEAD_DIM, :]
        zeros = jnp.zeros((HEAD_DIM, tq), qt.dtype)
        return jnp.concatenate([qt, zeros] if h % 2 == 0 else [zeros, qt], axis=0)

    def pair_lanes(h):
        return slice((h // 2) * 2 * HEAD_DIM, (h // 2 + 1) * 2 * HEAD_DIM)

    def value_rows(h):
        return slice(h * V_EXT, (h + 1) * V_EXT)

    def context_queries():
        _softmax_heads(NA_HEADS, lambda h: [_dot(kc_ref[:, pair_lanes(h)], padded_query(h))],
                       lambda h: [vtc_ref[value_rows(h), :]], o_ref)

    def latent_queries():
        off = pl.multiple_of(start_ref[i - 1] * GRID_W, 2 * GRID_W)
        var = var_ref[i - 1]

        def scores(h):
            qt = padded_query(h)
            return [_dot(kc_ref[:, pair_lanes(h)], qt),
                    _dot(kl_ref[pl.ds(off, n_loc), pair_lanes(h)], qt) + bias_ref[var, h]]

        _softmax_heads(NA_HEADS, scores,
                       lambda h: [vtc_ref[value_rows(h), :], vtl_ref[value_rows(h), pl.ds(off, n_loc)]], o_ref)

    if first_step == 0:
        pl.when(i == 0)(context_queries)
        pl.when(i > 0)(latent_queries)
    else:
        latent_queries()


def _neighbourhood_attention(qt, k, vt, bias, rows, skip_ctx_queries):
    grid_rows = rows.n_lat // GRID_W
    _, span, starts, variants, _ = _na_geometry(grid_rows)
    assert NA_ROWS_PER_BLOCK * GRID_W == TQ
    first = 1 if skip_ctx_queries else 0
    kern = functools.partial(_na_kernel, n_loc=span * GRID_W, first_step=first)
    qblock = lambda b, i: rows.query_block(b, i + first, TQ)
    v_w = vt.shape[0]
    grid_spec = pltpu.PrefetchScalarGridSpec(
        num_scalar_prefetch=2,
        grid=(rows.batch, rows.n_lat // TQ + 1 - first),
        in_specs=[pl.BlockSpec((NA_W, TQ), lambda b, i, *_: (0, qblock(b, i))),
                  pl.BlockSpec((N_CTX, NA_W), lambda b, i, *_: (b, 0)),
                  pl.BlockSpec((rows.n_lat, NA_W), lambda b, i, *_: (rows.latent_block(b), 0)),
                  pl.BlockSpec((v_w, N_CTX), lambda b, i, *_: (0, b)),
                  pl.BlockSpec((v_w, rows.n_lat), lambda b, i, *_: (0, rows.latent_block(b))),
                  pl.BlockSpec(bias.shape, lambda b, i, *_: (0, 0, 0, 0), pipeline_mode=pl.Buffered(1))],
        out_specs=pl.BlockSpec((TQ, NA_W), lambda b, i, *_: (qblock(b, i), 0)),
    )
    return pl.pallas_call(
        kern,
        grid_spec=grid_spec,
        out_shape=jax.ShapeDtypeStruct((rows.n_rows, NA_W), BF16),
        compiler_params=_cparams(2),
        name="neighbourhood_attention",
    )(jnp.asarray(starts, jnp.int32), jnp.asarray(variants, jnp.int32), qt, k, k, vt, vt, bias)


def _s5_matrices(lam_re, lam_im, log_dt, b_re, b_im, c_re, c_im, d_skip):
    L, G, P, N = SSM_CHUNK, SSM_GROUPS, SSM_GROUP, SSM_STATE
    hi = lax.Precision.HIGHEST
    dt = jnp.exp(log_dt)[..., None]
    pw = jnp.arange(L + 1, dtype=F32)[:, None, None, None]
    mag = jnp.exp(lam_re * dt * pw)
    e_re = mag * jnp.cos(lam_im * dt * pw)
    e_im = mag * jnp.sin(lam_im * dt * pw)
    a_re, a_im = e_re[1], e_im[1]
    den = jnp.square(lam_re) + jnp.square(lam_im)
    f_re = ((a_re - 1.0) * lam_re + a_im * lam_im) / den
    f_im = (a_im * lam_re - (a_re - 1.0) * lam_im) / den
    bb_re = f_re[..., None] * b_re - f_im[..., None] * b_im
    bb_im = f_re[..., None] * b_im + f_im[..., None] * b_re
    ce_re = c_re[None] * e_re[:, :, :, None, :] - c_im[None] * e_im[:, :, :, None, :]
    ce_im = c_re[None] * e_im[:, :, :, None, :] + c_im[None] * e_re[:, :, :, None, :]
    kk = (jnp.einsum("kdgqn,dgnp->kdgqp", ce_re, bb_re, precision=hi)
          - jnp.einsum("kdgqn,dgnp->kdgqp", ce_im, bb_im, precision=hi))
    lag = np.arange(L)[None, :] - np.arange(L)[:, None]
    kf = kk[np.clip(lag, 0, L - 1), 0]
    kb = kk[np.clip(-lag, 0, L - 1), 1]
    skip = jnp.eye(P, dtype=F32)[None] * d_skip.reshape(G, P)[:, None, :]
    m = lambda cond: jnp.asarray(cond, F32)[:, :, None, None, None]
    kst = m(lag >= 0) * kf + m(lag <= 0) * kb + m(lag == 0) * skip[None, None]
    a_t = jnp.transpose(kst, (0, 2, 4, 1, 3)).reshape(L * G * P, L * P)

    def state_in(pows, d):
        x_re = e_re[pows, d][..., None] * bb_re[d][None] - e_im[pows, d][..., None] * bb_im[d][None]
        x_im = e_re[pows, d][..., None] * bb_im[d][None] + e_im[pows, d][..., None] * bb_re[d][None]
        flat = lambda x: jnp.pad(jnp.transpose(x, (0, 1, 3, 2)).reshape(L * G * P, N), ((0, 0), (0, L * P - N)))
        return [flat(x_re), flat(x_im)]

    a_w1 = jnp.concatenate([a_t] + state_in(np.arange(L)[::-1].copy(), 0) + state_in(np.arange(L), 1), axis=1)
    w1 = _expand_group_blocks(a_w1[None], L * G * P + 4 * G * N, row_shift=4)

    def state_out(x):
        return jnp.transpose(x, (1, 3, 0, 2)).reshape(G * N, L * P)

    pf, pb = np.arange(1, L + 1), np.arange(L, 0, -1)
    a_wc = jnp.stack([jnp.concatenate([state_out(ce_re[pf, 0]), state_out(ce_re[pb, 1])], axis=0),
                      jnp.concatenate([state_out(-ce_im[pf, 0]), state_out(-ce_im[pb, 1])], axis=0)])
    wc = _expand_group_blocks(a_wc, L * G * P, row_shift=6)
    al_re = e_re[L].reshape(2, 1, G * N)
    al_im = e_im[L].reshape(2, 1, G * N)
    return w1[0], wc[0], wc[1], al_re, al_im


def _expand_kernel(a_ref, o_ref, *, row_shift, n_response_tiles):
    j = pl.program_id(1)
    a = a_ref[0].astype(BF16)
    n_rows, tn = o_ref.shape[1], o_ref.shape[2]
    src = lax.broadcasted_iota(jnp.int32, (a.shape[1], tn), 0)
    col = lax.broadcasted_iota(jnp.int32, (a.shape[1], tn), 1)
    row_group = (lax.broadcasted_iota(jnp.int32, (n_rows, tn), 0) >> row_shift) & (SSM_GROUPS - 1)
    out_col = lax.broadcasted_iota(jnp.int32, (n_rows, tn), 1)

    def emit(spread, col_group):
        val = _dot(a, jnp.where(spread, 1.0, 0.0).astype(BF16))
        o_ref[0] = jnp.where(row_group == col_group, val, 0.0).astype(o_ref.dtype)

    @pl.when(j < n_response_tiles)
    def _():
        cg = col + j * tn
        spread = ((src >> 4) == (cg >> 8)) & ((src & 15) == (cg & 15))
        emit(spread, ((out_col + j * tn) >> 4) & (SSM_GROUPS - 1))

    @pl.when(j >= n_response_tiles)
    def _():
        spread = src == (col & (SSM_STATE - 1))
        emit(spread, (out_col >> 6) & (SSM_GROUPS - 1))


def _expand_group_blocks(a, n_cols, *, row_shift):
    k, n_rows, _ = a.shape
    tn = 1024
    n_response_tiles = SSM_CHUNK * SSM_WIDTH // tn
    assert SSM_GROUP == 16 and SSM_STATE == 64 and SSM_CHUNK * SSM_GROUP == 128 and tn % (SSM_GROUPS * SSM_STATE) == 0
    kern = functools.partial(_expand_kernel, row_shift=row_shift, n_response_tiles=n_response_tiles)
    return pl.pallas_call(
        kern,
        grid=(k, n_cols // tn),
        in_specs=[pl.BlockSpec((1, n_rows, 128), lambda d, j: (d, 0, jnp.maximum(j - n_response_tiles + 1, 0)))],
        out_specs=pl.BlockSpec((1, n_rows, tn), lambda d, j: (d, 0, j)),
        out_shape=jax.ShapeDtypeStruct((k, n_rows, n_cols), BF16),
        compiler_params=_cparams(2),
        name="s5_expand",
    )(a)


def _mm_kernel(a_ref, b_ref, o_ref):
    o_ref[...] = _dot(a_ref[...], b_ref[...]).astype(o_ref.dtype)


def _matmul(a, b, tm, tn, out_dtype):
    m, kdim = a.shape
    n = b.shape[1]
    return pl.pallas_call(
        _mm_kernel,
        grid=(n // tn, m // tm),
        in_specs=[pl.BlockSpec((tm, kdim), lambda j, i: (i, 0)),
                  pl.BlockSpec((kdim, tn), lambda j, i: (0, j))],
        out_specs=pl.BlockSpec((tm, tn), lambda j, i: (i, j)),
        out_shape=jax.ShapeDtypeStruct((m, n), out_dtype),
        compiler_params=_cparams(2),
        name="s5_chunk_matmul",
    )(a, b)


def _s5_carry_kernel(sre_ref, sim_ref, are_ref, aim_ref, hre_ref, him_ref, *, ctx_chunks):
    n_chunks = sre_ref.shape[0]
    backward = pl.program_id(0) == 1
    a_re = jnp.broadcast_to(are_ref[0], sre_ref.shape[1:])
    a_im = jnp.broadcast_to(aim_ref[0], sre_ref.shape[1:])

    def step(j, carry):
        h_re, h_im = carry
        cb = jnp.where(j < ctx_chunks, ctx_chunks - 1 - j, n_chunks + ctx_chunks - 1 - j)
        c = jnp.where(backward, cb, j)
        hre_ref[c] = h_re.astype(hre_ref.dtype)
        him_ref[c] = h_im.astype(him_ref.dtype)
        n_re = a_re * h_re - a_im * h_im + sre_ref[c]
        n_im = a_re * h_im + a_im * h_re + sim_ref[c]
        return n_re, n_im

    zero = jnp.zeros(sre_ref.shape[1:], F32)
    lax.fori_loop(0, n_chunks, step, (zero, zero))


def _s5_carry(ys, al_re, al_im, n_chunks, batch, ctx_chunks):
    gn = SSM_GROUPS * SSM_STATE
    tw = 256
    base = SSM_CHUNK * SSM_WIDTH // tw
    per_dir = 2 * gn // tw
    s3 = ys.reshape(n_chunks, batch, ys.shape[1])
    kern = functools.partial(_s5_carry_kernel, ctx_chunks=ctx_chunks)
    blk = (n_chunks, batch, tw)
    return pl.pallas_call(
        kern,
        grid=(2, gn // tw),
        in_specs=[pl.BlockSpec(blk, lambda d, j: (0, 0, base + d * per_dir + j)),
                  pl.BlockSpec(blk, lambda d, j: (0, 0, base + d * per_dir + gn // tw + j)),
                  pl.BlockSpec((1, 1, tw), lambda d, j: (d, 0, j)),
                  pl.BlockSpec((1, 1, tw), lambda d, j: (d, 0, j))],
        out_specs=[pl.BlockSpec(blk, lambda d, j: (0, 0, d * (gn // tw) + j))] * 2,
        out_shape=[jax.ShapeDtypeStruct((n_chunks, batch, 2 * gn), BF16)] * 2,
        compiler_params=_cparams(2),
        name="s5_carry",
    )(s3, s3, al_re, al_im)


def _s5_readout_kernel(hre_ref, him_ref, wre_ref, wim_ref, y_ref, o_ref):
    o_ref[...] = y_ref[...] + _dot(hre_ref[...], wre_ref[...]) + _dot(him_ref[...], wim_ref[...])


def _s5_readout(h_re, h_im, wc_re, wc_im, ys, tm):
    m, kdim = h_re.shape
    n = wc_re.shape[1]
    tn = 1024
    return pl.pallas_call(
        _s5_readout_kernel,
        grid=(n // tn, m // tm),
        in_specs=[pl.BlockSpec((tm, kdim), lambda j, i: (i, 0)),
                  pl.BlockSpec((tm, kdim), lambda j, i: (i, 0)),
                  pl.BlockSpec((kdim, tn), lambda j, i: (0, j)),
                  pl.BlockSpec((kdim, tn), lambda j, i: (0, j)),
                  pl.BlockSpec((tm, tn), lambda j, i: (i, j))],
        out_specs=pl.BlockSpec((tm, tn), lambda j, i: (i, j)),
        out_shape=jax.ShapeDtypeStruct((m, n), F32),
        compiler_params=_cparams(2),
        name="s5_readout",
    )(h_re, h_im, wc_re, wc_im, ys)


def _row_tile(m, target=512):
    t = min(m, target)
    while m % t or t % 16:
        t -= 16
    return t


def _s5_mixer(u, mats, rows):
    w1, wc_re, wc_im, al_re, al_im = mats
    batch = rows.batch
    cw = SSM_CHUNK * SSM_WIDTH
    ctx_chunks = N_CTX // SSM_CHUNK
    n_chunks = ctx_chunks + rows.n_lat // SSM_CHUNK
    n_ctx_chunk_rows = rows.n_ctx_rows // SSM_CHUNK
    uc = jnp.concatenate([u[:n_ctx_chunk_rows].reshape(batch, ctx_chunks, cw),
                          u[n_ctx_chunk_rows:].reshape(batch, n_chunks - ctx_chunks, cw)], axis=1)
    uc = jnp.transpose(uc, (1, 0, 2)).reshape(n_chunks * batch, cw)
    tm = _row_tile(n_chunks * batch)
    ys = _matmul(uc, w1, tm, 1024, F32)
    h_re, h_im = _s5_carry(ys, al_re, al_im, n_chunks, batch, ctx_chunks)
    gn2 = 2 * SSM_GROUPS * SSM_STATE
    y = _s5_readout(h_re.reshape(-1, gn2), h_im.reshape(-1, gn2), wc_re, wc_im, ys, tm)
    y = jnp.transpose(y.reshape(n_chunks, batch, cw), (1, 0, 2))
    return jnp.concatenate([y[:, :ctx_chunks].reshape(n_ctx_chunk_rows, cw),
                            y[:, ctx_chunks:].reshape(-1, cw)], axis=0)


def _gelu_tanh(y):
    return 0.5 * y * (1.0 + jnp.tanh(math.sqrt(2.0 / math.pi) * (y + 0.044715 * (y * y * y))))


def _out_mlp_kernel(*refs, even, ctx_blocks):
    read_x, refs = _stream_reader(refs, ctx_blocks)
    if even:
        m1_ref, m2_ref, mod_ref, perm_ref, wglu_ref, bglu_ref, wo_ref, g2_ref, w1_ref, w2_ref, o_ref = refs
        yc = m2_ref[...]
        ys = jnp.concatenate([yc[:, s * SSM_WIDTH:(s + 1) * SSM_WIDTH] for s in range(SSM_CHUNK)], axis=0)
        hi = ys.astype(BF16)
        lo = (ys - hi.astype(F32)).astype(BF16)
        y = _gelu_tanh(_dot(perm_ref[...], hi) + _dot(perm_ref[...], lo))
        z = _dot(y.astype(BF16), wglu_ref[...]) + bglu_ref[...]
        second = (y * jax.nn.sigmoid(z)).astype(BF16)
    else:
        m1_ref, m2_ref, mod_ref, wo_ref, g2_ref, w1_ref, w2_ref, o_ref = refs
        second = m2_ref[...]
    m = mod_ref[0]
    mod = lambda j: m[:, j * D_MODEL:(j + 1) * D_MODEL]
    mix = jnp.concatenate([m1_ref[...], second], axis=1)
    x1 = read_x(slice(None)) + mod(2) * _dot(mix, wo_ref[...])
    a = _modulate(x1, g2_ref[...], mod(3), mod(4)).astype(BF16)
    acc = None
    ck = 1024
    for c in range(D_FF // ck):
        h = jnp.maximum(_dot(a, w1_ref[:, c * ck:(c + 1) * ck]), 0.0)
        part = _dot((h * h).astype(BF16), w2_ref[c * ck:(c + 1) * ck, :])
        acc = part if acc is None else acc + part
    o_ref[...] = x1 + mod(5) * acc


def _out_mlp(x, mix1, mix2, modtab, consts, rows, even, latent_only):
    first = rows.ctx_blocks if latent_only else 0
    row_spec = lambda w: pl.BlockSpec((TM, w), lambda i: (i + first, 0))
    n_blocks = rows.n_rows // TM - first
    mix2_rows = TM // SSM_CHUNK if even else TM
    if isinstance(x, tuple):
        assert not latent_only
        x_parts, x_specs, ctx_blocks = _stream_specs(x, rows)
    else:
        x_parts, x_specs, ctx_blocks = (x,), [row_spec(D_MODEL)], None
    return pl.pallas_call(
        functools.partial(_out_mlp_kernel, even=even, ctx_blocks=ctx_blocks),
        grid=(n_blocks,),
        in_specs=x_specs + [row_spec(mix1.shape[1]),
                  pl.BlockSpec((mix2_rows, mix2.shape[1]), lambda i: (i + first, 0)),
                  pl.BlockSpec((1, 1, N_MOD * D_MODEL), lambda i: (rows.mod_index(i + first), 0, 0))]
                 + [_resident(c.shape) for c in consts],
        out_specs=pl.BlockSpec((TM, D_MODEL), lambda i: (i, 0)),
        out_shape=jax.ShapeDtypeStruct((n_blocks * TM, D_MODEL), F32),
        compiler_params=_cparams(1),
        name="out_mlp_even" if even else "out_mlp_odd",
    )(*x_parts, mix1, mix2, modtab, *consts)


def _rope_tables(n_lat, rot_dim, head_w, lane_off):
    t = jnp.arange(n_lat)
    grid_r = (t // GRID_W).astype(F32)
    grid_c = (t % GRID_W).astype(F32)
    axis_dim = rot_dim // 2
    freqs = ROPE_BASE ** (-jnp.arange(0, axis_dim, 2, dtype=F32) / axis_dim)
    ang_r = grid_r[:, None] * freqs
    ang_c = grid_c[:, None] * freqs
    ang = jnp.concatenate([ang_r, ang_r, ang_c, ang_c], axis=-1)
    cos, sin = jnp.cos(ang), jnp.sin(ang)
    quarter = rot_dim // 4
    first = (np.arange(rot_dim) // quarter) % 2 == 0
    sin_a = jnp.where(first, -sin, 0.0)
    sin_b = jnp.where(first, 0.0, sin)

    def widen(tab, fill):
        full = jnp.full((n_lat, head_w), fill, F32).at[:, lane_off:lane_off + rot_dim].set(tab)
        full = jnp.concatenate([jnp.full((TM, head_w), fill, F32), full], axis=0)
        return jnp.tile(full, (1, 128 // head_w))

    return widen(cos, 1.0), widen(sin_a, 0.0), widen(sin_b, 0.0)


def _with_gain(tabs, gain, shift):
    cos, sin_a, sin_b = tabs
    return cos * gain, sin_a * jnp.roll(gain, -shift), sin_b * jnp.roll(gain, shift)


def _head_sum_matrices(n_heads, head_dim):
    down = np.kron(np.eye(n_heads), np.ones((head_dim, 1)))
    pad = (-n_heads) % 128
    down = np.pad(down, ((0, 0), (0, pad)))
    return jnp.asarray(down, BF16), jnp.asarray(down.T, BF16)


def _pad_heads(w, n_heads, width, pad_to):
    lead = w.shape[:-1]
    w = w.reshape(lead + (n_heads, width))
    w = jnp.pad(w, [(0, 0)] * len(lead) + [(0, 0), (0, pad_to - width)])
    return w.reshape(lead + (n_heads * pad_to,))


def kernel(x, c, ctx, c_ctx, w_mod, b_mod, g_norm1, g_norm2, w_ff1, w_ff2, e_w_in, e_w_out, e_g_q, e_g_k, ssm_lam_re, ssm_lam_im, ssm_log_dt, ssm_b_re, ssm_b_im, ssm_c_re, ssm_c_im, ssm_d, ssm_w_glu, ssm_b_glu, o_w_in, o_w_out, mla_g_cq, mla_g_ckv, mla_w_uq, mla_w_ukv, mla_g_q, mla_g_k, na_g_q, na_g_k, na_rpb):
    batch, n_lat, d = x.shape
    depth = w_mod.shape[0]
    assert d == D_MODEL and ctx.shape[1] == N_CTX and n_lat % GRID_W == 0
    rows = _Rows(batch, n_lat)

    pad_rows = (-(batch + 1)) % 8
    cond = jnp.concatenate([c_ctx[None], c, jnp.zeros((pad_rows, d), F32)], axis=0)
    modtabs = _mod_vectors(cond, w_mod, b_mod).reshape(depth, -1, 1, N_MOD * d)

    xs = (ctx.reshape(rows.n_ctx_rows, d), x.reshape(batch * n_lat, d))

    even_tabs = _rope_tables(n_lat, HEAD_DIM, HEAD_DIM, 0)
    mla_tabs = _rope_tables(n_lat, MLA_ROPE, MLA_PAD, MLA_NOPE)
    down_e, up_e = _head_sum_matrices(GQA_Q_HEADS + GQA_KV_HEADS, HEAD_DIM)
    down_o, up_o = _head_sum_matrices(2 * NA_HEADS, HEAD_DIM)
    row = lambda v: v.reshape(1, -1)

    def chunk_perm(n):
        tok = np.arange(n)
        mat = np.zeros((n, n), np.float32)
        mat[(tok % SSM_CHUNK) * (n // SSM_CHUNK) + tok // SSM_CHUNK, tok] = 1.0
        return mat

    perm = jnp.asarray(chunk_perm(EVEN_SUB_ROWS), BF16)
    perm_t = jnp.asarray(chunk_perm(TM).T, BF16)

    for i in range(depth):
        j = i // 2
        last = i == depth - 1
        g1, g2 = row(g_norm1[i]), row(g_norm2[i])
        w1, w2 = w_ff1[i].astype(BF16), w_ff2[i].astype(BF16)
        if i % 2 == 0:
            lanes = 128 // HEAD_DIM
            tabs = (_with_gain(even_tabs, jnp.tile(e_g_q[j], lanes) * (HEAD_DIM ** -0.5 * LOG2E), HEAD_DIM // 4)
                    + _with_gain(even_tabs, jnp.tile(e_g_k[j], lanes), HEAD_DIM // 4))
            q, k, v, u = _even_in_proj(xs, modtabs[i], (g1, e_w_in[j].astype(BF16), down_e, up_e, perm), tabs, rows)
            att = _attention(q, k, v, rows, n_heads=GQA_Q_HEADS, group=GQA_GROUP, dk=HEAD_DIM, skip_ctx_queries=last)
            mats = _s5_matrices(ssm_lam_re[j], ssm_lam_im[j], ssm_log_dt[j], ssm_b_re[j], ssm_b_im[j],
                                ssm_c_re[j], ssm_c_im[j], ssm_d[j])
            y = _s5_mixer(u, mats, rows)
            consts = (perm_t, ssm_w_glu[j].astype(BF16), row(ssm_b_glu[j]), e_w_out[j].astype(BF16), g2, w1, w2)
            xs = _out_mlp(xs, att, y, modtabs[i], consts, rows, True, last)
        else:
            w_in = o_w_in[j]
            c1 = MLA_Q_RANK
            c2 = c1 + MLA_KV_RANK
            c3 = c2 + MLA_ROPE
            kr_cols = jnp.pad(w_in[:, c2:c3], ((0, 0), (MLA_NOPE, MLA_PAD - MLA_QK)))
            w_in_p = jnp.concatenate([w_in[:, :c2], kr_cols, w_in[:, c3:]], axis=1).astype(BF16)
            wuq = _pad_heads(mla_w_uq[j], MLA_HEADS, MLA_QK, MLA_PAD).astype(BF16)
            wukv = mla_w_ukv[j].reshape(MLA_KV_RANK, MLA_HEADS, MLA_NOPE + MLA_V)
            wuk = _pad_heads(wukv[:, :, :MLA_NOPE].reshape(MLA_KV_RANK, -1), MLA_HEADS, MLA_NOPE, MLA_PAD).astype(BF16)
            wuv = wukv[:, :, MLA_NOPE:].reshape(MLA_KV_RANK, -1).astype(BF16)
            gmq = _pad_heads(jnp.tile(mla_g_q[j], MLA_HEADS) * (MLA_QK ** -0.5 * LOG2E), MLA_HEADS, MLA_QK, MLA_PAD)
            gmk = _pad_heads(jnp.tile(mla_g_k[j], MLA_HEADS), MLA_HEADS, MLA_QK, MLA_PAD)
            gnqk = jnp.concatenate([jnp.tile(na_g_q[j], NA_HEADS) * (HEAD_DIM ** -0.5 * LOG2E),
                                    jnp.tile(na_g_k[j], NA_HEADS)])
            consts = (g1, w_in_p, row(mla_g_cq[j]), row(mla_g_ckv[j]), wuq, wuk, wuv, row(gmq), row(gmk),
                      down_o, up_o, row(gnqk))
            mq, mk, mv, nq, nk, nv = _odd_in_proj(xs, modtabs[i], consts, mla_tabs, rows)
            mla = _attention(mq, mk, mv, rows, n_heads=MLA_HEADS, group=1, dk=MLA_PAD, skip_ctx_queries=last)
            bias = _na_bias_table(na_rpb[j], n_lat // GRID_W)
            na = _neighbourhood_attention(nq, nk, nv, bias, rows, last)
            consts = (o_w_out[j].astype(BF16), g2, w1, w2)
            xs = _out_mlp(xs, mla, na, modtabs[i], consts, rows, False, last)
    return xs.reshape(batch, n_lat, d)
```

```python
import functools
import math

import numpy as np
import jax
import jax.numpy as jnp
from jax import lax
from jax.experimental import pallas as pl
from jax.experimental.pallas import tpu as pltpu

F32 = jnp.float32
BF16 = jnp.bfloat16

D_MODEL = 1024
GRID_W = 64
HEAD_DIM = 64
ROPE_BASE = 10000.0
EPS = 1e-6
N_MOD = 6
D_FF = 4 * D_MODEL
LOG2E = math.log2(math.e)

GQA_Q_HEADS = 12
GQA_KV_HEADS = 4
GQA_GROUP = GQA_Q_HEADS // GQA_KV_HEADS
GQA_Q_W = GQA_Q_HEADS * HEAD_DIM
GQA_KV_W = GQA_KV_HEADS * HEAD_DIM
SSM_WIDTH = 256
SSM_GROUP = 16
SSM_GROUPS = SSM_WIDTH // SSM_GROUP
SSM_STATE = 64
SSM_CHUNK = 8

MLA_HEADS = 8
MLA_Q_RANK = 512
MLA_KV_RANK = 256
MLA_NOPE = 64
MLA_ROPE = 32
MLA_QK = MLA_NOPE + MLA_ROPE
MLA_V = 64
MLA_PAD = 128
V_EXT = 128
NA_HEADS = 8
NA_W = NA_HEADS * HEAD_DIM
NA_WIN_R = 8
NA_WIN_C = 16
NA_ROWS_PER_BLOCK = 4

N_CTX = 256
TM = 512
EVEN_SUB_ROWS = 256
ODD_SUB_ROWS = 128
TQ = 256
KEY_CHUNK = 1024
MASK_VALUE = -1e30
VMEM_LIMIT = 52 * 1024 * 1024


def _cparams(n_axes):
    return pltpu.CompilerParams(dimension_semantics=("parallel",) * n_axes, vmem_limit_bytes=VMEM_LIMIT)


def _resident(shape):
    nd = len(shape)
    return pl.BlockSpec(shape, lambda *_: (0,) * nd, pipeline_mode=pl.Buffered(1))


def _dot(a, b):
    return jnp.dot(a, b, preferred_element_type=F32)


def _dot_nt(a, b):
    return lax.dot_general(a, b, (((1,), (1,)), ((), ())), preferred_element_type=F32)


def _split_dot(x, w):
    hi = x.astype(BF16)
    lo = (x - hi.astype(F32)).astype(BF16)
    return _dot(hi, w) + _dot(lo, w)


def _modulate(xf, g, shift, scale):
    ms = jnp.mean(xf * xf, axis=-1, keepdims=True)
    return (xf * lax.rsqrt(ms + EPS)) * (g * (1.0 + scale)) + shift


def _head_inv_rms(x, down, up, head_dim):
    ss = _split_dot(x * x, down)
    return _split_dot(lax.rsqrt(ss * (1.0 / head_dim) + EPS), up)


def _head_rms(x, down, up, head_dim):
    return x * _head_inv_rms(x, down, up, head_dim)


def _rope(x, cos, sin_a, sin_b, shift):
    w = cos.shape[-1]
    blocks = []
    for j in range(x.shape[-1] // w):
        xb = x[:, j * w:(j + 1) * w]
        blocks.append(xb * cos + pltpu.roll(xb, w - shift, 1) * sin_a + pltpu.roll(xb, shift, 1) * sin_b)
    return jnp.concatenate(blocks, axis=1)


def _issue_ahead(project, finish, sub_rows):
    n = TM // sub_rows
    piece = lambda r: slice(r * sub_rows, (r + 1) * sub_rows)
    nxt = project(piece(0))
    for r in range(n):
        h, nxt = nxt, (project(piece(r + 1)) if r + 1 < n else None)
        finish(r, piece(r), h)


class _Rows:
    def __init__(self, batch, n_lat):
        assert (batch * N_CTX) % TM == 0 and n_lat % TM == 0 and (batch * N_CTX) % n_lat == 0
        self.batch, self.n_lat = batch, n_lat
        self.n_ctx_rows = batch * N_CTX
        self.n_rows = self.n_ctx_rows + batch * n_lat
        self.ctx_blocks = self.n_ctx_rows // TM
        self.lat_blocks = batch * n_lat // TM
        self.blocks_per_batch = n_lat // TM

    def mod_index(self, blk):
        return jnp.where(blk < self.ctx_blocks, 0, 1 + (blk - self.ctx_blocks) // self.blocks_per_batch)

    def rope_index(self, blk):
        return jnp.where(blk < self.ctx_blocks, 0, 1 + (blk - self.ctx_blocks) % self.blocks_per_batch)

    def query_block(self, b, i, tq):
        cs = N_CTX // tq
        return jnp.where(i < cs, b * cs + i, self.n_ctx_rows // tq + b * (self.n_lat // tq) + i - cs)

    def latent_block(self, b):
        return self.n_ctx_rows // self.n_lat + b


def _mod_kernel(c_ref, w_ref, b_ref, o_ref):
    c = c_ref[...]
    s = (c * jax.nn.sigmoid(c)).astype(BF16)
    o_ref[0] = _dot(s, w_ref[0].astype(BF16)) + b_ref[0]


def _mod_vectors(cond, w_mod, b_mod):
    depth, d, n = w_mod.shape
    rows = cond.shape[0]
    tn = 1536
    return pl.pallas_call(
        _mod_kernel,
        grid=(depth, n // tn),
        in_specs=[pl.BlockSpec((rows, d), lambda l, j: (0, 0)),
                  pl.BlockSpec((1, d, tn), lambda l, j: (l, 0, j)),
                  pl.BlockSpec((1, 1, tn), lambda l, j: (l, 0, j))],
        out_specs=pl.BlockSpec((1, rows, tn), lambda l, j: (l, 0, j)),
        out_shape=jax.ShapeDtypeStruct((depth, rows, n), F32),
        compiler_params=_cparams(2),
        name="mod_vectors",
    )(cond, w_mod, b_mod.reshape(depth, 1, n))


def _values_with_ones_t(v, n_heads):
    vt = v.T
    ones = jnp.ones((V_EXT - HEAD_DIM, vt.shape[1]), F32)
    pieces = []
    for h in range(n_heads):
        pieces += [vt[h * HEAD_DIM:(h + 1) * HEAD_DIM], ones]
    return jnp.concatenate(pieces, axis=0)


def _stream_reader(refs, ctx_blocks):
    if ctx_blocks is None:
        return (lambda rows: refs[0][rows, :]), refs[1:]
    is_ctx = pl.program_id(0) < ctx_blocks
    return (lambda rows: jnp.where(is_ctx, refs[0][rows, :], refs[1][rows, :])), refs[2:]


def _stream_specs(x, rows):
    if not isinstance(x, tuple):
        return (x,), [pl.BlockSpec((TM, D_MODEL), lambda i: (i, 0))], None
    cb = rows.ctx_blocks
    return x, [pl.BlockSpec((TM, D_MODEL), lambda i: (jnp.minimum(i, cb - 1), 0)),
               pl.BlockSpec((TM, D_MODEL), lambda i: (jnp.maximum(i - cb, 0), 0))], cb


def _even_in_kernel(*refs, ctx_blocks):
    read_x, refs = _stream_reader(refs, ctx_blocks)
    (mod_ref, g1_ref, w_ref, down_ref, up_ref, perm_ref, cq_ref, saq_ref, sbq_ref, ck_ref, sak_ref, sbk_ref,
     q_ref, k_ref, v_ref, u_ref) = refs
    m = mod_ref[0]
    nqk = GQA_Q_W + GQA_KV_W
    nv = GQA_KV_W
    nc = EVEN_SUB_ROWS // SSM_CHUNK

    def project(rows):
        a = _modulate(read_x(rows), g1_ref[...], m[:, 0:D_MODEL], m[:, D_MODEL:2 * D_MODEL]).astype(BF16)
        return _dot(a, w_ref[...])

    def finish(r, rows, h):
        qk = h[:, :nqk]
        inv = _head_inv_rms(qk, down_ref[...], up_ref[...], HEAD_DIM)
        q = _rope(qk[:, :GQA_Q_W], cq_ref[rows, :], saq_ref[rows, :], sbq_ref[rows, :], HEAD_DIM // 4)
        k = _rope(qk[:, GQA_Q_W:], ck_ref[rows, :], sak_ref[rows, :], sbk_ref[rows, :], HEAD_DIM // 4)
        q_ref[:, rows] = (q * inv[:, :GQA_Q_W]).T.astype(BF16)
        k_ref[rows, :] = (k * inv[:, GQA_Q_W:]).astype(BF16)
        v_ref[:, rows] = _values_with_ones_t(h[:, nqk:nqk + nv], GQA_KV_HEADS).astype(BF16)
        us = _dot(perm_ref[...], h[:, nqk + nv:].astype(BF16))
        u_ref[r * nc:(r + 1) * nc, :] = jnp.concatenate(
            [us[s * nc:(s + 1) * nc] for s in range(SSM_CHUNK)], axis=1).astype(BF16)

    _issue_ahead(project, finish, EVEN_SUB_ROWS)


def _mod_spec(rows):
    return pl.BlockSpec((1, 1, N_MOD * D_MODEL), lambda i: (rows.mod_index(i), 0, 0))


def _even_in_proj(x, modtab, consts, tabs, rows):
    row_spec = lambda w: pl.BlockSpec((TM, w), lambda i: (i, 0))
    tab_spec = pl.BlockSpec((TM, 128), lambda i: (rows.rope_index(i), 0))
    col_spec = lambda w: pl.BlockSpec((w, TM), lambda i: (0, i))
    v_w = GQA_KV_HEADS * V_EXT
    chunk_rows, chunk_w = TM // SSM_CHUNK, SSM_CHUNK * SSM_WIDTH
    sds = jax.ShapeDtypeStruct
    x_parts, x_specs, ctx_blocks = _stream_specs(x, rows)
    return pl.pallas_call(
        functools.partial(_even_in_kernel, ctx_blocks=ctx_blocks),
        grid=(rows.n_rows // TM,),
        in_specs=x_specs + [_mod_spec(rows)] + [_resident(c.shape) for c in consts] + [tab_spec] * len(tabs),
        out_specs=[col_spec(GQA_Q_W), row_spec(GQA_KV_W), col_spec(v_w),
                   pl.BlockSpec((chunk_rows, chunk_w), lambda i: (i, 0))],
        out_shape=[sds((GQA_Q_W, rows.n_rows), BF16), sds((rows.n_rows, GQA_KV_W), BF16),
                   sds((v_w, rows.n_rows), BF16), sds((rows.n_rows // SSM_CHUNK, chunk_w), BF16)],
        compiler_params=_cparams(1),
        name="even_in_proj",
    )(*x_parts, modtab, *consts, *tabs)


def _odd_in_kernel(x_ref, mod_ref, g1_ref, w_ref, gcq_ref, gckv_ref, wuq_ref, wuk_ref, wuv_ref,
                   gmq_ref, gmk_ref, down_ref, up_ref, gnqk_ref, cos_ref, sa_ref, sb_ref,
                   mq_ref, mk_ref, mv_ref, nq_ref, nk_ref, nv_ref):
    m = mod_ref[0]
    c1 = MLA_Q_RANK
    c2 = c1 + MLA_KV_RANK
    c3 = c2 + MLA_PAD

    def rms(t, g):
        return (t * lax.rsqrt(jnp.mean(t * t, axis=-1, keepdims=True) + EPS) * g).astype(BF16)

    def project(rows):
        a = _modulate(x_ref[rows, :], g1_ref[...], m[:, 0:D_MODEL], m[:, D_MODEL:2 * D_MODEL]).astype(BF16)
        return _dot(a, w_ref[...])

    def finish(r, rows, h):
        q = _dot(rms(h[:, :c1], gcq_ref[...]), wuq_ref[...])
        ckv = rms(h[:, c1:c2], gckv_ref[...])
        k = _dot(ckv, wuk_ref[...]) + jnp.tile(h[:, c2:c3], (1, MLA_HEADS))
        mv_ref[:, rows] = _values_with_ones_t(_dot(ckv, wuv_ref[...]), MLA_HEADS).astype(BF16)

        def mla_heads(t, g):
            parts = []
            for hh in range(MLA_HEADS):
                th = t[:, hh * MLA_PAD:(hh + 1) * MLA_PAD]
                ss = jnp.sum(th * th, axis=-1, keepdims=True)
                parts.append(th * lax.rsqrt(ss * (1.0 / MLA_QK) + EPS))
            t = jnp.concatenate(parts, axis=1) * g
            return _rope(t, cos_ref[rows, :], sa_ref[rows, :], sb_ref[rows, :], MLA_ROPE // 4)

        mq_ref[:, rows] = mla_heads(q, gmq_ref[...]).T.astype(BF16)
        mk_ref[rows, :] = mla_heads(k, gmk_ref[...]).astype(BF16)

        nqk = _head_rms(h[:, c3:c3 + 2 * NA_W], down_ref[...], up_ref[...], HEAD_DIM) * gnqk_ref[...]
        nq_ref[:, rows] = nqk[:, :NA_W].T.astype(BF16)
        nk_ref[rows, :] = nqk[:, NA_W:].astype(BF16)
        nv_ref[:, rows] = _values_with_ones_t(h[:, c3 + 2 * NA_W:], NA_HEADS).astype(BF16)

    _issue_ahead(project, finish, ODD_SUB_ROWS)


def _odd_in_proj(x, modtab, consts, tabs, rows):
    row_spec = lambda w: pl.BlockSpec((TM, w), lambda i: (i, 0))
    tab_spec = pl.BlockSpec((TM, 128), lambda i: (rows.rope_index(i), 0))
    col_spec = lambda w: pl.BlockSpec((w, TM), lambda i: (0, i))
    mla_w = MLA_HEADS * MLA_PAD
    v_w = MLA_HEADS * V_EXT
    rows_out = lambda w: jax.ShapeDtypeStruct((rows.n_rows, w), BF16)
    cols_out = lambda w: jax.ShapeDtypeStruct((w, rows.n_rows), BF16)
    return pl.pallas_call(
        _odd_in_kernel,
        grid=(rows.n_rows // TM,),
        in_specs=[row_spec(D_MODEL), _mod_spec(rows)] + [_resident(c.shape) for c in consts]
                 + [tab_spec] * len(tabs),
        out_specs=[col_spec(mla_w), row_spec(mla_w), col_spec(v_w),
                   col_spec(NA_W), row_spec(NA_W), col_spec(NA_HEADS * V_EXT)],
        out_shape=[cols_out(mla_w), rows_out(mla_w), cols_out(v_w),
                   cols_out(NA_W), rows_out(NA_W), cols_out(NA_HEADS * V_EXT)],
        compiler_params=_cparams(1),
        name="odd_in_proj",
    )(x, modtab, *consts, *tabs)


def _key_max8(s, ways=4):
    n = s.shape[0] // ways
    parts = [jnp.max(s[i * n:(i + 1) * n].reshape(n // 8, 8, s.shape[1]), axis=0) for i in range(ways)]
    while len(parts) > 1:
        parts = [jnp.maximum(a, b) for a, b in zip(parts[::2], parts[1::2])]
    return parts[0]


def _softmax_heads(n_heads, scores, values, o_ref):
    def probabilities(s_parts):
        m = jnp.max(functools.reduce(jnp.maximum, [_key_max8(s) for s in s_parts]), axis=0, keepdims=True)
        return [jnp.exp2(s - m).astype(BF16) for s in s_parts]

    def weighted_values(h, p_parts):
        acc = None
        for p, vt in zip(p_parts, values(h)):
            part = _dot(vt, p)
            acc = part if acc is None else acc + part
        return acc[:HEAD_DIM] / acc[HEAD_DIM:]

    outs = []
    s_cur, p_prev = scores(0), None
    for h in range(n_heads):
        s_next = scores(h + 1) if h + 1 < n_heads else None
        p_cur = probabilities(s_cur)
        if p_prev is not None:
            outs.append(weighted_values(h - 1, p_prev))
        s_cur, p_prev = s_next, p_cur
    outs.append(weighted_values(n_heads - 1, p_prev))
    o_ref[...] = jnp.concatenate(outs, axis=0).T.astype(o_ref.dtype)


def _attn_kernel(qt_ref, kc_ref, kl_ref, vtc_ref, vtl_ref, o_ref, *, n_heads, group, dk, first_step):
    tq = qt_ref.shape[1]
    n_lat = kl_ref.shape[0]
    k_width = kc_ref.shape[1]
    pad_q = k_width <= 256

    def scores(h, with_latent):
        kv = h // group
        qt = qt_ref[h * dk:(h + 1) * dk, :]
        if pad_q:
            pieces = []
            if kv:
                pieces.append(jnp.zeros((kv * dk, tq), qt.dtype))
            pieces.append(qt)
            if k_width - (kv + 1) * dk:
                pieces.append(jnp.zeros((k_width - (kv + 1) * dk, tq), qt.dtype))
            qt = jnp.concatenate(pieces, axis=0)
            ksl = slice(None)
        else:
            ksl = slice(kv * dk, (kv + 1) * dk)
        parts = [_dot(kc_ref[:, ksl], qt)]
        if with_latent:
            parts += [_dot(kl_ref[c * KEY_CHUNK:(c + 1) * KEY_CHUNK, ksl], qt) for c in range(n_lat // KEY_CHUNK)]
        return parts

    def values(h, with_latent):
        vsl = slice((h // group) * V_EXT, (h // group + 1) * V_EXT)
        parts = [vtc_ref[vsl, :]]
        if with_latent:
            parts += [vtl_ref[vsl, c * KEY_CHUNK:(c + 1) * KEY_CHUNK] for c in range(n_lat // KEY_CHUNK)]
        return parts

    def attend(with_latent):
        _softmax_heads(n_heads, lambda h: scores(h, with_latent), lambda h: values(h, with_latent), o_ref)

    if first_step == 0:
        is_ctx = pl.program_id(1) == 0

        @pl.when(is_ctx)
        def _():
            attend(False)

        @pl.when(jnp.logical_not(is_ctx))
        def _():
            attend(True)
    else:
        attend(True)


def _attention(qt, k, vt, rows, *, n_heads, group, dk, skip_ctx_queries):
    first = 1 if skip_ctx_queries else 0
    kern = functools.partial(_attn_kernel, n_heads=n_heads, group=group, dk=dk, first_step=first)
    wo = n_heads * HEAD_DIM
    return pl.pallas_call(
        kern,
        grid=(rows.batch, rows.n_lat // TQ + 1 - first),
        in_specs=[pl.BlockSpec((qt.shape[0], TQ), lambda b, i: (0, rows.query_block(b, i + first, TQ))),
                  pl.BlockSpec((N_CTX, k.shape[1]), lambda b, i: (b, 0)),
                  pl.BlockSpec((rows.n_lat, k.shape[1]), lambda b, i: (rows.latent_block(b), 0)),
                  pl.BlockSpec((vt.shape[0], N_CTX), lambda b, i: (0, b)),
                  pl.BlockSpec((vt.shape[0], rows.n_lat), lambda b, i: (0, rows.latent_block(b)))],
        out_specs=pl.BlockSpec((TQ, wo), lambda b, i: (rows.query_block(b, i + first, TQ), 0)),
        out_shape=jax.ShapeDtypeStruct((rows.n_rows, wo), BF16),
        compiler_params=_cparams(2),
        name="attention_dk%d" % dk,
    )(qt, k, k, vt, vt)


def _na_geometry(grid_rows):
    rb = NA_ROWS_PER_BLOCK
    wr = min(NA_WIN_R, grid_rows)
    assert grid_rows % 2 == 0 and GRID_W * 2 == 128
    span = min(grid_rows, wr + rb - 1 + (wr + rb - 1) % 2)
    starts, variants, keys = [], [], {}
    for r0 in range(0, grid_rows, rb):
        rs0 = int(np.clip(r0 - wr // 2, 0, grid_rows - wr))
        start = min(rs0, grid_rows - span)
        start -= start % 2
        rel = tuple(int(np.clip(r0 + dr - wr // 2, 0, grid_rows - wr)) - (r0 + dr) for dr in range(rb))
        key = (rel, start - r0)
        variants.append(keys.setdefault(key, len(keys)))
        starts.append(start)
    firsts = [variants.index(v) for v in range(len(keys))]
    return wr, span, starts, variants, firsts


def _na_bias_table(rpb, grid_rows):
    rb = NA_ROWS_PER_BLOCK
    wr, span, starts, variants, firsts = _na_geometry(grid_rows)
    nv = len(firsts)
    idx_r = np.zeros((nv, rb, span), np.int32)
    ok_r = np.zeros((nv, rb, span), bool)
    for v, blk in enumerate(firsts):
        r0, start = blk * rb, starts[blk]
        for dr in range(rb):
            r = r0 + dr
            rs = int(np.clip(r - wr // 2, 0, grid_rows - wr))
            for j in range(span):
                kr = start + j
                ok_r[v, dr, j] = rs <= kr < rs + wr
                idx_r[v, dr, j] = np.clip(kr - r + NA_WIN_R - 1, 0, 2 * NA_WIN_R - 2)
    c = np.arange(GRID_W)
    cs = np.clip(c - NA_WIN_C // 2, 0, GRID_W - NA_WIN_C)
    kc = np.arange(GRID_W)
    ok_c = (kc[None, :] >= cs[:, None]) & (kc[None, :] < cs[:, None] + NA_WIN_C)
    idx_c = np.clip(kc[None, :] - c[:, None] + NA_WIN_C - 1, 0, 2 * NA_WIN_C - 2)
    n_rel_c = 2 * NA_WIN_C - 1
    picked = rpb[:, idx_r.reshape(-1)].reshape(rpb.shape[0], nv, rb, span, n_rel_c)
    picked = jnp.transpose(picked, (1, 0, 3, 2, 4)).reshape(nv, rpb.shape[0], span, rb * n_rel_c) * LOG2E
    col_hit = idx_c.T[None, :, :] == np.arange(n_rel_c)[:, None, None]
    spread = (np.eye(rb, dtype=bool)[:, None, None, :, None] & col_hit[None, :, :, None, :])
    spread = jnp.asarray(spread.reshape(rb * n_rel_c, GRID_W, rb, GRID_W), F32)
    to_bf16_grid = lambda t: lax.reduce_precision(t, exponent_bits=8, mantissa_bits=7)
    hi = to_bf16_grid(picked)
    mid = to_bf16_grid(picked - hi)
    low = picked - hi - mid
    bias = sum(jnp.einsum("vhjy,ykdc->vhjkdc", term, spread) for term in (hi, mid, low))
    ok = ok_r[:, None, :, None, :, None] & ok_c[None, None, None, :, None, :]
    ok = np.transpose(ok, (0, 1, 4, 5, 2, 3))
    bias = jnp.where(ok, bias, MASK_VALUE)
    return bias.reshape(nv, rpb.shape[0], span * GRID_W, rb * GRID_W)


def _na_kernel(start_ref, var_ref, qt_ref, kc_ref, kl_ref, vtc_ref, vtl_ref, bias_ref, o_ref, *, n_loc, first_step):
    i = pl.program_id(1) + first_step
    tq = qt_ref.shape[1]

    def padded_query(h):
        qt = qt_ref[h * HEAD_DIM:(h + 1) * HEAD_DIM, :]
        zeros = jnp.zeros((HEAD_DIM, tq), qt.dtype)
        return jnp.concatenate([qt, zeros] if h % 2 == 0 else [zeros, qt], axis=0)

    def pair_lanes(h):
        return slice((h // 2) * 2 * HEAD_DIM, (h // 2 + 1) * 2 * HEAD_DIM)

    def value_rows(h):
        return slice(h * V_EXT, (h + 1) * V_EXT)

    def context_queries():
        _softmax_heads(NA_HEADS, lambda h: [_dot(kc_ref[:, pair_lanes(h)], padded_query(h))],
                       lambda h: [vtc_ref[value_rows(h), :]], o_ref)

    def latent_queries():
        off = pl.multiple_of(start_ref[i - 1] * GRID_W, 2 * GRID_W)
        var = var_ref[i - 1]

        def scores(h):
            qt = padded_query(h)
            return [_dot(kc_ref[:, pair_lanes(h)], qt),
                    _dot(kl_ref[pl.ds(off, n_loc), pair_lanes(h)], qt) + bias_ref[var, h]]

        _softmax_heads(NA_HEADS, scores,
                       lambda h: [vtc_ref[value_rows(h), :], vtl_ref[value_rows(h), pl.ds(off, n_loc)]], o_ref)

    if first_step == 0:
        pl.when(i == 0)(context_queries)
        pl.when(i > 0)(latent_queries)
    else:
        latent_queries()


def _neighbourhood_attention(qt, k, vt, bias, rows, skip_ctx_queries):
    grid_rows = rows.n_lat // GRID_W
    _, span, starts, variants, _ = _na_geometry(grid_rows)
    assert NA_ROWS_PER_BLOCK * GRID_W == TQ
    first = 1 if skip_ctx_queries else 0
    kern = functools.partial(_na_kernel, n_loc=span * GRID_W, first_step=first)
    qblock = lambda b, i: rows.query_block(b, i + first, TQ)
    v_w = vt.shape[0]
    grid_spec = pltpu.PrefetchScalarGridSpec(
        num_scalar_prefetch=2,
        grid=(rows.batch, rows.n_lat // TQ + 1 - first),
        in_specs=[pl.BlockSpec((NA_W, TQ), lambda b, i, *_: (0, qblock(b, i))),
                  pl.BlockSpec((N_CTX, NA_W), lambda b, i, *_: (b, 0)),
                  pl.BlockSpec((rows.n_lat, NA_W), lambda b, i, *_: (rows.latent_block(b), 0)),
                  pl.BlockSpec((v_w, N_CTX), lambda b, i, *_: (0, b)),
                  pl.BlockSpec((v_w, rows.n_lat), lambda b, i, *_: (0, rows.latent_block(b))),
                  pl.BlockSpec(bias.shape, lambda b, i, *_: (0, 0, 0, 0), pipeline_mode=pl.Buffered(1))],
        out_specs=pl.BlockSpec((TQ, NA_W), lambda b, i, *_: (qblock(b, i), 0)),
    )
    return pl.pallas_call(
        kern,
        grid_spec=grid_spec,
        out_shape=jax.ShapeDtypeStruct((rows.n_rows, NA_W), BF16),
        compiler_params=_cparams(2),
        name="neighbourhood_attention",
    )(jnp.asarray(starts, jnp.int32), jnp.asarray(variants, jnp.int32), qt, k, k, vt, vt, bias)


def _s5_matrices(lam_re, lam_im, log_dt, b_re, b_im, c_re, c_im, d_skip):
    L, G, P, N = SSM_CHUNK, SSM_GROUPS, SSM_GROUP, SSM_STATE
    hi = lax.Precision.HIGHEST
    dt = jnp.exp(log_dt)[..., None]
    pw = jnp.arange(L + 1, dtype=F32)[:, None, None, None]
    mag = jnp.exp(lam_re * dt * pw)
    e_re = mag * jnp.cos(lam_im * dt * pw)
    e_im = mag * jnp.sin(lam_im * dt * pw)
    a_re, a_im = e_re[1], e_im[1]
    den = jnp.square(lam_re) + jnp.square(lam_im)
    f_re = ((a_re - 1.0) * lam_re + a_im * lam_im) / den
    f_im = (a_im * lam_re - (a_re - 1.0) * lam_im) / den
    bb_re = f_re[..., None] * b_re - f_im[..., None] * b_im
    bb_im = f_re[..., None] * b_im + f_im[..., None] * b_re
    ce_re = c_re[None] * e_re[:, :, :, None, :] - c_im[None] * e_im[:, :, :, None, :]
    ce_im = c_re[None] * e_im[:, :, :, None, :] + c_im[None] * e_re[:, :, :, None, :]
    kk = (jnp.einsum("kdgqn,dgnp->kdgqp", ce_re, bb_re, precision=hi)
          - jnp.einsum("kdgqn,dgnp->kdgqp", ce_im, bb_im, precision=hi))
    lag = np.arange(L)[None, :] - np.arange(L)[:, None]
    kf = kk[np.clip(lag, 0, L - 1), 0]
    kb = kk[np.clip(-lag, 0, L - 1), 1]
    skip = jnp.eye(P, dtype=F32)[None] * d_skip.reshape(G, P)[:, None, :]
    m = lambda cond: jnp.asarray(cond, F32)[:, :, None, None, None]
    kst = m(lag >= 0) * kf + m(lag <= 0) * kb + m(lag == 0) * skip[None, None]
    a_t = jnp.transpose(kst, (0, 2, 4, 1, 3)).reshape(L * G * P, L * P)

    def state_in(pows, d):
        x_re = e_re[pows, d][..., None] * bb_re[d][None] - e_im[pows, d][..., None] * bb_im[d][None]
        x_im = e_re[pows, d][..., None] * bb_im[d][None] + e_im[pows, d][..., None] * bb_re[d][None]
        flat = lambda x: jnp.pad(jnp.transpose(x, (0, 1, 3, 2)).reshape(L * G * P, N), ((0, 0), (0, L * P - N)))
        return [flat(x_re), flat(x_im)]

    a_w1 = jnp.concatenate([a_t] + state_in(np.arange(L)[::-1].copy(), 0) + state_in(np.arange(L), 1), axis=1)
    w1 = _expand_group_blocks(a_w1[None], L * G * P + 4 * G * N, row_shift=4)

    def state_out(x):
        return jnp.transpose(x, (1, 3, 0, 2)).reshape(G * N, L * P)

    pf, pb = np.arange(1, L + 1), np.arange(L, 0, -1)
    a_wc = jnp.stack([jnp.concatenate([state_out(ce_re[pf, 0]), state_out(ce_re[pb, 1])], axis=0),
                      jnp.concatenate([state_out(-ce_im[pf, 0]), state_out(-ce_im[pb, 1])], axis=0)])
    wc = _expand_group_blocks(a_wc, L * G * P, row_shift=6)
    al_re = e_re[L].reshape(2, 1, G * N)
    al_im = e_im[L].reshape(2, 1, G * N)
    return w1[0], wc[0], wc[1], al_re, al_im


def _expand_kernel(a_ref, o_ref, *, row_shift, n_response_tiles):
    j = pl.program_id(1)
    a = a_ref[0].astype(BF16)
    n_rows, tn = o_ref.shape[1], o_ref.shape[2]
    src = lax.broadcasted_iota(jnp.int32, (a.shape[1], tn), 0)
    col = lax.broadcasted_iota(jnp.int32, (a.shape[1], tn), 1)
    row_group = (lax.broadcasted_iota(jnp.int32, (n_rows, tn), 0) >> row_shift) & (SSM_GROUPS - 1)
    out_col = lax.broadcasted_iota(jnp.int32, (n_rows, tn), 1)

    def emit(spread, col_group):
        val = _dot(a, jnp.where(spread, 1.0, 0.0).astype(BF16))
        o_ref[0] = jnp.where(row_group == col_group, val, 0.0).astype(o_ref.dtype)

    @pl.when(j < n_response_tiles)
    def _():
        cg = col + j * tn
        spread = ((src >> 4) == (cg >> 8)) & ((src & 15) == (cg & 15))
        emit(spread, ((out_col + j * tn) >> 4) & (SSM_GROUPS - 1))

    @pl.when(j >= n_response_tiles)
    def _():
        spread = src == (col & (SSM_STATE - 1))
        emit(spread, (out_col >> 6) & (SSM_GROUPS - 1))


def _expand_group_blocks(a, n_cols, *, row_shift):
    k, n_rows, _ = a.shape
    tn = 1024
    n_response_tiles = SSM_CHUNK * SSM_WIDTH // tn
    assert SSM_GROUP == 16 and SSM_STATE == 64 and SSM_CHUNK * SSM_GROUP == 128 and tn % (SSM_GROUPS * SSM_STATE) == 0
    kern = functools.partial(_expand_kernel, row_shift=row_shift, n_response_tiles=n_response_tiles)
    return pl.pallas_call(
        kern,
        grid=(k, n_cols // tn),
        in_specs=[pl.BlockSpec((1, n_rows, 128), lambda d, j: (d, 0, jnp.maximum(j - n_response_tiles + 1, 0)))],
        out_specs=pl.BlockSpec((1, n_rows, tn), lambda d, j: (d, 0, j)),
        out_shape=jax.ShapeDtypeStruct((k, n_rows, n_cols), BF16),
        compiler_params=_cparams(2),
        name="s5_expand",
    )(a)


def _mm_kernel(a_ref, b_ref, o_ref):
    o_ref[...] = _dot(a_ref[...], b_ref[...]).astype(o_ref.dtype)


def _matmul(a, b, tm, tn, out_dtype):
    m, kdim = a.shape
    n = b.shape[1]
    return pl.pallas_call(
        _mm_kernel,
        grid=(n // tn, m // tm),
        in_specs=[pl.BlockSpec((tm, kdim), lambda j, i: (i, 0)),
                  pl.BlockSpec((kdim, tn), lambda j, i: (0, j))],
        out_specs=pl.BlockSpec((tm, tn), lambda j, i: (i, j)),
        out_shape=jax.ShapeDtypeStruct((m, n), out_dtype),
        compiler_params=_cparams(2),
        name="s5_chunk_matmul",
    )(a, b)


def _s5_carry_kernel(sre_ref, sim_ref, are_ref, aim_ref, hre_ref, him_ref, *, ctx_chunks):
    n_chunks = sre_ref.shape[0]
    backward = pl.program_id(0) == 1
    a_re = jnp.broadcast_to(are_ref[0], sre_ref.shape[1:])
    a_im = jnp.broadcast_to(aim_ref[0], sre_ref.shape[1:])

    def step(j, carry):
        h_re, h_im = carry
        cb = jnp.where(j < ctx_chunks, ctx_chunks - 1 - j, n_chunks + ctx_chunks - 1 - j)
        c = jnp.where(backward, cb, j)
        hre_ref[c] = h_re.astype(hre_ref.dtype)
        him_ref[c] = h_im.astype(him_ref.dtype)
        n_re = a_re * h_re - a_im * h_im + sre_ref[c]
        n_im = a_re * h_im + a_im * h_re + sim_ref[c]
        return n_re, n_im

    zero = jnp.zeros(sre_ref.shape[1:], F32)
    lax.fori_loop(0, n_chunks, step, (zero, zero))


def _s5_carry(ys, al_re, al_im, n_chunks, batch, ctx_chunks):
    gn = SSM_GROUPS * SSM_STATE
    tw = 256
    base = SSM_CHUNK * SSM_WIDTH // tw
    per_dir = 2 * gn // tw
    s3 = ys.reshape(n_chunks, batch, ys.shape[1])
    kern = functools.partial(_s5_carry_kernel, ctx_chunks=ctx_chunks)
    blk = (n_chunks, batch, tw)
    return pl.pallas_call(
        kern,
        grid=(2, gn // tw),
        in_specs=[pl.BlockSpec(blk, lambda d, j: (0, 0, base + d * per_dir + j)),
                  pl.BlockSpec(blk, lambda d, j: (0, 0, base + d * per_dir + gn // tw + j)),
                  pl.BlockSpec((1, 1, tw), lambda d, j: (d, 0, j)),
                  pl.BlockSpec((1, 1, tw), lambda d, j: (d, 0, j))],
        out_specs=[pl.BlockSpec(blk, lambda d, j: (0, 0, d * (gn // tw) + j))] * 2,
        out_shape=[jax.ShapeDtypeStruct((n_chunks, batch, 2 * gn), BF16)] * 2,
        compiler_params=_cparams(2),
        name="s5_carry",
    )(s3, s3, al_re, al_im)


def _s5_readout_kernel(hre_ref, him_ref, wre_ref, wim_ref, y_ref, o_ref):
    o_ref[...] = y_ref[...] + _dot(hre_ref[...], wre_ref[...]) + _dot(him_ref[...], wim_ref[...])


def _s5_readout(h_re, h_im, wc_re, wc_im, ys, tm):
    m, kdim = h_re.shape
    n = wc_re.shape[1]
    tn = 1024
    return pl.pallas_call(
        _s5_readout_kernel,
        grid=(n // tn, m // tm),
        in_specs=[pl.BlockSpec((tm, kdim), lambda j, i: (i, 0)),
                  pl.BlockSpec((tm, kdim), lambda j, i: (i, 0)),
                  pl.BlockSpec((kdim, tn), lambda j, i: (0, j)),
                  pl.BlockSpec((kdim, tn), lambda j, i: (0, j)),
                  pl.BlockSpec((tm, tn), lambda j, i: (i, j))],
        out_specs=pl.BlockSpec((tm, tn), lambda j, i: (i, j)),
        out_shape=jax.ShapeDtypeStruct((m, n), F32),
        compiler_params=_cparams(2),
        name="s5_readout",
    )(h_re, h_im, wc_re, wc_im, ys)


def _row_tile(m, target=512):
    t = min(m, target)
    while m % t or t % 16:
        t -= 16
    return t


def _s5_mixer(u, mats, rows):
    w1, wc_re, wc_im, al_re, al_im = mats
    batch = rows.batch
    cw = SSM_CHUNK * SSM_WIDTH
    ctx_chunks = N_CTX // SSM_CHUNK
    n_chunks = ctx_chunks + rows.n_lat // SSM_CHUNK
    n_ctx_chunk_rows = rows.n_ctx_rows // SSM_CHUNK
    uc = jnp.concatenate([u[:n_ctx_chunk_rows].reshape(batch, ctx_chunks, cw),
                          u[n_ctx_chunk_rows:].reshape(batch, n_chunks - ctx_chunks, cw)], axis=1)
    uc = jnp.transpose(uc, (1, 0, 2)).reshape(n_chunks * batch, cw)
    tm = _row_tile(n_chunks * batch)
    ys = _matmul(uc, w1, tm, 1024, F32)
    h_re, h_im = _s5_carry(ys, al_re, al_im, n_chunks, batch, ctx_chunks)
    gn2 = 2 * SSM_GROUPS * SSM_STATE
    y = _s5_readout(h_re.reshape(-1, gn2), h_im.reshape(-1, gn2), wc_re, wc_im, ys, tm)
    y = jnp.transpose(y.reshape(n_chunks, batch, cw), (1, 0, 2))
    return jnp.concatenate([y[:, :ctx_chunks].reshape(n_ctx_chunk_rows, cw),
                            y[:, ctx_chunks:].reshape(-1, cw)], axis=0)


def _gelu_tanh(y):
    return 0.5 * y * (1.0 + jnp.tanh(math.sqrt(2.0 / math.pi) * (y + 0.044715 * (y * y * y))))


def _out_mlp_kernel(*refs, even, ctx_blocks):
    read_x, refs = _stream_reader(refs, ctx_blocks)
    if even:
        m1_ref, m2_ref, mod_ref, perm_ref, wglu_ref, bglu_ref, wo_ref, g2_ref, w1_ref, w2_ref, o_ref = refs
        yc = m2_ref[...]
        ys = jnp.concatenate([yc[:, s * SSM_WIDTH:(s + 1) * SSM_WIDTH] for s in range(SSM_CHUNK)], axis=0)
        hi = ys.astype(BF16)
        lo = (ys - hi.astype(F32)).astype(BF16)
        y = _gelu_tanh(_dot(perm_ref[...], hi) + _dot(perm_ref[...], lo))
        z = _dot(y.astype(BF16), wglu_ref[...]) + bglu_ref[...]
        second = (y * jax.nn.sigmoid(z)).astype(BF16)
    else:
        m1_ref, m2_ref, mod_ref, wo_ref, g2_ref, w1_ref, w2_ref, o_ref = refs
        second = m2_ref[...]
    m = mod_ref[0]
    mod = lambda j: m[:, j * D_MODEL:(j + 1) * D_MODEL]
    mix = jnp.concatenate([m1_ref[...], second], axis=1)
    x1 = read_x(slice(None)) + mod(2) * _dot(mix, wo_ref[...])
    a = _modulate(x1, g2_ref[...], mod(3), mod(4)).astype(BF16)
    acc = None
    ck = 1024
    for c in range(D_FF // ck):
        h = jnp.maximum(_dot(a, w1_ref[:, c * ck:(c + 1) * ck]), 0.0)
        part = _dot((h * h).astype(BF16), w2_ref[c * ck:(c + 1) * ck, :])
        acc = part if acc is None else acc + part
    o_ref[...] = x1 + mod(5) * acc


def _out_mlp(x, mix1, mix2, modtab, consts, rows, even, latent_only):
    first = rows.ctx_blocks if latent_only else 0
    row_spec = lambda w: pl.BlockSpec((TM, w), lambda i: (i + first, 0))
    n_blocks = rows.n_rows // TM - first
    mix2_rows = TM // SSM_CHUNK if even else TM
    if isinstance(x, tuple):
        assert not latent_only
        x_parts, x_specs, ctx_blocks = _stream_specs(x, rows)
    else:
        x_parts, x_specs, ctx_blocks = (x,), [row_spec(D_MODEL)], None
    return pl.pallas_call(
        functools.partial(_out_mlp_kernel, even=even, ctx_blocks=ctx_blocks),
        grid=(n_blocks,),
        in_specs=x_specs + [row_spec(mix1.shape[1]),
                  pl.BlockSpec((mix2_rows, mix2.shape[1]), lambda i: (i + first, 0)),
                  pl.BlockSpec((1, 1, N_MOD * D_MODEL), lambda i: (rows.mod_index(i + first), 0, 0))]
                 + [_resident(c.shape) for c in consts],
        out_specs=pl.BlockSpec((TM, D_MODEL), lambda i: (i, 0)),
        out_shape=jax.ShapeDtypeStruct((n_blocks * TM, D_MODEL), F32),
        compiler_params=_cparams(1),
        name="out_mlp_even" if even else "out_mlp_odd",
    )(*x_parts, mix1, mix2, modtab, *consts)


def _rope_tables(n_lat, rot_dim, head_w, lane_off):
    t = jnp.arange(n_lat)
    grid_r = (t // GRID_W).astype(F32)
    grid_c = (t % GRID_W).astype(F32)
    axis_dim = rot_dim // 2
    freqs = ROPE_BASE ** (-jnp.arange(0, axis_dim, 2, dtype=F32) / axis_dim)
    ang_r = grid_r[:, None] * freqs
    ang_c = grid_c[:, None] * freqs
    ang = jnp.concatenate([ang_r, ang_r, ang_c, ang_c], axis=-1)
    cos, sin = jnp.cos(ang), jnp.sin(ang)
    quarter = rot_dim // 4
    first = (np.arange(rot_dim) // quarter) % 2 == 0
    sin_a = jnp.where(first, -sin, 0.0)
    sin_b = jnp.where(first, 0.0, sin)

    def widen(tab, fill):
        full = jnp.full((n_lat, head_w), fill, F32).at[:, lane_off:lane_off + rot_dim].set(tab)
        full = jnp.concatenate([jnp.full((TM, head_w), fill, F32), full], axis=0)
        return jnp.tile(full, (1, 128 // head_w))

    return widen(cos, 1.0), widen(sin_a, 0.0), widen(sin_b, 0.0)


def _with_gain(tabs, gain, shift):
    cos, sin_a, sin_b = tabs
    return cos * gain, sin_a * jnp.roll(gain, -shift), sin_b * jnp.roll(gain, shift)


def _head_sum_matrices(n_heads, head_dim):
    down = np.kron(np.eye(n_heads), np.ones((head_dim, 1)))
    pad = (-n_heads) % 128
    down = np.pad(down, ((0, 0), (0, pad)))
    return jnp.asarray(down, BF16), jnp.asarray(down.T, BF16)


def _pad_heads(w, n_heads, width, pad_to):
    lead = w.shape[:-1]
    w = w.reshape(lead + (n_heads, width))
    w = jnp.pad(w, [(0, 0)] * len(lead) + [(0, 0), (0, pad_to - width)])
    return w.reshape(lead + (n_heads * pad_to,))


def kernel(x, c, ctx, c_ctx, w_mod, b_mod, g_norm1, g_norm2, w_ff1, w_ff2, e_w_in, e_w_out, e_g_q, e_g_k, ssm_lam_re, ssm_lam_im, ssm_log_dt, ssm_b_re, ssm_b_im, ssm_c_re, ssm_c_im, ssm_d, ssm_w_glu, ssm_b_glu, o_w_in, o_w_out, mla_g_cq, mla_g_ckv, mla_w_uq, mla_w_ukv, mla_g_q, mla_g_k, na_g_q, na_g_k, na_rpb):
    batch, n_lat, d = x.shape
    depth = w_mod.shape[0]
    assert d == D_MODEL and ctx.shape[1] == N_CTX and n_lat % GRID_W == 0
    rows = _Rows(batch, n_lat)

    pad_rows = (-(batch + 1)) % 8
    cond = jnp.concatenate([c_ctx[None], c, jnp.zeros((pad_rows, d), F32)], axis=0)
    modtabs = _mod_vectors(cond, w_mod, b_mod).reshape(depth, -1, 1, N_MOD * d)

    xs = (ctx.reshape(rows.n_ctx_rows, d), x.reshape(batch * n_lat, d))

    even_tabs = _rope_tables(n_lat, HEAD_DIM, HEAD_DIM, 0)
    mla_tabs = _rope_tables(n_lat, MLA_ROPE, MLA_PAD, MLA_NOPE)
    down_e, up_e = _head_sum_matrices(GQA_Q_HEADS + GQA_KV_HEADS, HEAD_DIM)
    down_o, up_o = _head_sum_matrices(2 * NA_HEADS, HEAD_DIM)
    row = lambda v: v.reshape(1, -1)

    def chunk_perm(n):
        tok = np.arange(n)
        mat = np.zeros((n, n), np.float32)
        mat[(tok % SSM_CHUNK) * (n // SSM_CHUNK) + tok // SSM_CHUNK, tok] = 1.0
        return mat

    perm = jnp.asarray(chunk_perm(EVEN_SUB_ROWS), BF16)
    perm_t = jnp.asarray(chunk_perm(TM).T, BF16)

    for i in range(depth):
        j = i // 2
        last = i == depth - 1
        g1, g2 = row(g_norm1[i]), row(g_norm2[i])
        w1, w2 = w_ff1[i].astype(BF16), w_ff2[i].astype(BF16)
        if i % 2 == 0:
            lanes = 128 // HEAD_DIM
            tabs = (_with_gain(even_tabs, jnp.tile(e_g_q[j], lanes) * (HEAD_DIM ** -0.5 * LOG2E), HEAD_DIM // 4)
                    + _with_gain(even_tabs, jnp.tile(e_g_k[j], lanes), HEAD_DIM // 4))
            q, k, v, u = _even_in_proj(xs, modtabs[i], (g1, e_w_in[j].astype(BF16), down_e, up_e, perm), tabs, rows)
            att = _attention(q, k, v, rows, n_heads=GQA_Q_HEADS, group=GQA_GROUP, dk=HEAD_DIM, skip_ctx_queries=last)
            mats = _s5_matrices(ssm_lam_re[j], ssm_lam_im[j], ssm_log_dt[j], ssm_b_re[j], ssm_b_im[j],
                                ssm_c_re[j], ssm_c_im[j], ssm_d[j])
            y = _s5_mixer(u, mats, rows)
            consts = (perm_t, ssm_w_glu[j].astype(BF16), row(ssm_b_glu[j]), e_w_out[j].astype(BF16), g2, w1, w2)
            xs = _out_mlp(xs, att, y, modtabs[i], consts, rows, True, last)
        else:
            w_in = o_w_in[j]
            c1 = MLA_Q_RANK
            c2 = c1 + MLA_KV_RANK
            c3 = c2 + MLA_ROPE
            kr_cols = jnp.pad(w_in[:, c2:c3], ((0, 0), (MLA_NOPE, MLA_PAD - MLA_QK)))
            w_in_p = jnp.concatenate([w_in[:, :c2], kr_cols, w_in[:, c3:]], axis=1).astype(BF16)
            wuq = _pad_heads(mla_w_uq[j], MLA_HEADS, MLA_QK, MLA_PAD).astype(BF16)
            wukv = mla_w_ukv[j].reshape(MLA_KV_RANK, MLA_HEADS, MLA_NOPE + MLA_V)
            wuk = _pad_heads(wukv[:, :, :MLA_NOPE].reshape(MLA_KV_RANK, -1), MLA_HEADS, MLA_NOPE, MLA_PAD).astype(BF16)
            wuv = wukv[:, :, MLA_NOPE:].reshape(MLA_KV_RANK, -1).astype(BF16)
            gmq = _pad_heads(jnp.tile(mla_g_q[j], MLA_HEADS) * (MLA_QK ** -0.5 * LOG2E), MLA_HEADS, MLA_QK, MLA_PAD)
            gmk = _pad_heads(jnp.tile(mla_g_k[j], MLA_HEADS), MLA_HEADS, MLA_QK, MLA_PAD)
            gnqk = jnp.concatenate([jnp.tile(na_g_q[j], NA_HEADS) * (HEAD_DIM ** -0.5 * LOG2E),
                                    jnp.tile(na_g_k[j], NA_HEADS)])
            consts = (g1, w_in_p, row(mla_g_cq[j]), row(mla_g_ckv[j]), wuq, wuk, wuv, row(gmq), row(gmk),
                      down_o, up_o, row(gnqk))
            mq, mk, mv, nq, nk, nv = _odd_in_proj(xs, modtabs[i], consts, mla_tabs, rows)
            mla = _attention(mq, mk, mv, rows, n_heads=MLA_HEADS, group=1, dk=MLA_PAD, skip_ctx_queries=last)
            bias = _na_bias_table(na_rpb[j], n_lat // GRID_W)
            na = _neighbourhood_attention(nq, nk, nv, bias, rows, last)
            consts = (o_w_out[j].astype(BF16), g2, w1, w2)
            xs = _out_mlp(xs, mla, na, modtabs[i], consts, rows, False, last)
    return xs.reshape(batch, n_lat, d)
```

```python
import functools
import math

import numpy as np
import jax
import jax.numpy as jnp
from jax import lax
from jax.experimental import pallas as pl
from jax.experimental.pallas import tpu as pltpu

F32 = jnp.float32
BF16 = jnp.bfloat16

D_MODEL = 1024
GRID_W = 64
HEAD_DIM = 64
ROPE_BASE = 10000.0
EPS = 1e-6
N_MOD = 6
D_FF = 4 * D_MODEL
LOG2E = math.log2(math.e)

GQA_Q_HEADS = 12
GQA_KV_HEADS = 4
GQA_GROUP = GQA_Q_HEADS // GQA_KV_HEADS
GQA_Q_W = GQA_Q_HEADS * HEAD_DIM
GQA_KV_W = GQA_KV_HEADS * HEAD_DIM
SSM_WIDTH = 256
SSM_GROUP = 16
SSM_GROUPS = SSM_WIDTH // SSM_GROUP
SSM_STATE = 64
SSM_CHUNK = 8

MLA_HEADS = 8
MLA_Q_RANK = 512
MLA_KV_RANK = 256
MLA_NOPE = 64
MLA_ROPE = 32
MLA_QK = MLA_NOPE + MLA_ROPE
MLA_V = 64
MLA_PAD = 128
V_EXT = 128
NA_HEADS = 8
NA_W = NA_HEADS * HEAD_DIM
NA_WIN_R = 8
NA_WIN_C = 16
NA_ROWS_PER_BLOCK = 4

N_CTX = 256
TM = 512
EVEN_SUB_ROWS = 256
ODD_SUB_ROWS = 128
TQ = 256
KEY_CHUNK = 1024
MASK_VALUE = -1e30
VMEM_LIMIT = 52 * 1024 * 1024


def _cparams(n_axes):
    return pltpu.CompilerParams(dimension_semantics=("parallel",) * n_axes, vmem_limit_bytes=VMEM_LIMIT)


def _resident(shape):
    nd = len(shape)
    return pl.BlockSpec(shape, lambda *_: (0,) * nd, pipeline_mode=pl.Buffered(1))


def _dot(a, b):
    return jnp.dot(a, b, preferred_element_type=F32)


def _dot_nt(a, b):
    return lax.dot_general(a, b, (((1,), (1,)), ((), ())), preferred_element_type=F32)


def _split_dot(x, w):
    hi = x.astype(BF16)
    lo = (x - hi.astype(F32)).astype(BF16)
    return _dot(hi, w) + _dot(lo, w)


def _modulate(xf, g, shift, scale):
    ms = jnp.mean(xf * xf, axis=-1, keepdims=True)
    return (xf * lax.rsqrt(ms + EPS)) * (g * (1.0 + scale)) + shift


def _head_inv_rms(x, down, up, head_dim):
    ss = _split_dot(x * x, down)
    return _split_dot(lax.rsqrt(ss * (1.0 / head_dim) + EPS), up)


def _head_rms(x, down, up, head_dim):
    return x * _head_inv_rms(x, down, up, head_dim)


def _rope(x, cos, sin_a, sin_b, shift):
    w = cos.shape[-1]
    blocks = []
    for j in range(x.shape[-1] // w):
        xb = x[:, j * w:(j + 1) * w]
        blocks.append(xb * cos + pltpu.roll(xb, w - shift, 1) * sin_a + pltpu.roll(xb, shift, 1) * sin_b)
    return jnp.concatenate(blocks, axis=1)


def _issue_ahead(project, finish, sub_rows):
    n = TM // sub_rows
    piece = lambda r: slice(r * sub_rows, (r + 1) * sub_rows)
    nxt = project(piece(0))
    for r in range(n):
        h, nxt = nxt, (project(piece(r + 1)) if r + 1 < n else None)
        finish(r, piece(r), h)


class _Rows:
    def __init__(self, batch, n_lat):
        assert (batch * N_CTX) % TM == 0 and n_lat % TM == 0 and (batch * N_CTX) % n_lat == 0
        self.batch, self.n_lat = batch, n_lat
        self.n_ctx_rows = batch * N_CTX
        self.n_rows = self.n_ctx_rows + batch * n_lat
        self.ctx_blocks = self.n_ctx_rows // TM
        self.lat_blocks = batch * n_lat // TM
        self.blocks_per_batch = n_lat // TM

    def mod_index(self, blk):
        return jnp.where(blk < self.ctx_blocks, 0, 1 + (blk - self.ctx_blocks) // self.blocks_per_batch)

    def rope_index(self, blk):
        return jnp.where(blk < self.ctx_blocks, 0, 1 + (blk - self.ctx_blocks) % self.blocks_per_batch)

    def query_block(self, b, i, tq):
        cs = N_CTX // tq
        return jnp.where(i < cs, b * cs + i, self.n_ctx_rows // tq + b * (self.n_lat // tq) + i - cs)

    def latent_block(self, b):
        return self.n_ctx_rows // self.n_lat + b


def _mod_kernel(c_ref, w_ref, b_ref, o_ref):
    c = c_ref[...]
    s = (c * jax.nn.sigmoid(c)).astype(BF16)
    o_ref[0] = _dot(s, w_ref[0].astype(BF16)) + b_ref[0]


def _mod_vectors(cond, w_mod, b_mod):
    depth, d, n = w_mod.shape
    rows = cond.shape[0]
    tn = 1536
    return pl.pallas_call(
        _mod_kernel,
        grid=(depth, n // tn),
        in_specs=[pl.BlockSpec((rows, d), lambda l, j: (0, 0)),
                  pl.BlockSpec((1, d, tn), lambda l, j: (l, 0, j)),
                  pl.BlockSpec((1, 1, tn), lambda l, j: (l, 0, j))],
        out_specs=pl.BlockSpec((1, rows, tn), lambda l, j: (l, 0, j)),
        out_shape=jax.ShapeDtypeStruct((depth, rows, n), F32),
        compiler_params=_cparams(2),
        name="mod_vectors",
    )(cond, w_mod, b_mod.reshape(depth, 1, n))


def _values_with_ones_t(v, n_heads):
    vt = v.T
    ones = jnp.ones((V_EXT - HEAD_DIM, vt.shape[1]), F32)
    pieces = []
    for h in range(n_heads):
        pieces += [vt[h * HEAD_DIM:(h + 1) * HEAD_DIM], ones]
    return jnp.concatenate(pieces, axis=0)


def _stream_reader(refs, ctx_blocks):
    if ctx_blocks is None:
        return (lambda rows: refs[0][rows, :]), refs[1:]
    is_ctx = pl.program_id(0) < ctx_blocks
    return (lambda rows: jnp.where(is_ctx, refs[0][rows, :], refs[1][rows, :])), refs[2:]


def _stream_specs(x, rows):
    if not isinstance(x, tuple):
        return (x,), [pl.BlockSpec((TM, D_MODEL), lambda i: (i, 0))], None
    cb = rows.ctx_blocks
    return x, [pl.BlockSpec((TM, D_MODEL), lambda i: (jnp.minimum(i, cb - 1), 0)),
               pl.BlockSpec((TM, D_MODEL), lambda i: (jnp.maximum(i - cb, 0), 0))], cb


def _even_in_kernel(*refs, ctx_blocks):
    read_x, refs = _stream_reader(refs, ctx_blocks)
    (mod_ref, g1_ref, w_ref, down_ref, up_ref, perm_ref, cq_ref, saq_ref, sbq_ref, ck_ref, sak_ref, sbk_ref,
     q_ref, k_ref, v_ref, u_ref) = refs
    m = mod_ref[0]
    nqk = GQA_Q_W + GQA_KV_W
    nv = GQA_KV_W
    nc = EVEN_SUB_ROWS // SSM_CHUNK

    def project(rows):
        a = _modulate(read_x(rows), g1_ref[...], m[:, 0:D_MODEL], m[:, D_MODEL:2 * D_MODEL]).astype(BF16)
        return _dot(a, w_ref[...])

    def finish(r, rows, h):
        qk = h[:, :nqk]
        inv = _head_inv_rms(qk, down_ref[...], up_ref[...], HEAD_DIM)
        q = _rope(qk[:, :GQA_Q_W], cq_ref[rows, :], saq_ref[rows, :], sbq_ref[rows, :], HEAD_DIM // 4)
        k = _rope(qk[:, GQA_Q_W:], ck_ref[rows, :], sak_ref[rows, :], sbk_ref[rows, :], HEAD_DIM // 4)
        q_ref[:, rows] = (q * inv[:, :GQA_Q_W]).T.astype(BF16)
        k_ref[rows, :] = (k * inv[:, GQA_Q_W:]).astype(BF16)
        v_ref[:, rows] = _values_with_ones_t(h[:, nqk:nqk + nv], GQA_KV_HEADS).astype(BF16)
        us = _dot(perm_ref[...], h[:, nqk + nv:].astype(BF16))
        u_ref[r * nc:(r + 1) * nc, :] = jnp.concatenate(
            [us[s * nc:(s + 1) * nc] for s in range(SSM_CHUNK)], axis=1).astype(BF16)

    _issue_ahead(project, finish, EVEN_SUB_ROWS)


def _mod_spec(rows):
    return pl.BlockSpec((1, 1, N_MOD * D_MODEL), lambda i: (rows.mod_index(i), 0, 0))


def _even_in_proj(x, modtab, consts, tabs, rows):
    row_spec = lambda w: pl.BlockSpec((TM, w), lambda i: (i, 0))
    tab_spec = pl.BlockSpec((TM, 128), lambda i: (rows.rope_index(i), 0))
    col_spec = lambda w: pl.BlockSpec((w, TM), lambda i: (0, i))
    v_w = GQA_KV_HEADS * V_EXT
    chunk_rows, chunk_w = TM // SSM_CHUNK, SSM_CHUNK * SSM_WIDTH
    sds = jax.ShapeDtypeStruct
    x_parts, x_specs, ctx_blocks = _stream_specs(x, rows)
    return pl.pallas_call(
        functools.partial(_even_in_kernel, ctx_blocks=ctx_blocks),
        grid=(rows.n_rows // TM,),
        in_specs=x_specs + [_mod_spec(rows)] + [_resident(c.shape) for c in consts] + [tab_spec] * len(tabs),
        out_specs=[col_spec(GQA_Q_W), row_spec(GQA_KV_W), col_spec(v_w),
                   pl.BlockSpec((chunk_rows, chunk_w), lambda i: (i, 0))],
        out_shape=[sds((GQA_Q_W, rows.n_rows), BF16), sds((rows.n_rows, GQA_KV_W), BF16),
                   sds((v_w, rows.n_rows), BF16), sds((rows.n_rows // SSM_CHUNK, chunk_w), BF16)],
        compiler_params=_cparams(1),
        name="even_in_proj",
    )(*x_parts, modtab, *consts, *tabs)


def _odd_in_kernel(x_ref, mod_ref, g1_ref, w_ref, gcq_ref, gckv_ref, wuq_ref, wuk_ref, wuv_ref,
                   gmq_ref, gmk_ref, down_ref, up_ref, gnqk_ref, cos_ref, sa_ref, sb_ref,
                   mq_ref, mk_ref, mv_ref, nq_ref, nk_ref, nv_ref):
    m = mod_ref[0]
    c1 = MLA_Q_RANK
    c2 = c1 + MLA_KV_RANK
    c3 = c2 + MLA_PAD

    def rms(t, g):
        return (t * lax.rsqrt(jnp.mean(t * t, axis=-1, keepdims=True) + EPS) * g).astype(BF16)

    def project(rows):
        a = _modulate(x_ref[rows, :], g1_ref[...], m[:, 0:D_MODEL], m[:, D_MODEL:2 * D_MODEL]).astype(BF16)
        return _dot(a, w_ref[...])

    def finish(r, rows, h):
        q = _dot(rms(h[:, :c1], gcq_ref[...]), wuq_ref[...])
        ckv = rms(h[:, c1:c2], gckv_ref[...])
        k = _dot(ckv, wuk_ref[...]) + jnp.tile(h[:, c2:c3], (1, MLA_HEADS))
        mv_ref[:, rows] = _values_with_ones_t(_dot(ckv, wuv_ref[...]), MLA_HEADS).astype(BF16)

        def mla_heads(t, g):
            parts = []
            for hh in range(MLA_HEADS):
                th = t[:, hh * MLA_PAD:(hh + 1) * MLA_PAD]
                ss = jnp.sum(th * th, axis=-1, keepdims=True)
                parts.append(th * lax.rsqrt(ss * (1.0 / MLA_QK) + EPS))
            t = jnp.concatenate(parts, axis=1) * g
            return _rope(t, cos_ref[rows, :], sa_ref[rows, :], sb_ref[rows, :], MLA_ROPE // 4)

        mq_ref[:, rows] = mla_heads(q, gmq_ref[...]).T.astype(BF16)
        mk_ref[rows, :] = mla_heads(k, gmk_ref[...]).astype(BF16)

        nqk = _head_rms(h[:, c3:c3 + 2 * NA_W], down_ref[...], up_ref[...], HEAD_DIM) * gnqk_ref[...]
        nq_ref[:, rows] = nqk[:, :NA_W].T.astype(BF16)
        nk_ref[rows, :] = nqk[:, NA_W:].astype(BF16)
        nv_ref[:, rows] = _values_with_ones_t(h[:, c3 + 2 * NA_W:], NA_HEADS).astype(BF16)

    _issue_ahead(project, finish, ODD_SUB_ROWS)


def _odd_in_proj(x, modtab, consts, tabs, rows):
    row_spec = lambda w: pl.BlockSpec((TM, w), lambda i: (i, 0))
    tab_spec = pl.BlockSpec((TM, 128), lambda i: (rows.rope_index(i), 0))
    col_spec = lambda w: pl.BlockSpec((w, TM), lambda i: (0, i))
    mla_w = MLA_HEADS * MLA_PAD
    v_w = MLA_HEADS * V_EXT
    rows_out = lambda w: jax.ShapeDtypeStruct((rows.n_rows, w), BF16)
    cols_out = lambda w: jax.ShapeDtypeStruct((w, rows.n_rows), BF16)
    return pl.pallas_call(
        _odd_in_kernel,
        grid=(rows.n_rows // TM,),
        in_specs=[row_spec(D_MODEL), _mod_spec(rows)] + [_resident(c.shape) for c in consts]
                 + [tab_spec] * len(tabs),
        out_specs=[col_spec(mla_w), row_spec(mla_w), col_spec(v_w),
                   col_spec(NA_W), row_spec(NA_W), col_spec(NA_HEADS * V_EXT)],
        out_shape=[cols_out(mla_w), rows_out(mla_w), cols_out(v_w),
                   cols_out(NA_W), rows_out(NA_W), cols_out(NA_HEADS * V_EXT)],
        compiler_params=_cparams(1),
        name="odd_in_proj",
    )(x, modtab, *consts, *tabs)


def _key_max8(s, ways=4):
    n = s.shape[0] // ways
    parts = [jnp.max(s[i * n:(i + 1) * n].reshape(n // 8, 8, s.shape[1]), axis=0) for i in range(ways)]
    while len(parts) > 1:
        parts = [jnp.maximum(a, b) for a, b in zip(parts[::2], parts[1::2])]
    return parts[0]


def _softmax_heads(n_heads, scores, values, o_ref):
    def probabilities(s_parts):
        m = jnp.max(functools.reduce(jnp.maximum, [_key_max8(s) for s in s_parts]), axis=0, keepdims=True)
        return [jnp.exp2(s - m).astype(BF16) for s in s_parts]

    def weighted_values(h, p_parts):
        acc = None
        for p, vt in zip(p_parts, values(h)):
            part = _dot(vt, p)
            acc = part if acc is None else acc + part
        return acc[:HEAD_DIM] / acc[HEAD_DIM:]

    outs = []
    s_cur, p_prev = scores(0), None
    for h in range(n_heads):
        s_next = scores(h + 1) if h + 1 < n_heads else None
        p_cur = probabilities(s_cur)
        if p_prev is not None:
            outs.append(weighted_values(h - 1, p_prev))
        s_cur, p_prev = s_next, p_cur
    outs.append(weighted_values(n_heads - 1, p_prev))
    o_ref[...] = jnp.concatenate(outs, axis=0).T.astype(o_ref.dtype)


def _attn_kernel(qt_ref, kc_ref, kl_ref, vtc_ref, vtl_ref, o_ref, *, n_heads, group, dk, first_step):
    tq = qt_ref.shape[1]
    n_lat = kl_ref.shape[0]
    k_width = kc_ref.shape[1]
    pad_q = k_width <= 256

    def scores(h, with_latent):
        kv = h // group
        qt = qt_ref[h * dk:(h + 1) * dk, :]
        if pad_q:
            pieces = []
            if kv:
                pieces.append(jnp.zeros((kv * dk, tq), qt.dtype))
            pieces.append(qt)
            if k_width - (kv + 1) * dk:
                pieces.append(jnp.zeros((k_width - (kv + 1) * dk, tq), qt.dtype))
            qt = jnp.concatenate(pieces, axis=0)
            ksl = slice(None)
        else:
            ksl = slice(kv * dk, (kv + 1) * dk)
        parts = [_dot(kc_ref[:, ksl], qt)]
        if with_latent:
            parts += [_dot(kl_ref[c * KEY_CHUNK:(c + 1) * KEY_CHUNK, ksl], qt) for c in range(n_lat // KEY_CHUNK)]
        return parts

    def values(h, with_latent):
        vsl = slice((h // group) * V_EXT, (h // group + 1) * V_EXT)
        parts = [vtc_ref[vsl, :]]
        if with_latent:
            parts += [vtl_ref[vsl, c * KEY_CHUNK:(c + 1) * KEY_CHUNK] for c in range(n_lat // KEY_CHUNK)]
        return parts

    def attend(with_latent):
        _softmax_heads(n_heads, lambda h: scores(h, with_latent), lambda h: values(h, with_latent), o_ref)

    if first_step == 0:
        is_ctx = pl.program_id(1) == 0

        @pl.when(is_ctx)
        def _():
            attend(False)

        @pl.when(jnp.logical_not(is_ctx))
        def _():
            attend(True)
    else:
        attend(True)


def _attention(qt, k, vt, rows, *, n_heads, group, dk, skip_ctx_queries):
    first = 1 if skip_ctx_queries else 0
    kern = functools.partial(_attn_kernel, n_heads=n_heads, group=group, dk=dk, first_step=first)
    wo = n_heads * HEAD_DIM
    return pl.pallas_call(
        kern,
        grid=(rows.batch, rows.n_lat // TQ + 1 - first),
        in_specs=[pl.BlockSpec((qt.shape[0], TQ), lambda b, i: (0, rows.query_block(b, i + first, TQ))),
                  pl.BlockSpec((N_CTX, k.shape[1]), lambda b, i: (b, 0)),
                  pl.BlockSpec((rows.n_lat, k.shape[1]), lambda b, i: (rows.latent_block(b), 0)),
                  pl.BlockSpec((vt.shape[0], N_CTX), lambda b, i: (0, b)),
                  pl.BlockSpec((vt.shape[0], rows.n_lat), lambda b, i: (0, rows.latent_block(b)))],
        out_specs=pl.BlockSpec((TQ, wo), lambda b, i: (rows.query_block(b, i + first, TQ), 0)),
        out_shape=jax.ShapeDtypeStruct((rows.n_rows, wo), BF16),
        compiler_params=_cparams(2),
        name="attention_dk%d" % dk,
    )(qt, k, k, vt, vt)


def _na_geometry(grid_rows):
    rb = NA_ROWS_PER_BLOCK
    wr = min(NA_WIN_R, grid_rows)
    assert grid_rows % 2 == 0 and GRID_W * 2 == 128
    span = min(grid_rows, wr + rb - 1 + (wr + rb - 1) % 2)
    starts, variants, keys = [], [], {}
    for r0 in range(0, grid_rows, rb):
        rs0 = int(np.clip(r0 - wr // 2, 0, grid_rows - wr))
        start = min(rs0, grid_rows - span)
        start -= start % 2
        rel = tuple(int(np.clip(r0 + dr - wr // 2, 0, grid_rows - wr)) - (r0 + dr) for dr in range(rb))
        key = (rel, start - r0)
        variants.append(keys.setdefault(key, len(keys)))
        starts.append(start)
    firsts = [variants.index(v) for v in range(len(keys))]
    return wr, span, starts, variants, firsts


def _na_bias_table(rpb, grid_rows):
    rb = NA_ROWS_PER_BLOCK
    wr, span, starts, variants, firsts = _na_geometry(grid_rows)
    nv = len(firsts)
    idx_r = np.zeros((nv, rb, span), np.int32)
    ok_r = np.zeros((nv, rb, span), bool)
    for v, blk in enumerate(firsts):
        r0, start = blk * rb, starts[blk]
        for dr in range(rb):
            r = r0 + dr
            rs = int(np.clip(r - wr // 2, 0, grid_rows - wr))
            for j in range(span):
                kr = start + j
                ok_r[v, dr, j] = rs <= kr < rs + wr
                idx_r[v, dr, j] = np.clip(kr - r + NA_WIN_R - 1, 0, 2 * NA_WIN_R - 2)
    c = np.arange(GRID_W)
    cs = np.clip(c - NA_WIN_C // 2, 0, GRID_W - NA_WIN_C)
    kc = np.arange(GRID_W)
    ok_c = (kc[None, :] >= cs[:, None]) & (kc[None, :] < cs[:, None] + NA_WIN_C)
    idx_c = np.clip(kc[None, :] - c[:, None] + NA_WIN_C - 1, 0, 2 * NA_WIN_C - 2)
    n_rel_c = 2 * NA_WIN_C - 1
    picked = rpb[:, idx_r.reshape(-1)].reshape(rpb.shape[0], nv, rb, span, n_rel_c)
    picked = jnp.transpose(picked, (1, 0, 3, 2, 4)).reshape(nv, rpb.shape[0], span, rb * n_rel_c) * LOG2E
    col_hit = idx_c.T[None, :, :] == np.arange(n_rel_c)[:, None, None]
    spread = (np.eye(rb, dtype=bool)[:, None, None, :, None] & col_hit[None, :, :, None, :])
    spread = jnp.asarray(spread.reshape(rb * n_rel_c, GRID_W, rb * GRID_W), F32)
    bias = jnp.einsum("vhjy,ykq->vhjkq", picked, spread, precision=lax.Precision.HIGHEST)
    ok = ok_r[:, None, :, None, :, None] & ok_c[None, None, None, :, None, :]
    ok = np.transpose(ok, (0, 1, 4, 5, 2, 3)).reshape(nv, 1, span, GRID_W, rb * GRID_W)
    bias = jnp.where(ok, bias, MASK_VALUE)
    return bias.reshape(nv, rpb.shape[0], span * GRID_W, rb * GRID_W)


def _na_kernel(start_ref, var_ref, qt_ref, kc_ref, kl_ref, vtc_ref, vtl_ref, bias_ref, o_ref, *, n_loc, first_step):
    i = pl.program_id(1) + first_step
    tq = qt_ref.shape[1]

    def padded_query(h):
        qt = qt_ref[h * HEAD_DIM:(h + 1) * HEAD_DIM, :]
        zeros = jnp.zeros((HEAD_DIM, tq), qt.dtype)
        return jnp.concatenate([qt, zeros] if h % 2 == 0 else [zeros, qt], axis=0)

    def pair_lanes(h):
        return slice((h // 2) * 2 * HEAD_DIM, (h // 2 + 1) * 2 * HEAD_DIM)

    def value_rows(h):
        return slice(h * V_EXT, (h + 1) * V_EXT)

    def context_queries():
        _softmax_heads(NA_HEADS, lambda h: [_dot(kc_ref[:, pair_lanes(h)], padded_query(h))],
                       lambda h: [vtc_ref[value_rows(h), :]], o_ref)

    def latent_queries():
        off = pl.multiple_of(start_ref[i - 1] * GRID_W, 2 * GRID_W)
        var = var_ref[i - 1]

        def scores(h):
            qt = padded_query(h)
            return [_dot(kc_ref[:, pair_lanes(h)], qt),
                    _dot(kl_ref[pl.ds(off, n_loc), pair_lanes(h)], qt) + bias_ref[var, h]]

        _softmax_heads(NA_HEADS, scores,
                       lambda h: [vtc_ref[value_rows(h), :], vtl_ref[value_rows(h), pl.ds(off, n_loc)]], o_ref)

    if first_step == 0:
        pl.when(i == 0)(context_queries)
        pl.when(i > 0)(latent_queries)
    else:
        latent_queries()


def _neighbourhood_attention(qt, k, vt, bias, rows, skip_ctx_queries):
    grid_rows = rows.n_lat // GRID_W
    _, span, starts, variants, _ = _na_geometry(grid_rows)
    assert NA_ROWS_PER_BLOCK * GRID_W == TQ
    first = 1 if skip_ctx_queries else 0
    kern = functools.partial(_na_kernel, n_loc=span * GRID_W, first_step=first)
    qblock = lambda b, i: rows.query_block(b, i + first, TQ)
    v_w = vt.shape[0]
    grid_spec = pltpu.PrefetchScalarGridSpec(
        num_scalar_prefetch=2,
        grid=(rows.batch, rows.n_lat // TQ + 1 - first),
        in_specs=[pl.BlockSpec((NA_W, TQ), lambda b, i, *_: (0, qblock(b, i))),
                  pl.BlockSpec((N_CTX, NA_W), lambda b, i, *_: (b, 0)),
                  pl.BlockSpec((rows.n_lat, NA_W), lambda b, i, *_: (rows.latent_block(b), 0)),
                  pl.BlockSpec((v_w, N_CTX), lambda b, i, *_: (0, b)),
                  pl.BlockSpec((v_w, rows.n_lat), lambda b, i, *_: (0, rows.latent_block(b))),
                  pl.BlockSpec(bias.shape, lambda b, i, *_: (0, 0, 0, 0), pipeline_mode=pl.Buffered(1))],
        out_specs=pl.BlockSpec((TQ, NA_W), lambda b, i, *_: (qblock(b, i), 0)),
    )
    return pl.pallas_call(
        kern,
        grid_spec=grid_spec,
        out_shape=jax.ShapeDtypeStruct((rows.n_rows, NA_W), BF16),
        compiler_params=_cparams(2),
        name="neighbourhood_attention",
    )(jnp.asarray(starts, jnp.int32), jnp.asarray(variants, jnp.int32), qt, k, k, vt, vt, bias)


def _s5_matrices(lam_re, lam_im, log_dt, b_re, b_im, c_re, c_im, d_skip):
    L, G, P, N = SSM_CHUNK, SSM_GROUPS, SSM_GROUP, SSM_STATE
    hi = lax.Precision.HIGHEST
    dt = jnp.exp(log_dt)[..., None]
    pw = jnp.arange(L + 1, dtype=F32)[:, None, None, None]
    mag = jnp.exp(lam_re * dt * pw)
    e_re = mag * jnp.cos(lam_im * dt * pw)
    e_im = mag * jnp.sin(lam_im * dt * pw)
    a_re, a_im = e_re[1], e_im[1]
    den = jnp.square(lam_re) + jnp.square(lam_im)
    f_re = ((a_re - 1.0) * lam_re + a_im * lam_im) / den
    f_im = (a_im * lam_re - (a_re - 1.0) * lam_im) / den
    bb_re = f_re[..., None] * b_re - f_im[..., None] * b_im
    bb_im = f_re[..., None] * b_im + f_im[..., None] * b_re
    ce_re = c_re[None] * e_re[:, :, :, None, :] - c_im[None] * e_im[:, :, :, None, :]
    ce_im = c_re[None] * e_im[:, :, :, None, :] + c_im[None] * e_re[:, :, :, None, :]
    kk = (jnp.einsum("kdgqn,dgnp->kdgqp", ce_re, bb_re, precision=hi)
          - jnp.einsum("kdgqn,dgnp->kdgqp", ce_im, bb_im, precision=hi))
    lag = np.arange(L)[None, :] - np.arange(L)[:, None]
    kf = kk[np.clip(lag, 0, L - 1), 0]
    kb = kk[np.clip(-lag, 0, L - 1), 1]
    skip = jnp.eye(P, dtype=F32)[None] * d_skip.reshape(G, P)[:, None, :]
    m = lambda cond: jnp.asarray(cond, F32)[:, :, None, None, None]
    kst = m(lag >= 0) * kf + m(lag <= 0) * kb + m(lag == 0) * skip[None, None]
    a_t = jnp.transpose(kst, (0, 2, 4, 1, 3)).reshape(L * G * P, L * P)

    def state_in(pows, d):
        x_re = e_re[pows, d][..., None] * bb_re[d][None] - e_im[pows, d][..., None] * bb_im[d][None]
        x_im = e_re[pows, d][..., None] * bb_im[d][None] + e_im[pows, d][..., None] * bb_re[d][None]
        flat = lambda x: jnp.pad(jnp.transpose(x, (0, 1, 3, 2)).reshape(L * G * P, N), ((0, 0), (0, L * P - N)))
        return [flat(x_re), flat(x_im)]

    a_w1 = jnp.concatenate([a_t] + state_in(np.arange(L)[::-1].copy(), 0) + state_in(np.arange(L), 1), axis=1)
    w1 = _expand_group_blocks(a_w1[None], L * G * P + 4 * G * N, row_shift=4)

    def state_out(x):
        return jnp.transpose(x, (1, 3, 0, 2)).reshape(G * N, L * P)

    pf, pb = np.arange(1, L + 1), np.arange(L, 0, -1)
    a_wc = jnp.stack([jnp.concatenate([state_out(ce_re[pf, 0]), state_out(ce_re[pb, 1])], axis=0),
                      jnp.concatenate([state_out(-ce_im[pf, 0]), state_out(-ce_im[pb, 1])], axis=0)])
    wc = _expand_group_blocks(a_wc, L * G * P, row_shift=6)
    al_re = e_re[L].reshape(2, 1, G * N)
    al_im = e_im[L].reshape(2, 1, G * N)
    return w1[0], wc[0], wc[1], al_re, al_im


def _expand_kernel(a_ref, o_ref, *, row_shift, n_response_tiles):
    j = pl.program_id(1)
    a = a_ref[0].astype(BF16)
    n_rows, tn = o_ref.shape[1], o_ref.shape[2]
    src = lax.broadcasted_iota(jnp.int32, (a.shape[1], tn), 0)
    col = lax.broadcasted_iota(jnp.int32, (a.shape[1], tn), 1)
    row_group = (lax.broadcasted_iota(jnp.int32, (n_rows, tn), 0) >> row_shift) & (SSM_GROUPS - 1)
    out_col = lax.broadcasted_iota(jnp.int32, (n_rows, tn), 1)

    def emit(spread, col_group):
        val = _dot(a, jnp.where(spread, 1.0, 0.0).astype(BF16))
        o_ref[0] = jnp.where(row_group == col_group, val, 0.0).astype(o_ref.dtype)

    @pl.when(j < n_response_tiles)
    def _():
        cg = col + j * tn
        spread = ((src >> 4) == (cg >> 8)) & ((src & 15) == (cg & 15))
        emit(spread, ((out_col + j * tn) >> 4) & (SSM_GROUPS - 1))

    @pl.when(j >= n_response_tiles)
    def _():
        spread = src == (col & (SSM_STATE - 1))
        emit(spread, (out_col >> 6) & (SSM_GROUPS - 1))


def _expand_group_blocks(a, n_cols, *, row_shift):
    k, n_rows, _ = a.shape
    tn = 1024
    n_response_tiles = SSM_CHUNK * SSM_WIDTH // tn
    assert SSM_GROUP == 16 and SSM_STATE == 64 and SSM_CHUNK * SSM_GROUP == 128 and tn % (SSM_GROUPS * SSM_STATE) == 0
    kern = functools.partial(_expand_kernel, row_shift=row_shift, n_response_tiles=n_response_tiles)
    return pl.pallas_call(
        kern,
        grid=(k, n_cols // tn),
        in_specs=[pl.BlockSpec((1, n_rows, 128), lambda d, j: (d, 0, jnp.maximum(j - n_response_tiles + 1, 0)))],
        out_specs=pl.BlockSpec((1, n_rows, tn), lambda d, j: (d, 0, j)),
        out_shape=jax.ShapeDtypeStruct((k, n_rows, n_cols), BF16),
        compiler_params=_cparams(2),
        name="s5_expand",
    )(a)


def _mm_kernel(a_ref, b_ref, o_ref):
    o_ref[...] = _dot(a_ref[...], b_ref[...]).astype(o_ref.dtype)


def _matmul(a, b, tm, tn, out_dtype):
    m, kdim = a.shape
    n = b.shape[1]
    return pl.pallas_call(
        _mm_kernel,
        grid=(n // tn, m // tm),
        in_specs=[pl.BlockSpec((tm, kdim), lambda j, i: (i, 0)),
                  pl.BlockSpec((kdim, tn), lambda j, i: (0, j))],
        out_specs=pl.BlockSpec((tm, tn), lambda j, i: (i, j)),
        out_shape=jax.ShapeDtypeStruct((m, n), out_dtype),
        compiler_params=_cparams(2),
        name="s5_chunk_matmul",
    )(a, b)


def _s5_carry_kernel(sre_ref, sim_ref, are_ref, aim_ref, hre_ref, him_ref, *, ctx_chunks):
    n_chunks = sre_ref.shape[0]
    backward = pl.program_id(0) == 1
    a_re = jnp.broadcast_to(are_ref[0], sre_ref.shape[1:])
    a_im = jnp.broadcast_to(aim_ref[0], sre_ref.shape[1:])

    def step(j, carry):
        h_re, h_im = carry
        cb = jnp.where(j < ctx_chunks, ctx_chunks - 1 - j, n_chunks + ctx_chunks - 1 - j)
        c = jnp.where(backward, cb, j)
        hre_ref[c] = h_re.astype(hre_ref.dtype)
        him_ref[c] = h_im.astype(him_ref.dtype)
        n_re = a_re * h_re - a_im * h_im + sre_ref[c]
        n_im = a_re * h_im + a_im * h_re + sim_ref[c]
        return n_re, n_im

    zero = jnp.zeros(sre_ref.shape[1:], F32)
    lax.fori_loop(0, n_chunks, step, (zero, zero))


def _s5_carry(ys, al_re, al_im, n_chunks, batch, ctx_chunks):
    gn = SSM_GROUPS * SSM_STATE
    tw = 256
    base = SSM_CHUNK * SSM_WIDTH // tw
    per_dir = 2 * gn // tw
    s3 = ys.reshape(n_chunks, batch, ys.shape[1])
    kern = functools.partial(_s5_carry_kernel, ctx_chunks=ctx_chunks)
    blk = (n_chunks, batch, tw)
    return pl.pallas_call(
        kern,
        grid=(2, gn // tw),
        in_specs=[pl.BlockSpec(blk, lambda d, j: (0, 0, base + d * per_dir + j)),
                  pl.BlockSpec(blk, lambda d, j: (0, 0, base + d * per_dir + gn // tw + j)),
                  pl.BlockSpec((1, 1, tw), lambda d, j: (d, 0, j)),
                  pl.BlockSpec((1, 1, tw), lambda d, j: (d, 0, j))],
        out_specs=[pl.BlockSpec(blk, lambda d, j: (0, 0, d * (gn // tw) + j))] * 2,
        out_shape=[jax.ShapeDtypeStruct((n_chunks, batch, 2 * gn), BF16)] * 2,
        compiler_params=_cparams(2),
        name="s5_carry",
    )(s3, s3, al_re, al_im)


def _s5_readout_kernel(hre_ref, him_ref, wre_ref, wim_ref, y_ref, o_ref):
    o_ref[...] = y_ref[...] + _dot(hre_ref[...], wre_ref[...]) + _dot(him_ref[...], wim_ref[...])


def _s5_readout(h_re, h_im, wc_re, wc_im, ys, tm):
    m, kdim = h_re.shape
    n = wc_re.shape[1]
    tn = 1024
    return pl.pallas_call(
        _s5_readout_kernel,
        grid=(n // tn, m // tm),
        in_specs=[pl.BlockSpec((tm, kdim), lambda j, i: (i, 0)),
                  pl.BlockSpec((tm, kdim), lambda j, i: (i, 0)),
                  pl.BlockSpec((kdim, tn), lambda j, i: (0, j)),
                  pl.BlockSpec((kdim, tn), lambda j, i: (0, j)),
                  pl.BlockSpec((tm, tn), lambda j, i: (i, j))],
        out_specs=pl.BlockSpec((tm, tn), lambda j, i: (i, j)),
        out_shape=jax.ShapeDtypeStruct((m, n), F32),
        compiler_params=_cparams(2),
        name="s5_readout",
    )(h_re, h_im, wc_re, wc_im, ys)


def _row_tile(m, target=512):
    t = min(m, target)
    while m % t or t % 16:
        t -= 16
    return t


def _s5_mixer(u, mats, rows):
    w1, wc_re, wc_im, al_re, al_im = mats
    batch = rows.batch
    cw = SSM_CHUNK * SSM_WIDTH
    ctx_chunks = N_CTX // SSM_CHUNK
    n_chunks = ctx_chunks + rows.n_lat // SSM_CHUNK
    n_ctx_chunk_rows = rows.n_ctx_rows // SSM_CHUNK
    uc = jnp.concatenate([u[:n_ctx_chunk_rows].reshape(batch, ctx_chunks, cw),
                          u[n_ctx_chunk_rows:].reshape(batch, n_chunks - ctx_chunks, cw)], axis=1)
    uc = jnp.transpose(uc, (1, 0, 2)).reshape(n_chunks * batch, cw)
    tm = _row_tile(n_chunks * batch)
    ys = _matmul(uc, w1, tm, 1024, F32)
    h_re, h_im = _s5_carry(ys, al_re, al_im, n_chunks, batch, ctx_chunks)
    gn2 = 2 * SSM_GROUPS * SSM_STATE
    y = _s5_readout(h_re.reshape(-1, gn2), h_im.reshape(-1, gn2), wc_re, wc_im, ys, tm)
    y = jnp.transpose(y.reshape(n_chunks, batch, cw), (1, 0, 2))
    return jnp.concatenate([y[:, :ctx_chunks].reshape(n_ctx_chunk_rows, cw),
                            y[:, ctx_chunks:].reshape(-1, cw)], axis=0)


def _gelu_tanh(y):
    return 0.5 * y * (1.0 + jnp.tanh(math.sqrt(2.0 / math.pi) * (y + 0.044715 * (y * y * y))))


def _out_mlp_kernel(*refs, even, ctx_blocks):
    read_x, refs = _stream_reader(refs, ctx_blocks)
    if even:
        m1_ref, m2_ref, mod_ref, perm_ref, wglu_ref, bglu_ref, wo_ref, g2_ref, w1_ref, w2_ref, o_ref = refs
        yc = m2_ref[...]
        ys = jnp.concatenate([yc[:, s * SSM_WIDTH:(s + 1) * SSM_WIDTH] for s in range(SSM_CHUNK)], axis=0)
        hi = ys.astype(BF16)
        lo = (ys - hi.astype(F32)).astype(BF16)
        y = _gelu_tanh(_dot(perm_ref[...], hi) + _dot(perm_ref[...], lo))
        z = _dot(y.astype(BF16), wglu_ref[...]) + bglu_ref[...]
        second = (y * jax.nn.sigmoid(z)).astype(BF16)
    else:
        m1_ref, m2_ref, mod_ref, wo_ref, g2_ref, w1_ref, w2_ref, o_ref = refs
        second = m2_ref[...]
    m = mod_ref[0]
    mod = lambda j: m[:, j * D_MODEL:(j + 1) * D_MODEL]
    mix = jnp.concatenate([m1_ref[...], second], axis=1)
    x1 = read_x(slice(None)) + mod(2) * _dot(mix, wo_ref[...])
    a = _modulate(x1, g2_ref[...], mod(3), mod(4)).astype(BF16)
    acc = None
    ck = 1024
    for c in range(D_FF // ck):
        h = jnp.maximum(_dot(a, w1_ref[:, c * ck:(c + 1) * ck]), 0.0)
        part = _dot((h * h).astype(BF16), w2_ref[c * ck:(c + 1) * ck, :])
        acc = part if acc is None else acc + part
    o_ref[...] = x1 + mod(5) * acc


def _out_mlp(x, mix1, mix2, modtab, consts, rows, even, latent_only):
    first = rows.ctx_blocks if latent_only else 0
    row_spec = lambda w: pl.BlockSpec((TM, w), lambda i: (i + first, 0))
    n_blocks = rows.n_rows // TM - first
    mix2_rows = TM // SSM_CHUNK if even else TM
    if isinstance(x, tuple):
        assert not latent_only
        x_parts, x_specs, ctx_blocks = _stream_specs(x, rows)
    else:
        x_parts, x_specs, ctx_blocks = (x,), [row_spec(D_MODEL)], None
    return pl.pallas_call(
        functools.partial(_out_mlp_kernel, even=even, ctx_blocks=ctx_blocks),
        grid=(n_blocks,),
        in_specs=x_specs + [row_spec(mix1.shape[1]),
                  pl.BlockSpec((mix2_rows, mix2.shape[1]), lambda i: (i + first, 0)),
                  pl.BlockSpec((1, 1, N_MOD * D_MODEL), lambda i: (rows.mod_index(i + first), 0, 0))]
                 + [_resident(c.shape) for c in consts],
        out_specs=pl.BlockSpec((TM, D_MODEL), lambda i: (i, 0)),
        out_shape=jax.ShapeDtypeStruct((n_blocks * TM, D_MODEL), F32),
        compiler_params=_cparams(1),
        name="out_mlp_even" if even else "out_mlp_odd",
    )(*x_parts, mix1, mix2, modtab, *consts)


def _rope_tables(n_lat, rot_dim, head_w, lane_off):
    t = jnp.arange(n_lat)
    grid_r = (t // GRID_W).astype(F32)
    grid_c = (t % GRID_W).astype(F32)
    axis_dim = rot_dim // 2
    freqs = ROPE_BASE ** (-jnp.arange(0, axis_dim, 2, dtype=F32) / axis_dim)
    ang_r = grid_r[:, None] * freqs
    ang_c = grid_c[:, None] * freqs
    ang = jnp.concatenate([ang_r, ang_r, ang_c, ang_c], axis=-1)
    cos, sin = jnp.cos(ang), jnp.sin(ang)
    quarter = rot_dim // 4
    first = (np.arange(rot_dim) // quarter) % 2 == 0
    sin_a = jnp.where(first, -sin, 0.0)
    sin_b = jnp.where(first, 0.0, sin)

    def widen(tab, fill):
        full = jnp.full((n_lat, head_w), fill, F32).at[:, lane_off:lane_off + rot_dim].set(tab)
        full = jnp.concatenate([jnp.full((TM, head_w), fill, F32), full], axis=0)
        return jnp.tile(full, (1, 128 // head_w))

    return widen(cos, 1.0), widen(sin_a, 0.0), widen(sin_b, 0.0)


def _with_gain(tabs, gain, shift):
    cos, sin_a, sin_b = tabs
    return cos * gain, sin_a * jnp.roll(gain, -shift), sin_b * jnp.roll(gain, shift)


def _head_sum_matrices(n_heads, head_dim):
    down = np.kron(np.eye(n_heads), np.ones((head_dim, 1)))
    pad = (-n_heads) % 128
    down = np.pad(down, ((0, 0), (0, pad)))
    return jnp.asarray(down, BF16), jnp.asarray(down.T, BF16)


def _pad_heads(w, n_heads, width, pad_to):
    lead = w.shape[:-1]
    w = w.reshape(lead + (n_heads, width))
    w = jnp.pad(w, [(0, 0)] * len(lead) + [(0, 0), (0, pad_to - width)])
    return w.reshape(lead + (n_heads * pad_to,))


def kernel(x, c, ctx, c_ctx, w_mod, b_mod, g_norm1, g_norm2, w_ff1, w_ff2, e_w_in, e_w_out, e_g_q, e_g_k, ssm_lam_re, ssm_lam_im, ssm_log_dt, ssm_b_re, ssm_b_im, ssm_c_re, ssm_c_im, ssm_d, ssm_w_glu, ssm_b_glu, o_w_in, o_w_out, mla_g_cq, mla_g_ckv, mla_w_uq, mla_w_ukv, mla_g_q, mla_g_k, na_g_q, na_g_k, na_rpb):
    batch, n_lat, d = x.shape
    depth = w_mod.shape[0]
    assert d == D_MODEL and ctx.shape[1] == N_CTX and n_lat % GRID_W == 0
    rows = _Rows(batch, n_lat)

    pad_rows = (-(batch + 1)) % 8
    cond = jnp.concatenate([c_ctx[None], c, jnp.zeros((pad_rows, d), F32)], axis=0)
    modtabs = _mod_vectors(cond, w_mod, b_mod).reshape(depth, -1, 1, N_MOD * d)

    xs = (ctx.reshape(rows.n_ctx_rows, d), x.reshape(batch * n_lat, d))

    even_tabs = _rope_tables(n_lat, HEAD_DIM, HEAD_DIM, 0)
    mla_tabs = _rope_tables(n_lat, MLA_ROPE, MLA_PAD, MLA_NOPE)
    down_e, up_e = _head_sum_matrices(GQA_Q_HEADS + GQA_KV_HEADS, HEAD_DIM)
    down_o, up_o = _head_sum_matrices(2 * NA_HEADS, HEAD_DIM)
    row = lambda v: v.reshape(1, -1)

    def chunk_perm(n):
        tok = np.arange(n)
        mat = np.zeros((n, n), np.float32)
        mat[(tok % SSM_CHUNK) * (n // SSM_CHUNK) + tok // SSM_CHUNK, tok] = 1.0
        return mat

    perm = jnp.asarray(chunk_perm(EVEN_SUB_ROWS), BF16)
    perm_t = jnp.asarray(chunk_perm(TM).T, BF16)

    for i in range(depth):
        j = i // 2
        last = i == depth - 1
        g1, g2 = row(g_norm1[i]), row(g_norm2[i])
        w1, w2 = w_ff1[i].astype(BF16), w_ff2[i].astype(BF16)
        if i % 2 == 0:
            lanes = 128 // HEAD_DIM
            tabs = (_with_gain(even_tabs, jnp.tile(e_g_q[j], lanes) * (HEAD_DIM ** -0.5 * LOG2E), HEAD_DIM // 4)
                    + _with_gain(even_tabs, jnp.tile(e_g_k[j], lanes), HEAD_DIM // 4))
            q, k, v, u = _even_in_proj(xs, modtabs[i], (g1, e_w_in[j].astype(BF16), down_e, up_e, perm), tabs, rows)
            att = _attention(q, k, v, rows, n_heads=GQA_Q_HEADS, group=GQA_GROUP, dk=HEAD_DIM, skip_ctx_queries=last)
            mats = _s5_matrices(ssm_lam_re[j], ssm_lam_im[j], ssm_log_dt[j], ssm_b_re[j], ssm_b_im[j],
                                ssm_c_re[j], ssm_c_im[j], ssm_d[j])
            y = _s5_mixer(u, mats, rows)
            consts = (perm_t, ssm_w_glu[j].astype(BF16), row(ssm_b_glu[j]), e_w_out[j].astype(BF16), g2, w1, w2)
            xs = _out_mlp(xs, att, y, modtabs[i], consts, rows, True, last)
        else:
            w_in = o_w_in[j]
            c1 = MLA_Q_RANK
            c2 = c1 + MLA_KV_RANK
            c3 = c2 + MLA_ROPE
            kr_cols = jnp.pad(w_in[:, c2:c3], ((0, 0), (MLA_NOPE, MLA_PAD - MLA_QK)))
            w_in_p = jnp.concatenate([w_in[:, :c2], kr_cols, w_in[:, c3:]], axis=1).astype(BF16)
            wuq = _pad_heads(mla_w_uq[j], MLA_HEADS, MLA_QK, MLA_PAD).astype(BF16)
            wukv = mla_w_ukv[j].reshape(MLA_KV_RANK, MLA_HEADS, MLA_NOPE + MLA_V)
            wuk = _pad_heads(wukv[:, :, :MLA_NOPE].reshape(MLA_KV_RANK, -1), MLA_HEADS, MLA_NOPE, MLA_PAD).astype(BF16)
            wuv = wukv[:, :, MLA_NOPE:].reshape(MLA_KV_RANK, -1).astype(BF16)
            gmq = _pad_heads(jnp.tile(mla_g_q[j], MLA_HEADS) * (MLA_QK ** -0.5 * LOG2E), MLA_HEADS, MLA_QK, MLA_PAD)
            gmk = _pad_heads(jnp.tile(mla_g_k[j], MLA_HEADS), MLA_HEADS, MLA_QK, MLA_PAD)
            gnqk = jnp.concatenate([jnp.tile(na_g_q[j], NA_HEADS) * (HEAD_DIM ** -0.5 * LOG2E),
                                    jnp.tile(na_g_k[j], NA_HEADS)])
            consts = (g1, w_in_p, row(mla_g_cq[j]), row(mla_g_ckv[j]), wuq, wuk, wuv, row(gmq), row(gmk),
                      down_o, up_o, row(gnqk))
            mq, mk, mv, nq, nk, nv = _odd_in_proj(xs, modtabs[i], consts, mla_tabs, rows)
            mla = _attention(mq, mk, mv, rows, n_heads=MLA_HEADS, group=1, dk=MLA_PAD, skip_ctx_queries=last)
            bias = _na_bias_table(na_rpb[j], n_lat // GRID_W)
            na = _neighbourhood_attention(nq, nk, nv, bias, rows, last)
            consts = (o_w_out[j].astype(BF16), g2, w1, w2)
            xs = _out_mlp(xs, mla, na, modtabs[i], consts, rows, False, last)
    return xs.reshape(batch, n_lat, d)
```

```python
import functools
import math

import numpy as np
import jax
import jax.numpy as jnp
from jax import lax
from jax.experimental import pallas as pl
from jax.experimental.pallas import tpu as pltpu

F32 = jnp.float32
BF16 = jnp.bfloat16

D_MODEL = 1024
GRID_W = 64
HEAD_DIM = 64
ROPE_BASE = 10000.0
EPS = 1e-6
N_MOD = 6
D_FF = 4 * D_MODEL
LOG2E = math.log2(math.e)

GQA_Q_HEADS = 12
GQA_KV_HEADS = 4
GQA_GROUP = GQA_Q_HEADS // GQA_KV_HEADS
GQA_Q_W = GQA_Q_HEADS * HEAD_DIM
GQA_KV_W = GQA_KV_HEADS * HEAD_DIM
SSM_WIDTH = 256
SSM_GROUP = 16
SSM_GROUPS = SSM_WIDTH // SSM_GROUP
SSM_STATE = 64
SSM_CHUNK = 8

MLA_HEADS = 8
MLA_Q_RANK = 512
MLA_KV_RANK = 256
MLA_NOPE = 64
MLA_ROPE = 32
MLA_QK = MLA_NOPE + MLA_ROPE
MLA_V = 64
MLA_PAD = 128
V_EXT = 128
NA_HEADS = 8
NA_W = NA_HEADS * HEAD_DIM
NA_WIN_R = 8
NA_WIN_C = 16
NA_ROWS_PER_BLOCK = 4

N_CTX = 256
TM = 512
EVEN_SUB_ROWS = 256
ODD_SUB_ROWS = 128
TQ = 256
KEY_CHUNK = 1024
MASK_VALUE = -1e30
VMEM_LIMIT = 52 * 1024 * 1024


def _cparams(n_axes):
    return pltpu.CompilerParams(dimension_semantics=("parallel",) * n_axes, vmem_limit_bytes=VMEM_LIMIT)


def _resident(shape):
    nd = len(shape)
    return pl.BlockSpec(shape, lambda *_: (0,) * nd, pipeline_mode=pl.Buffered(1))


def _dot(a, b):
    return jnp.dot(a, b, preferred_element_type=F32)


def _split_dot(x, w):
    hi = x.astype(BF16)
    lo = (x - hi.astype(F32)).astype(BF16)
    return _dot(hi, w) + _dot(lo, w)


def _modulate(xf, g, shift, scale):
    ms = jnp.mean(xf * xf, axis=-1, keepdims=True)
    return (xf * lax.rsqrt(ms + EPS)) * (g * (1.0 + scale)) + shift


def _head_inv_rms(x, down, up, head_dim):
    ss = _split_dot(x * x, down)
    return _split_dot(lax.rsqrt(ss * (1.0 / head_dim) + EPS), up)


def _head_rms(x, down, up, head_dim):
    return x * _head_inv_rms(x, down, up, head_dim)


def _rope(x, cos, sin_a, sin_b, shift):
    w = cos.shape[-1]
    blocks = []
    for j in range(x.shape[-1] // w):
        xb = x[:, j * w:(j + 1) * w]
        blocks.append(xb * cos + pltpu.roll(xb, w - shift, 1) * sin_a + pltpu.roll(xb, shift, 1) * sin_b)
    return jnp.concatenate(blocks, axis=1)


def _issue_ahead(project, finish, sub_rows):
    n = TM // sub_rows
    piece = lambda r: slice(r * sub_rows, (r + 1) * sub_rows)
    nxt = project(piece(0))
    for r in range(n):
        h, nxt = nxt, (project(piece(r + 1)) if r + 1 < n else None)
        finish(r, piece(r), h)


class _Rows:
    def __init__(self, batch, n_lat):
        assert (batch * N_CTX) % TM == 0 and n_lat % TM == 0 and (batch * N_CTX) % n_lat == 0
        self.batch, self.n_lat = batch, n_lat
        self.n_ctx_rows = batch * N_CTX
        self.n_rows = self.n_ctx_rows + batch * n_lat
        self.ctx_blocks = self.n_ctx_rows // TM
        self.blocks_per_batch = n_lat // TM

    def mod_index(self, blk):
        return jnp.where(blk < self.ctx_blocks, 0, 1 + (blk - self.ctx_blocks) // self.blocks_per_batch)

    def rope_index(self, blk):
        return jnp.where(blk < self.ctx_blocks, 0, 1 + (blk - self.ctx_blocks) % self.blocks_per_batch)

    def query_block(self, b, i, tq):
        cs = N_CTX // tq
        return jnp.where(i < cs, b * cs + i, self.n_ctx_rows // tq + b * (self.n_lat // tq) + i - cs)

    def latent_block(self, b):
        return self.n_ctx_rows // self.n_lat + b


def _mod_kernel(c_ref, w_ref, b_ref, o_ref):
    c = c_ref[...]
    s = (c * jax.nn.sigmoid(c)).astype(BF16)
    o_ref[0] = _dot(s, w_ref[0].astype(BF16)) + b_ref[0]


def _mod_vectors(cond, w_mod, b_mod):
    depth, d, n = w_mod.shape
    rows = cond.shape[0]
    tn = 1536
    return pl.pallas_call(
        _mod_kernel,
        grid=(depth, n // tn),
        in_specs=[pl.BlockSpec((rows, d), lambda l, j: (0, 0)),
                  pl.BlockSpec((1, d, tn), lambda l, j: (l, 0, j)),
                  pl.BlockSpec((1, 1, tn), lambda l, j: (l, 0, j))],
        out_specs=pl.BlockSpec((1, rows, tn), lambda l, j: (l, 0, j)),
        out_shape=jax.ShapeDtypeStruct((depth, rows, n), F32),
        compiler_params=_cparams(2),
        name="mod_vectors",
    )(cond, w_mod, b_mod.reshape(depth, 1, n))


def _values_with_ones_t(v, n_heads):
    vt = v.T
    ones = jnp.ones((V_EXT - HEAD_DIM, vt.shape[1]), F32)
    pieces = []
    for h in range(n_heads):
        pieces += [vt[h * HEAD_DIM:(h + 1) * HEAD_DIM], ones]
    return jnp.concatenate(pieces, axis=0)


def _stream_reader(refs, ctx_blocks):
    if ctx_blocks is None:
        return (lambda rows: refs[0][rows, :]), refs[1:]
    is_ctx = pl.program_id(0) < ctx_blocks
    return (lambda rows: jnp.where(is_ctx, refs[0][rows, :], refs[1][rows, :])), refs[2:]


def _stream_specs(x, rows):
    if not isinstance(x, tuple):
        return (x,), [pl.BlockSpec((TM, D_MODEL), lambda i: (i, 0))], None
    cb = rows.ctx_blocks
    return x, [pl.BlockSpec((TM, D_MODEL), lambda i: (jnp.minimum(i, cb - 1), 0)),
               pl.BlockSpec((TM, D_MODEL), lambda i: (jnp.maximum(i - cb, 0), 0))], cb


def _even_in_kernel(*refs, ctx_blocks):
    read_x, refs = _stream_reader(refs, ctx_blocks)
    (mod_ref, g1_ref, w_ref, down_ref, up_ref, perm_ref, cq_ref, saq_ref, sbq_ref, ck_ref, sak_ref, sbk_ref,
     q_ref, k_ref, v_ref, u_ref) = refs
    m = mod_ref[0]
    nqk = GQA_Q_W + GQA_KV_W
    nv = GQA_KV_W
    nc = EVEN_SUB_ROWS // SSM_CHUNK

    def project(rows):
        a = _modulate(read_x(rows), g1_ref[...], m[:, 0:D_MODEL], m[:, D_MODEL:2 * D_MODEL]).astype(BF16)
        return _dot(a, w_ref[...])

    def finish(r, rows, h):
        qk = h[:, :nqk]
        inv = _head_inv_rms(qk, down_ref[...], up_ref[...], HEAD_DIM)
        q = _rope(qk[:, :GQA_Q_W], cq_ref[rows, :], saq_ref[rows, :], sbq_ref[rows, :], HEAD_DIM // 4)
        k = _rope(qk[:, GQA_Q_W:], ck_ref[rows, :], sak_ref[rows, :], sbk_ref[rows, :], HEAD_DIM // 4)
        q_ref[:, rows] = (q * inv[:, :GQA_Q_W]).T.astype(BF16)
        k_ref[rows, :] = (k * inv[:, GQA_Q_W:]).astype(BF16)
        v_ref[:, rows] = _values_with_ones_t(h[:, nqk:nqk + nv], GQA_KV_HEADS).astype(BF16)
        us = _dot(perm_ref[...], h[:, nqk + nv:].astype(BF16))
        u_ref[r * nc:(r + 1) * nc, :] = jnp.concatenate(
            [us[s * nc:(s + 1) * nc] for s in range(SSM_CHUNK)], axis=1).astype(BF16)

    _issue_ahead(project, finish, EVEN_SUB_ROWS)


def _mod_spec(rows):
    return pl.BlockSpec((1, 1, N_MOD * D_MODEL), lambda i: (rows.mod_index(i), 0, 0))


def _even_in_proj(x, modtab, consts, tabs, rows):
    row_spec = lambda w: pl.BlockSpec((TM, w), lambda i: (i, 0))
    tab_spec = pl.BlockSpec((TM, 128), lambda i: (rows.rope_index(i), 0))
    col_spec = lambda w: pl.BlockSpec((w, TM), lambda i: (0, i))
    v_w = GQA_KV_HEADS * V_EXT
    chunk_rows, chunk_w = TM // SSM_CHUNK, SSM_CHUNK * SSM_WIDTH
    sds = jax.ShapeDtypeStruct
    x_parts, x_specs, ctx_blocks = _stream_specs(x, rows)
    return pl.pallas_call(
        functools.partial(_even_in_kernel, ctx_blocks=ctx_blocks),
        grid=(rows.n_rows // TM,),
        in_specs=x_specs + [_mod_spec(rows)] + [_resident(c.shape) for c in consts] + [tab_spec] * len(tabs),
        out_specs=[col_spec(GQA_Q_W), row_spec(GQA_KV_W), col_spec(v_w),
                   pl.BlockSpec((chunk_rows, chunk_w), lambda i: (i, 0))],
        out_shape=[sds((GQA_Q_W, rows.n_rows), BF16), sds((rows.n_rows, GQA_KV_W), BF16),
                   sds((v_w, rows.n_rows), BF16), sds((rows.n_rows // SSM_CHUNK, chunk_w), BF16)],
        compiler_params=_cparams(1),
        name="even_in_proj",
    )(*x_parts, modtab, *consts, *tabs)


def _odd_in_kernel(x_ref, mod_ref, g1_ref, w_ref, gcq_ref, gckv_ref, wuq_ref, wuk_ref, wuv_ref,
                   gmq_ref, gmk_ref, down_ref, up_ref, gnqk_ref, cos_ref, sa_ref, sb_ref,
                   mq_ref, mk_ref, mv_ref, nq_ref, nk_ref, nv_ref):
    m = mod_ref[0]
    c1 = MLA_Q_RANK
    c2 = c1 + MLA_KV_RANK
    c3 = c2 + MLA_PAD

    def rms(t, g):
        return (t * lax.rsqrt(jnp.mean(t * t, axis=-1, keepdims=True) + EPS) * g).astype(BF16)

    def project(rows):
        a = _modulate(x_ref[rows, :], g1_ref[...], m[:, 0:D_MODEL], m[:, D_MODEL:2 * D_MODEL]).astype(BF16)
        return _dot(a, w_ref[...])

    def finish(r, rows, h):
        q = _dot(rms(h[:, :c1], gcq_ref[...]), wuq_ref[...])
        ckv = rms(h[:, c1:c2], gckv_ref[...])
        k = _dot(ckv, wuk_ref[...]) + jnp.tile(h[:, c2:c3], (1, MLA_HEADS))
        mv_ref[:, rows] = _values_with_ones_t(_dot(ckv, wuv_ref[...]), MLA_HEADS).astype(BF16)

        def mla_heads(t, g):
            parts = []
            for hh in range(MLA_HEADS):
                th = t[:, hh * MLA_PAD:(hh + 1) * MLA_PAD]
                ss = jnp.sum(th * th, axis=-1, keepdims=True)
                parts.append(th * lax.rsqrt(ss * (1.0 / MLA_QK) + EPS))
            t = jnp.concatenate(parts, axis=1) * g
            return _rope(t, cos_ref[rows, :], sa_ref[rows, :], sb_ref[rows, :], MLA_ROPE // 4)

        mq_ref[:, rows] = mla_heads(q, gmq_ref[...]).T.astype(BF16)
        mk_ref[rows, :] = mla_heads(k, gmk_ref[...]).astype(BF16)

        nqk = _head_rms(h[:, c3:c3 + 2 * NA_W], down_ref[...], up_ref[...], HEAD_DIM) * gnqk_ref[...]
        nq_ref[:, rows] = nqk[:, :NA_W].T.astype(BF16)
        nk_ref[rows, :] = nqk[:, NA_W:].astype(BF16)
        nv_ref[:, rows] = _values_with_ones_t(h[:, c3 + 2 * NA_W:], NA_HEADS).astype(BF16)

    _issue_ahead(project, finish, ODD_SUB_ROWS)


def _odd_in_proj(x, modtab, consts, tabs, rows):
    row_spec = lambda w: pl.BlockSpec((TM, w), lambda i: (i, 0))
    tab_spec = pl.BlockSpec((TM, 128), lambda i: (rows.rope_index(i), 0))
    col_spec = lambda w: pl.BlockSpec((w, TM), lambda i: (0, i))
    mla_w = MLA_HEADS * MLA_PAD
    v_w = MLA_HEADS * V_EXT
    rows_out = lambda w: jax.ShapeDtypeStruct((rows.n_rows, w), BF16)
    cols_out = lambda w: jax.ShapeDtypeStruct((w, rows.n_rows), BF16)
    return pl.pallas_call(
        _odd_in_kernel,
        grid=(rows.n_rows // TM,),
        in_specs=[row_spec(D_MODEL), _mod_spec(rows)] + [_resident(c.shape) for c in consts]
                 + [tab_spec] * len(tabs),
        out_specs=[col_spec(mla_w), row_spec(mla_w), col_spec(v_w),
                   col_spec(NA_W), row_spec(NA_W), col_spec(NA_HEADS * V_EXT)],
        out_shape=[cols_out(mla_w), rows_out(mla_w), cols_out(v_w),
                   cols_out(NA_W), rows_out(NA_W), cols_out(NA_HEADS * V_EXT)],
        compiler_params=_cparams(1),
        name="odd_in_proj",
    )(x, modtab, *consts, *tabs)


def _key_max8(s, ways=4):
    n = s.shape[0] // ways
    parts = [jnp.max(s[i * n:(i + 1) * n].reshape(n // 8, 8, s.shape[1]), axis=0) for i in range(ways)]
    while len(parts) > 1:
        parts = [jnp.maximum(a, b) for a, b in zip(parts[::2], parts[1::2])]
    return parts[0]


def _softmax_heads(n_heads, scores, values, o_ref):
    def probabilities(s_parts):
        m = jnp.max(functools.reduce(jnp.maximum, [_key_max8(s) for s in s_parts]), axis=0, keepdims=True)
        return [jnp.exp2(s - m).astype(BF16) for s in s_parts]

    def weighted_values(h, p_parts):
        acc = None
        for p, vt in zip(p_parts, values(h)):
            part = _dot(vt, p)
            acc = part if acc is None else acc + part
        return acc[:HEAD_DIM] / acc[HEAD_DIM:]

    outs = []
    s_cur, p_prev = scores(0), None
    for h in range(n_heads):
        s_next = scores(h + 1) if h + 1 < n_heads else None
        p_cur = probabilities(s_cur)
        if p_prev is not None:
            outs.append(weighted_values(h - 1, p_prev))
        s_cur, p_prev = s_next, p_cur
    outs.append(weighted_values(n_heads - 1, p_prev))
    o_ref[...] = jnp.concatenate(outs, axis=0).T.astype(o_ref.dtype)


def _attn_kernel(qt_ref, kc_ref, kl_ref, vtc_ref, vtl_ref, o_ref, *, n_heads, group, dk, first_step):
    tq = qt_ref.shape[1]
    n_lat = kl_ref.shape[0]
    pad_q = dk == HEAD_DIM

    def scores(h, with_latent):
        kv = h // group
        qt = qt_ref[h * dk:(h + 1) * dk, :]
        if pad_q:
            zeros = jnp.zeros((dk, tq), qt.dtype)
            qt = jnp.concatenate([qt, zeros] if kv % 2 == 0 else [zeros, qt], axis=0)
            ksl = slice((kv // 2) * 2 * dk, (kv // 2 + 1) * 2 * dk)
        else:
            ksl = slice(kv * dk, (kv + 1) * dk)
        parts = [_dot(kc_ref[:, ksl], qt)]
        if with_latent:
            parts += [_dot(kl_ref[c * KEY_CHUNK:(c + 1) * KEY_CHUNK, ksl], qt) for c in range(n_lat // KEY_CHUNK)]
        return parts

    def values(h, with_latent):
        vsl = slice((h // group) * V_EXT, (h // group + 1) * V_EXT)
        parts = [vtc_ref[vsl, :]]
        if with_latent:
            parts += [vtl_ref[vsl, c * KEY_CHUNK:(c + 1) * KEY_CHUNK] for c in range(n_lat // KEY_CHUNK)]
        return parts

    def attend(with_latent):
        _softmax_heads(n_heads, lambda h: scores(h, with_latent), lambda h: values(h, with_latent), o_ref)

    if first_step == 0:
        is_ctx = pl.program_id(1) == 0

        @pl.when(is_ctx)
        def _():
            attend(False)

        @pl.when(jnp.logical_not(is_ctx))
        def _():
            attend(True)
    else:
        attend(True)


def _attention(qt, k, vt, rows, *, n_heads, group, dk, skip_ctx_queries):
    first = 1 if skip_ctx_queries else 0
    kern = functools.partial(_attn_kernel, n_heads=n_heads, group=group, dk=dk, first_step=first)
    wo = n_heads * HEAD_DIM
    return pl.pallas_call(
        kern,
        grid=(rows.batch, rows.n_lat // TQ + 1 - first),
        in_specs=[pl.BlockSpec((qt.shape[0], TQ), lambda b, i: (0, rows.query_block(b, i + first, TQ))),
                  pl.BlockSpec((N_CTX, k.shape[1]), lambda b, i: (b, 0)),
                  pl.BlockSpec((rows.n_lat, k.shape[1]), lambda b, i: (rows.latent_block(b), 0)),
                  pl.BlockSpec((vt.shape[0], N_CTX), lambda b, i: (0, b)),
                  pl.BlockSpec((vt.shape[0], rows.n_lat), lambda b, i: (0, rows.latent_block(b)))],
        out_specs=pl.BlockSpec((TQ, wo), lambda b, i: (rows.query_block(b, i + first, TQ), 0)),
        out_shape=jax.ShapeDtypeStruct((rows.n_rows, wo), BF16),
        compiler_params=_cparams(2),
        name="attention_dk%d" % dk,
    )(qt, k, k, vt, vt)


def _na_geometry(grid_rows):
    rb = NA_ROWS_PER_BLOCK
    wr = min(NA_WIN_R, grid_rows)
    assert grid_rows % 2 == 0 and GRID_W * 2 == 128
    span = min(grid_rows, wr + rb - 1 + (wr + rb - 1) % 2)
    starts, variants, keys = [], [], {}
    for r0 in range(0, grid_rows, rb):
        rs0 = int(np.clip(r0 - wr // 2, 0, grid_rows - wr))
        start = min(rs0, grid_rows - span)
        start -= start % 2
        rel = tuple(int(np.clip(r0 + dr - wr // 2, 0, grid_rows - wr)) - (r0 + dr) for dr in range(rb))
        key = (rel, start - r0)
        variants.append(keys.setdefault(key, len(keys)))
        starts.append(start)
    firsts = [variants.index(v) for v in range(len(keys))]
    return wr, span, starts, variants, firsts


def _na_bias_table(rpb, grid_rows):
    rb = NA_ROWS_PER_BLOCK
    wr, span, starts, variants, firsts = _na_geometry(grid_rows)
    nv = len(firsts)
    idx_r = np.zeros((nv, rb, span), np.int32)
    ok_r = np.zeros((nv, rb, span), bool)
    for v, blk in enumerate(firsts):
        r0, start = blk * rb, starts[blk]
        for dr in range(rb):
            r = r0 + dr
            rs = int(np.clip(r - wr // 2, 0, grid_rows - wr))
            for j in range(span):
                kr = start + j
                ok_r[v, dr, j] = rs <= kr < rs + wr
                idx_r[v, dr, j] = np.clip(kr - r + NA_WIN_R - 1, 0, 2 * NA_WIN_R - 2)
    c = np.arange(GRID_W)
    cs = np.clip(c - NA_WIN_C // 2, 0, GRID_W - NA_WIN_C)
    kc = np.arange(GRID_W)
    ok_c = (kc[None, :] >= cs[:, None]) & (kc[None, :] < cs[:, None] + NA_WIN_C)
    idx_c = np.clip(kc[None, :] - c[:, None] + NA_WIN_C - 1, 0, 2 * NA_WIN_C - 2)
    n_rel_c = 2 * NA_WIN_C - 1
    picked = rpb[:, idx_r.reshape(-1)].reshape(rpb.shape[0], nv, rb, span, n_rel_c)
    picked = jnp.transpose(picked, (1, 0, 3, 2, 4)).reshape(nv, rpb.shape[0], span, rb * n_rel_c) * LOG2E
    col_hit = idx_c.T[None, :, :] == np.arange(n_rel_c)[:, None, None]
    spread = (np.eye(rb, dtype=bool)[:, None, None, :, None] & col_hit[None, :, :, None, :])
    spread = jnp.asarray(spread.reshape(rb * n_rel_c, GRID_W, rb * GRID_W), F32)
    bias = jnp.einsum("vhjy,ykq->vhjkq", picked, spread, precision=lax.Precision.HIGHEST)
    ok = ok_r[:, None, :, None, :, None] & ok_c[None, None, None, :, None, :]
    ok = np.transpose(ok, (0, 1, 4, 5, 2, 3)).reshape(nv, 1, span, GRID_W, rb * GRID_W)
    bias = jnp.where(ok, bias, MASK_VALUE)
    return bias.reshape(nv, rpb.shape[0], span * GRID_W, rb * GRID_W)


def _na_kernel(start_ref, var_ref, qt_ref, kc_ref, kl_ref, vtc_ref, vtl_ref, bias_ref, o_ref, *, n_loc, first_step):
    i = pl.program_id(1) + first_step
    tq = qt_ref.shape[1]

    def padded_query(h):
        qt = qt_ref[h * HEAD_DIM:(h + 1) * HEAD_DIM, :]
        zeros = jnp.zeros((HEAD_DIM, tq), qt.dtype)
        return jnp.concatenate([qt, zeros] if h % 2 == 0 else [zeros, qt], axis=0)

    def pair_lanes(h):
        return slice((h // 2) * 2 * HEAD_DIM, (h // 2 + 1) * 2 * HEAD_DIM)

    def value_rows(h):
        return slice(h * V_EXT, (h + 1) * V_EXT)

    def context_queries():
        _softmax_heads(NA_HEADS, lambda h: [_dot(kc_ref[:, pair_lanes(h)], padded_query(h))],
                       lambda h: [vtc_ref[value_rows(h), :]], o_ref)

    def latent_queries():
        off = pl.multiple_of(start_ref[i - 1] * GRID_W, 2 * GRID_W)
        var = var_ref[i - 1]

        def scores(h):
            qt = padded_query(h)
            return [_dot(kc_ref[:, pair_lanes(h)], qt),
                    _dot(kl_ref[pl.ds(off, n_loc), pair_lanes(h)], qt) + bias_ref[var, h]]

        _softmax_heads(NA_HEADS, scores,
                       lambda h: [vtc_ref[value_rows(h), :], vtl_ref[value_rows(h), pl.ds(off, n_loc)]], o_ref)

    if first_step == 0:
        pl.when(i == 0)(context_queries)
        pl.when(i > 0)(latent_queries)
    else:
        latent_queries()


def _neighbourhood_attention(qt, k, vt, bias, rows, skip_ctx_queries):
    grid_rows = rows.n_lat // GRID_W
    _, span, starts, variants, _ = _na_geometry(grid_rows)
    assert NA_ROWS_PER_BLOCK * GRID_W == TQ
    first = 1 if skip_ctx_queries else 0
    kern = functools.partial(_na_kernel, n_loc=span * GRID_W, first_step=first)
    qblock = lambda b, i: rows.query_block(b, i + first, TQ)
    v_w = vt.shape[0]
    grid_spec = pltpu.PrefetchScalarGridSpec(
        num_scalar_prefetch=2,
        grid=(rows.batch, rows.n_lat // TQ + 1 - first),
        in_specs=[pl.BlockSpec((NA_W, TQ), lambda b, i, *_: (0, qblock(b, i))),
                  pl.BlockSpec((N_CTX, NA_W), lambda b, i, *_: (b, 0)),
                  pl.BlockSpec((rows.n_lat, NA_W), lambda b, i, *_: (rows.latent_block(b), 0)),
                  pl.BlockSpec((v_w, N_CTX), lambda b, i, *_: (0, b)),
                  pl.BlockSpec((v_w, rows.n_lat), lambda b, i, *_: (0, rows.latent_block(b))),
                  pl.BlockSpec(bias.shape, lambda b, i, *_: (0, 0, 0, 0), pipeline_mode=pl.Buffered(1))],
        out_specs=pl.BlockSpec((TQ, NA_W), lambda b, i, *_: (qblock(b, i), 0)),
    )
    return pl.pallas_call(
        kern,
        grid_spec=grid_spec,
        out_shape=jax.ShapeDtypeStruct((rows.n_rows, NA_W), BF16),
        compiler_params=_cparams(2),
        name="neighbourhood_attention",
    )(jnp.asarray(starts, jnp.int32), jnp.asarray(variants, jnp.int32), qt, k, k, vt, vt, bias)


def _s5_matrices(lam_re, lam_im, log_dt, b_re, b_im, c_re, c_im, d_skip):
    L, G, P, N = SSM_CHUNK, SSM_GROUPS, SSM_GROUP, SSM_STATE
    hi = lax.Precision.HIGHEST
    dt = jnp.exp(log_dt)[..., None]
    pw = jnp.arange(L + 1, dtype=F32)[:, None, None, None]
    mag = jnp.exp(lam_re * dt * pw)
    e_re = mag * jnp.cos(lam_im * dt * pw)
    e_im = mag * jnp.sin(lam_im * dt * pw)
    a_re, a_im = e_re[1], e_im[1]
    den = jnp.square(lam_re) + jnp.square(lam_im)
    f_re = ((a_re - 1.0) * lam_re + a_im * lam_im) / den
    f_im = (a_im * lam_re - (a_re - 1.0) * lam_im) / den
    bb_re = f_re[..., None] * b_re - f_im[..., None] * b_im
    bb_im = f_re[..., None] * b_im + f_im[..., None] * b_re
    ce_re = c_re[None] * e_re[:, :, :, None, :] - c_im[None] * e_im[:, :, :, None, :]
    ce_im = c_re[None] * e_im[:, :, :, None, :] + c_im[None] * e_re[:, :, :, None, :]
    kk = (jnp.einsum("kdgqn,dgnp->kdgqp", ce_re, bb_re, precision=hi)
          - jnp.einsum("kdgqn,dgnp->kdgqp", ce_im, bb_im, precision=hi))
    lag = np.arange(L)[None, :] - np.arange(L)[:, None]
    kf = kk[np.clip(lag, 0, L - 1), 0]
    kb = kk[np.clip(-lag, 0, L - 1), 1]
    skip = jnp.eye(P, dtype=F32)[None] * d_skip.reshape(G, P)[:, None, :]
    m = lambda cond: jnp.asarray(cond, F32)[:, :, None, None, None]
    kst = m(lag >= 0) * kf + m(lag <= 0) * kb + m(lag == 0) * skip[None, None]
    a_t = jnp.transpose(kst, (0, 2, 4, 1, 3)).reshape(L * G * P, L * P)

    def state_in(pows, d):
        x_re = e_re[pows, d][..., None] * bb_re[d][None] - e_im[pows, d][..., None] * bb_im[d][None]
        x_im = e_re[pows, d][..., None] * bb_im[d][None] + e_im[pows, d][..., None] * bb_re[d][None]
        flat = lambda x: jnp.pad(jnp.transpose(x, (0, 1, 3, 2)).reshape(L * G * P, N), ((0, 0), (0, L * P - N)))
        return [flat(x_re), flat(x_im)]

    a_w1 = jnp.concatenate([a_t] + state_in(np.arange(L)[::-1].copy(), 0) + state_in(np.arange(L), 1), axis=1)
    w1 = _expand_group_blocks(a_w1[None], L * G * P + 4 * G * N, row_shift=4)

    def state_out(x):
        return jnp.transpose(x, (1, 3, 0, 2)).reshape(G * N, L * P)

    pf, pb = np.arange(1, L + 1), np.arange(L, 0, -1)
    a_wc = jnp.stack([jnp.concatenate([state_out(ce_re[pf, 0]), state_out(ce_re[pb, 1])], axis=0),
                      jnp.concatenate([state_out(-ce_im[pf, 0]), state_out(-ce_im[pb, 1])], axis=0)])
    wc = _expand_group_blocks(a_wc, L * G * P, row_shift=6)
    al_re = e_re[L].reshape(2, 1, G * N)
    al_im = e_im[L].reshape(2, 1, G * N)
    return w1[0], wc[0], wc[1], al_re, al_im


def _expand_kernel(a_ref, o_ref, *, row_shift, n_response_tiles):
    j = pl.program_id(1)
    a = a_ref[0].astype(BF16)
    n_rows, tn = o_ref.shape[1], o_ref.shape[2]
    src = lax.broadcasted_iota(jnp.int32, (a.shape[1], tn), 0)
    col = lax.broadcasted_iota(jnp.int32, (a.shape[1], tn), 1)
    row_group = (lax.broadcasted_iota(jnp.int32, (n_rows, tn), 0) >> row_shift) & (SSM_GROUPS - 1)
    out_col = lax.broadcasted_iota(jnp.int32, (n_rows, tn), 1)

    def emit(spread, col_group):
        val = _dot(a, jnp.where(spread, 1.0, 0.0).astype(BF16))
        o_ref[0] = jnp.where(row_group == col_group, val, 0.0).astype(o_ref.dtype)

    @pl.when(j < n_response_tiles)
    def _():
        cg = col + j * tn
        spread = ((src >> 4) == (cg >> 8)) & ((src & 15) == (cg & 15))
        emit(spread, ((out_col + j * tn) >> 4) & (SSM_GROUPS - 1))

    @pl.when(j >= n_response_tiles)
    def _():
        spread = src == (col & (SSM_STATE - 1))
        emit(spread, (out_col >> 6) & (SSM_GROUPS - 1))


def _expand_group_blocks(a, n_cols, *, row_shift):
    k, n_rows, _ = a.shape
    tn = 1024
    n_response_tiles = SSM_CHUNK * SSM_WIDTH // tn
    assert SSM_GROUP == 16 and SSM_STATE == 64 and SSM_CHUNK * SSM_GROUP == 128 and tn % (SSM_GROUPS * SSM_STATE) == 0
    kern = functools.partial(_expand_kernel, row_shift=row_shift, n_response_tiles=n_response_tiles)
    return pl.pallas_call(
        kern,
        grid=(k, n_cols // tn),
        in_specs=[pl.BlockSpec((1, n_rows, 128), lambda d, j: (d, 0, jnp.maximum(j - n_response_tiles + 1, 0)))],
        out_specs=pl.BlockSpec((1, n_rows, tn), lambda d, j: (d, 0, j)),
        out_shape=jax.ShapeDtypeStruct((k, n_rows, n_cols), BF16),
        compiler_params=_cparams(2),
        name="s5_expand",
    )(a)


def _mm_kernel(a_ref, b_ref, o_ref):
    o_ref[...] = _dot(a_ref[...], b_ref[...]).astype(o_ref.dtype)


def _matmul(a, b, tm, tn, out_dtype):
    m, kdim = a.shape
    n = b.shape[1]
    return pl.pallas_call(
        _mm_kernel,
        grid=(n // tn, m // tm),
        in_specs=[pl.BlockSpec((tm, kdim), lambda j, i: (i, 0)),
                  pl.BlockSpec((kdim, tn), lambda j, i: (0, j))],
        out_specs=pl.BlockSpec((tm, tn), lambda j, i: (i, j)),
        out_shape=jax.ShapeDtypeStruct((m, n), out_dtype),
        compiler_params=_cparams(2),
        name="s5_chunk_matmul",
    )(a, b)


def _s5_carry_kernel(sre_ref, sim_ref, are_ref, aim_ref, hre_ref, him_ref, *, ctx_chunks):
    n_chunks = sre_ref.shape[0]
    backward = pl.program_id(0) == 1
    a_re = jnp.broadcast_to(are_ref[0], sre_ref.shape[1:])
    a_im = jnp.broadcast_to(aim_ref[0], sre_ref.shape[1:])

    def step(j, carry):
        h_re, h_im = carry
        cb = jnp.where(j < ctx_chunks, ctx_chunks - 1 - j, n_chunks + ctx_chunks - 1 - j)
        c = jnp.where(backward, cb, j)
        hre_ref[c] = h_re.astype(hre_ref.dtype)
        him_ref[c] = h_im.astype(him_ref.dtype)
        n_re = a_re * h_re - a_im * h_im + sre_ref[c]
        n_im = a_re * h_im + a_im * h_re + sim_ref[c]
        return n_re, n_im

    zero = jnp.zeros(sre_ref.shape[1:], F32)
    lax.fori_loop(0, n_chunks, step, (zero, zero))


def _s5_carry(ys, al_re, al_im, n_chunks, batch, ctx_chunks):
    gn = SSM_GROUPS * SSM_STATE
    tw = 256
    base = SSM_CHUNK * SSM_WIDTH // tw
    per_dir = 2 * gn // tw
    s3 = ys.reshape(n_chunks, batch, ys.shape[1])
    kern = functools.partial(_s5_carry_kernel, ctx_chunks=ctx_chunks)
    blk = (n_chunks, batch, tw)
    return pl.pallas_call(
        kern,
        grid=(2, gn // tw),
        in_specs=[pl.BlockSpec(blk, lambda d, j: (0, 0, base + d * per_dir + j)),
                  pl.BlockSpec(blk, lambda d, j: (0, 0, base + d * per_dir + gn // tw + j)),
                  pl.BlockSpec((1, 1, tw), lambda d, j: (d, 0, j)),
                  pl.BlockSpec((1, 1, tw), lambda d, j: (d, 0, j))],
        out_specs=[pl.BlockSpec(blk, lambda d, j: (0, 0, d * (gn // tw) + j))] * 2,
        out_shape=[jax.ShapeDtypeStruct((n_chunks, batch, 2 * gn), BF16)] * 2,
        compiler_params=_cparams(2),
        name="s5_carry",
    )(s3, s3, al_re, al_im)


def _s5_readout_kernel(hre_ref, him_ref, wre_ref, wim_ref, y_ref, o_ref):
    o_ref[...] = y_ref[...] + _dot(hre_ref[...], wre_ref[...]) + _dot(him_ref[...], wim_ref[...])


def _s5_readout(h_re, h_im, wc_re, wc_im, ys, tm):
    m, kdim = h_re.shape
    n = wc_re.shape[1]
    tn = 1024
    return pl.pallas_call(
        _s5_readout_kernel,
        grid=(n // tn, m // tm),
        in_specs=[pl.BlockSpec((tm, kdim), lambda j, i: (i, 0)),
                  pl.BlockSpec((tm, kdim), lambda j, i: (i, 0)),
                  pl.BlockSpec((kdim, tn), lambda j, i: (0, j)),
                  pl.BlockSpec((kdim, tn), lambda j, i: (0, j)),
                  pl.BlockSpec((tm, tn), lambda j, i: (i, j))],
        out_specs=pl.BlockSpec((tm, tn), lambda j, i: (i, j)),
        out_shape=jax.ShapeDtypeStruct((m, n), F32),
        compiler_params=_cparams(2),
        name="s5_readout",
    )(h_re, h_im, wc_re, wc_im, ys)


def _row_tile(m, target=512):
    t = min(m, target)
    while m % t or t % 16:
        t -= 16
    return t


def _s5_mixer(u, mats, rows):
    w1, wc_re, wc_im, al_re, al_im = mats
    batch = rows.batch
    cw = SSM_CHUNK * SSM_WIDTH
    ctx_chunks = N_CTX // SSM_CHUNK
    n_chunks = ctx_chunks + rows.n_lat // SSM_CHUNK
    n_ctx_chunk_rows = rows.n_ctx_rows // SSM_CHUNK
    uc = jnp.concatenate([u[:n_ctx_chunk_rows].reshape(batch, ctx_chunks, cw),
                          u[n_ctx_chunk_rows:].reshape(batch, n_chunks - ctx_chunks, cw)], axis=1)
    uc = jnp.transpose(uc, (1, 0, 2)).reshape(n_chunks * batch, cw)
    tm = _row_tile(n_chunks * batch)
    ys = _matmul(uc, w1, tm, 1024, F32)
    h_re, h_im = _s5_carry(ys, al_re, al_im, n_chunks, batch, ctx_chunks)
    gn2 = 2 * SSM_GROUPS * SSM_STATE
    y = _s5_readout(h_re.reshape(-1, gn2), h_im.reshape(-1, gn2), wc_re, wc_im, ys, tm)
    y = jnp.transpose(y.reshape(n_chunks, batch, cw), (1, 0, 2))
    return jnp.concatenate([y[:, :ctx_chunks].reshape(n_ctx_chunk_rows, cw),
                            y[:, ctx_chunks:].reshape(-1, cw)], axis=0)


def _gelu_tanh(y):
    return 0.5 * y * (1.0 + jnp.tanh(math.sqrt(2.0 / math.pi) * (y + 0.044715 * (y * y * y))))


def _out_mlp_kernel(*refs, even, ctx_blocks):
    read_x, refs = _stream_reader(refs, ctx_blocks)
    if even:
        m1_ref, m2_ref, mod_ref, perm_ref, wglu_ref, bglu_ref, wo_ref, g2_ref, w1_ref, w2_ref, o_ref = refs
        yc = m2_ref[...]
        ys = jnp.concatenate([yc[:, s * SSM_WIDTH:(s + 1) * SSM_WIDTH] for s in range(SSM_CHUNK)], axis=0)
        hi = ys.astype(BF16)
        lo = (ys - hi.astype(F32)).astype(BF16)
        y = _gelu_tanh(_dot(perm_ref[...], hi) + _dot(perm_ref[...], lo))
        z = _dot(y.astype(BF16), wglu_ref[...]) + bglu_ref[...]
        second = (y * jax.nn.sigmoid(z)).astype(BF16)
    else:
        m1_ref, m2_ref, mod_ref, wo_ref, g2_ref, w1_ref, w2_ref, o_ref = refs
        second = m2_ref[...]
    m = mod_ref[0]
    mod = lambda j: m[:, j * D_MODEL:(j + 1) * D_MODEL]
    mix = jnp.concatenate([m1_ref[...], second], axis=1)
    x1 = read_x(slice(None)) + mod(2) * _dot(mix, wo_ref[...])
    a = _modulate(x1, g2_ref[...], mod(3), mod(4)).astype(BF16)
    acc = None
    ck = 1024
    for c in range(D_FF // ck):
        h = jnp.maximum(_dot(a, w1_ref[:, c * ck:(c + 1) * ck]), 0.0)
        part = _dot((h * h).astype(BF16), w2_ref[c * ck:(c + 1) * ck, :])
        acc = part if acc is None else acc + part
    o_ref[...] = x1 + mod(5) * acc


def _out_mlp(x, mix1, mix2, modtab, consts, rows, even, latent_only):
    first = rows.ctx_blocks if latent_only else 0
    row_spec = lambda w: pl.BlockSpec((TM, w), lambda i: (i + first, 0))
    n_blocks = rows.n_rows // TM - first
    mix2_rows = TM // SSM_CHUNK if even else TM
    if isinstance(x, tuple):
        assert not latent_only
        x_parts, x_specs, ctx_blocks = _stream_specs(x, rows)
    else:
        x_parts, x_specs, ctx_blocks = (x,), [row_spec(D_MODEL)], None
    return pl.pallas_call(
        functools.partial(_out_mlp_kernel, even=even, ctx_blocks=ctx_blocks),
        grid=(n_blocks,),
        in_specs=x_specs + [row_spec(mix1.shape[1]),
                  pl.BlockSpec((mix2_rows, mix2.shape[1]), lambda i: (i + first, 0)),
                  pl.BlockSpec((1, 1, N_MOD * D_MODEL), lambda i: (rows.mod_index(i + first), 0, 0))]
                 + [_resident(c.shape) for c in consts],
        out_specs=pl.BlockSpec((TM, D_MODEL), lambda i: (i, 0)),
        out_shape=jax.ShapeDtypeStruct((n_blocks * TM, D_MODEL), F32),
        compiler_params=_cparams(1),
        name="out_mlp_even" if even else "out_mlp_odd",
    )(*x_parts, mix1, mix2, modtab, *consts)


def _rope_tables(n_lat, rot_dim, head_w, lane_off):
    t = jnp.arange(n_lat)
    grid_r = (t // GRID_W).astype(F32)
    grid_c = (t % GRID_W).astype(F32)
    axis_dim = rot_dim // 2
    freqs = ROPE_BASE ** (-jnp.arange(0, axis_dim, 2, dtype=F32) / axis_dim)
    ang_r = grid_r[:, None] * freqs
    ang_c = grid_c[:, None] * freqs
    ang = jnp.concatenate([ang_r, ang_r, ang_c, ang_c], axis=-1)
    cos, sin = jnp.cos(ang), jnp.sin(ang)
    quarter = rot_dim // 4
    first = (np.arange(rot_dim) // quarter) % 2 == 0
    sin_a = jnp.where(first, -sin, 0.0)
    sin_b = jnp.where(first, 0.0, sin)

    def widen(tab, fill):
        full = jnp.full((n_lat, head_w), fill, F32).at[:, lane_off:lane_off + rot_dim].set(tab)
        full = jnp.concatenate([jnp.full((TM, head_w), fill, F32), full], axis=0)
        return jnp.tile(full, (1, 128 // head_w))

    return widen(cos, 1.0), widen(sin_a, 0.0), widen(sin_b, 0.0)


def _with_gain(tabs, gain, shift):
    cos, sin_a, sin_b = tabs
    return cos * gain, sin_a * jnp.roll(gain, -shift), sin_b * jnp.roll(gain, shift)


def _head_sum_matrices(n_heads, head_dim):
    down = np.kron(np.eye(n_heads), np.ones((head_dim, 1)))
    pad = (-n_heads) % 128
    down = np.pad(down, ((0, 0), (0, pad)))
    return jnp.asarray(down, BF16), jnp.asarray(down.T, BF16)


def _pad_heads(w, n_heads, width, pad_to):
    lead = w.shape[:-1]
    w = w.reshape(lead + (n_heads, width))
    w = jnp.pad(w, [(0, 0)] * len(lead) + [(0, 0), (0, pad_to - width)])
    return w.reshape(lead + (n_heads * pad_to,))


def kernel(x, c, ctx, c_ctx, w_mod, b_mod, g_norm1, g_norm2, w_ff1, w_ff2, e_w_in, e_w_out, e_g_q, e_g_k, ssm_lam_re, ssm_lam_im, ssm_log_dt, ssm_b_re, ssm_b_im, ssm_c_re, ssm_c_im, ssm_d, ssm_w_glu, ssm_b_glu, o_w_in, o_w_out, mla_g_cq, mla_g_ckv, mla_w_uq, mla_w_ukv, mla_g_q, mla_g_k, na_g_q, na_g_k, na_rpb):
    batch, n_lat, d = x.shape
    depth = w_mod.shape[0]
    assert d == D_MODEL and ctx.shape[1] == N_CTX and n_lat % GRID_W == 0
    rows = _Rows(batch, n_lat)

    pad_rows = (-(batch + 1)) % 8
    cond = jnp.concatenate([c_ctx[None], c, jnp.zeros((pad_rows, d), F32)], axis=0)
    modtabs = _mod_vectors(cond, w_mod, b_mod).reshape(depth, -1, 1, N_MOD * d)

    xs = (ctx.reshape(rows.n_ctx_rows, d), x.reshape(batch * n_lat, d))

    even_tabs = _rope_tables(n_lat, HEAD_DIM, HEAD_DIM, 0)
    mla_tabs = _rope_tables(n_lat, MLA_ROPE, MLA_PAD, MLA_NOPE)
    down_e, up_e = _head_sum_matrices(GQA_Q_HEADS + GQA_KV_HEADS, HEAD_DIM)
    down_o, up_o = _head_sum_matrices(2 * NA_HEADS, HEAD_DIM)
    row = lambda v: v.reshape(1, -1)

    def chunk_perm(n):
        tok = np.arange(n)
        mat = np.zeros((n, n), np.float32)
        mat[(tok % SSM_CHUNK) * (n // SSM_CHUNK) + tok // SSM_CHUNK, tok] = 1.0
        return mat

    perm = jnp.asarray(chunk_perm(EVEN_SUB_ROWS), BF16)
    perm_t = jnp.asarray(chunk_perm(TM).T, BF16)

    for i in range(depth):
        j = i // 2
        last = i == depth - 1
        g1, g2 = row(g_norm1[i]), row(g_norm2[i])
        w1, w2 = w_ff1[i].astype(BF16), w_ff2[i].astype(BF16)
        if i % 2 == 0:
            lanes = 128 // HEAD_DIM
            tabs = (_with_gain(even_tabs, jnp.tile(e_g_q[j], lanes) * (HEAD_DIM ** -0.5 * LOG2E), HEAD_DIM // 4)
                    + _with_gain(even_tabs, jnp.tile(e_g_k[j], lanes), HEAD_DIM // 4))
            q, k, v, u = _even_in_proj(xs, modtabs[i], (g1, e_w_in[j].astype(BF16), down_e, up_e, perm), tabs, rows)
            att = _attention(q, k, v, rows, n_heads=GQA_Q_HEADS, group=GQA_GROUP, dk=HEAD_DIM, skip_ctx_queries=last)
            mats = _s5_matrices(ssm_lam_re[j], ssm_lam_im[j], ssm_log_dt[j], ssm_b_re[j], ssm_b_im[j],
                                ssm_c_re[j], ssm_c_im[j], ssm_d[j])
            y = _s5_mixer(u, mats, rows)
            consts = (perm_t, ssm_w_glu[j].astype(BF16), row(ssm_b_glu[j]), e_w_out[j].astype(BF16), g2, w1, w2)
            xs = _out_mlp(xs, att, y, modtabs[i], consts, rows, True, last)
        else:
            w_in = o_w_in[j]
            c1 = MLA_Q_RANK
            c2 = c1 + MLA_KV_RANK
            c3 = c2 + MLA_ROPE
            kr_cols = jnp.pad(w_in[:, c2:c3], ((0, 0), (MLA_NOPE, MLA_PAD - MLA_QK)))
            w_in_p = jnp.concatenate([w_in[:, :c2], kr_cols, w_in[:, c3:]], axis=1).astype(BF16)
            wuq = _pad_heads(mla_w_uq[j], MLA_HEADS, MLA_QK, MLA_PAD).astype(BF16)
            wukv = mla_w_ukv[j].reshape(MLA_KV_RANK, MLA_HEADS, MLA_NOPE + MLA_V)
            wuk = _pad_heads(wukv[:, :, :MLA_NOPE].reshape(MLA_KV_RANK, -1), MLA_HEADS, MLA_NOPE, MLA_PAD).astype(BF16)
            wuv = wukv[:, :, MLA_NOPE:].reshape(MLA_KV_RANK, -1).astype(BF16)
            gmq = _pad_heads(jnp.tile(mla_g_q[j], MLA_HEADS) * (MLA_QK ** -0.5 * LOG2E), MLA_HEADS, MLA_QK, MLA_PAD)
            gmk = _pad_heads(jnp.tile(mla_g_k[j], MLA_HEADS), MLA_HEADS, MLA_QK, MLA_PAD)
            gnqk = jnp.concatenate([jnp.tile(na_g_q[j], NA_HEADS) * (HEAD_DIM ** -0.5 * LOG2E),
                                    jnp.tile(na_g_k[j], NA_HEADS)])
            consts = (g1, w_in_p, row(mla_g_cq[j]), row(mla_g_ckv[j]), wuq, wuk, wuv, row(gmq), row(gmk),
                      down_o, up_o, row(gnqk))
            mq, mk, mv, nq, nk, nv = _odd_in_proj(xs, modtabs[i], consts, mla_tabs, rows)
            mla = _attention(mq, mk, mv, rows, n_heads=MLA_HEADS, group=1, dk=MLA_PAD, skip_ctx_queries=last)
            bias = _na_bias_table(na_rpb[j], n_lat // GRID_W)
            na = _neighbourhood_attention(nq, nk, nv, bias, rows, last)
            consts = (o_w_out[j].astype(BF16), g2, w1, w2)
            xs = _out_mlp(xs, mla, na, modtabs[i], consts, rows, False, last)
    return xs.reshape(batch, n_lat, d)
```

```python
import functools
import math

import numpy as np
import jax
import jax.numpy as jnp
from jax import lax
from jax.experimental import pallas as pl
from jax.experimental.pallas import tpu as pltpu

F32 = jnp.float32
BF16 = jnp.bfloat16

D_MODEL = 1024
GRID_W = 64
HEAD_DIM = 64
ROPE_BASE = 10000.0
EPS = 1e-6
N_MOD = 6
D_FF = 4 * D_MODEL
LOG2E = math.log2(math.e)

GQA_Q_HEADS = 12
GQA_KV_HEADS = 4
GQA_GROUP = GQA_Q_HEADS // GQA_KV_HEADS
GQA_Q_W = GQA_Q_HEADS * HEAD_DIM
GQA_KV_W = GQA_KV_HEADS * HEAD_DIM
SSM_WIDTH = 256
SSM_GROUP = 16
SSM_GROUPS = SSM_WIDTH // SSM_GROUP
SSM_STATE = 64
SSM_CHUNK = 8

MLA_HEADS = 8
MLA_Q_RANK = 512
MLA_KV_RANK = 256
MLA_NOPE = 64
MLA_ROPE = 32
MLA_QK = MLA_NOPE + MLA_ROPE
MLA_V = 64
MLA_PAD = 128
V_EXT = 128
NA_HEADS = 8
NA_W = NA_HEADS * HEAD_DIM
NA_WIN_R = 8
NA_WIN_C = 16
NA_ROWS_PER_BLOCK = 4

N_CTX = 256
TM = 512
EVEN_SUB_ROWS = 256
ODD_SUB_ROWS = 128
TQ = 256
KEY_CHUNK = 1024
MASK_VALUE = -1e30
VMEM_LIMIT = 52 * 1024 * 1024


def _cparams(n_axes):
    return pltpu.CompilerParams(dimension_semantics=("parallel",) * n_axes, vmem_limit_bytes=VMEM_LIMIT)


def _resident(shape):
    nd = len(shape)
    return pl.BlockSpec(shape, lambda *_: (0,) * nd, pipeline_mode=pl.Buffered(1))


def _dot(a, b):
    return jnp.dot(a, b, preferred_element_type=F32)


def _split_dot(x, w):
    hi = x.astype(BF16)
    lo = (x - hi.astype(F32)).astype(BF16)
    return _dot(hi, w) + _dot(lo, w)


def _modulate(xf, g, shift, scale):
    ms = jnp.mean(xf * xf, axis=-1, keepdims=True)
    return (xf * lax.rsqrt(ms + EPS)) * (g * (1.0 + scale)) + shift


def _head_inv_rms(x, down, up, head_dim):
    ss = _split_dot(x * x, down)
    return _split_dot(lax.rsqrt(ss * (1.0 / head_dim) + EPS), up)


def _head_rms(x, down, up, head_dim):
    return x * _head_inv_rms(x, down, up, head_dim)


def _rope(x, cos, sin_a, sin_b, shift):
    w = cos.shape[-1]
    blocks = []
    for j in range(x.shape[-1] // w):
        xb = x[:, j * w:(j + 1) * w]
        blocks.append(xb * cos + pltpu.roll(xb, w - shift, 1) * sin_a + pltpu.roll(xb, shift, 1) * sin_b)
    return jnp.concatenate(blocks, axis=1)


def _issue_ahead(project, finish, sub_rows):
    n = TM // sub_rows
    piece = lambda r: slice(r * sub_rows, (r + 1) * sub_rows)
    nxt = project(piece(0))
    for r in range(n):
        h, nxt = nxt, (project(piece(r + 1)) if r + 1 < n else None)
        finish(r, piece(r), h)


class _Rows:
    def __init__(self, batch, n_lat):
        assert (batch * N_CTX) % TM == 0 and n_lat % TM == 0 and (batch * N_CTX) % n_lat == 0
        self.batch, self.n_lat = batch, n_lat
        self.n_ctx_rows = batch * N_CTX
        self.n_rows = self.n_ctx_rows + batch * n_lat
        self.ctx_blocks = self.n_ctx_rows // TM
        self.blocks_per_batch = n_lat // TM

    def mod_index(self, blk):
        return jnp.where(blk < self.ctx_blocks, 0, 1 + (blk - self.ctx_blocks) // self.blocks_per_batch)

    def rope_index(self, blk):
        return jnp.where(blk < self.ctx_blocks, 0, 1 + (blk - self.ctx_blocks) % self.blocks_per_batch)

    def query_block(self, b, i, tq):
        cs = N_CTX // tq
        return jnp.where(i < cs, b * cs + i, self.n_ctx_rows // tq + b * (self.n_lat // tq) + i - cs)

    def latent_block(self, b):
        return self.n_ctx_rows // self.n_lat + b


def _mod_kernel(c_ref, w_ref, b_ref, o_ref):
    c = c_ref[...]
    s = (c * jax.nn.sigmoid(c)).astype(BF16)
    o_ref[0] = _dot(s, w_ref[0].astype(BF16)) + b_ref[0]


def _mod_vectors(cond, w_mod, b_mod):
    depth, d, n = w_mod.shape
    rows = cond.shape[0]
    tn = 1536
    return pl.pallas_call(
        _mod_kernel,
        grid=(depth, n // tn),
        in_specs=[pl.BlockSpec((rows, d), lambda l, j: (0, 0)),
                  pl.BlockSpec((1, d, tn), lambda l, j: (l, 0, j)),
                  pl.BlockSpec((1, 1, tn), lambda l, j: (l, 0, j))],
        out_specs=pl.BlockSpec((1, rows, tn), lambda l, j: (l, 0, j)),
        out_shape=jax.ShapeDtypeStruct((depth, rows, n), F32),
        compiler_params=_cparams(2),
        name="mod_vectors",
    )(cond, w_mod, b_mod.reshape(depth, 1, n))


def _values_with_ones_t(v, n_heads):
    vt = v.T
    ones = jnp.ones((V_EXT - HEAD_DIM, vt.shape[1]), F32)
    pieces = []
    for h in range(n_heads):
        pieces += [vt[h * HEAD_DIM:(h + 1) * HEAD_DIM], ones]
    return jnp.concatenate(pieces, axis=0)


def _stream_reader(refs, ctx_blocks):
    if ctx_blocks is None:
        return (lambda rows: refs[0][rows, :]), refs[1:]
    is_ctx = pl.program_id(0) < ctx_blocks
    return (lambda rows: jnp.where(is_ctx, refs[0][rows, :], refs[1][rows, :])), refs[2:]


def _stream_specs(x, rows):
    if not isinstance(x, tuple):
        return (x,), [pl.BlockSpec((TM, D_MODEL), lambda i: (i, 0))], None
    cb = rows.ctx_blocks
    return x, [pl.BlockSpec((TM, D_MODEL), lambda i: (jnp.minimum(i, cb - 1), 0)),
               pl.BlockSpec((TM, D_MODEL), lambda i: (jnp.maximum(i - cb, 0), 0))], cb


def _even_in_kernel(*refs, ctx_blocks):
    read_x, refs = _stream_reader(refs, ctx_blocks)
    (mod_ref, g1_ref, w_ref, down_ref, up_ref, perm_ref, cq_ref, saq_ref, sbq_ref, ck_ref, sak_ref, sbk_ref,
     q_ref, k_ref, v_ref, u_ref) = refs
    m = mod_ref[0]
    nqk = GQA_Q_W + GQA_KV_W
    nv = GQA_KV_W
    nc = EVEN_SUB_ROWS // SSM_CHUNK

    def project(rows):
        a = _modulate(read_x(rows), g1_ref[...], m[:, 0:D_MODEL], m[:, D_MODEL:2 * D_MODEL]).astype(BF16)
        return _dot(a, w_ref[...])

    def finish(r, rows, h):
        qk = h[:, :nqk]
        inv = _head_inv_rms(qk, down_ref[...], up_ref[...], HEAD_DIM)
        q = _rope(qk[:, :GQA_Q_W], cq_ref[rows, :], saq_ref[rows, :], sbq_ref[rows, :], HEAD_DIM // 4)
        k = _rope(qk[:, GQA_Q_W:], ck_ref[rows, :], sak_ref[rows, :], sbk_ref[rows, :], HEAD_DIM // 4)
        q_ref[:, rows] = (q * inv[:, :GQA_Q_W]).T.astype(BF16)
        k_ref[rows, :] = (k * inv[:, GQA_Q_W:]).astype(BF16)
        v_ref[:, rows] = _values_with_ones_t(h[:, nqk:nqk + nv], GQA_KV_HEADS).astype(BF16)
        us = _dot(perm_ref[...], h[:, nqk + nv:].astype(BF16))
        u_ref[r * nc:(r + 1) * nc, :] = jnp.concatenate(
            [us[s * nc:(s + 1) * nc] for s in range(SSM_CHUNK)], axis=1).astype(BF16)

    _issue_ahead(project, finish, EVEN_SUB_ROWS)


def _mod_spec(rows):
    return pl.BlockSpec((1, 1, N_MOD * D_MODEL), lambda i: (rows.mod_index(i), 0, 0))


def _even_in_proj(x, modtab, consts, tabs, rows):
    row_spec = lambda w: pl.BlockSpec((TM, w), lambda i: (i, 0))
    tab_spec = pl.BlockSpec((TM, 128), lambda i: (rows.rope_index(i), 0))
    col_spec = lambda w: pl.BlockSpec((w, TM), lambda i: (0, i))
    v_w = GQA_KV_HEADS * V_EXT
    chunk_rows, chunk_w = TM // SSM_CHUNK, SSM_CHUNK * SSM_WIDTH
    sds = jax.ShapeDtypeStruct
    x_parts, x_specs, ctx_blocks = _stream_specs(x, rows)
    return pl.pallas_call(
        functools.partial(_even_in_kernel, ctx_blocks=ctx_blocks),
        grid=(rows.n_rows // TM,),
        in_specs=x_specs + [_mod_spec(rows)] + [_resident(c.shape) for c in consts] + [tab_spec] * len(tabs),
        out_specs=[col_spec(GQA_Q_W), row_spec(GQA_KV_W), col_spec(v_w),
                   pl.BlockSpec((chunk_rows, chunk_w), lambda i: (i, 0))],
        out_shape=[sds((GQA_Q_W, rows.n_rows), BF16), sds((rows.n_rows, GQA_KV_W), BF16),
                   sds((v_w, rows.n_rows), BF16), sds((rows.n_rows // SSM_CHUNK, chunk_w), BF16)],
        compiler_params=_cparams(1),
        name="even_in_proj",
    )(*x_parts, modtab, *consts, *tabs)


def _odd_in_kernel(x_ref, mod_ref, g1_ref, w_ref, gcq_ref, gckv_ref, wuq_ref, wuk_ref, wuv_ref,
                   gmq_ref, gmk_ref, down_ref, up_ref, gnqk_ref, cos_ref, sa_ref, sb_ref,
                   mq_ref, mk_ref, mv_ref, nq_ref, nk_ref, nv_ref):
    m = mod_ref[0]
    c1 = MLA_Q_RANK
    c2 = c1 + MLA_KV_RANK
    c3 = c2 + MLA_PAD

    def rms(t, g):
        return (t * lax.rsqrt(jnp.mean(t * t, axis=-1, keepdims=True) + EPS) * g).astype(BF16)

    def project(rows):
        a = _modulate(x_ref[rows, :], g1_ref[...], m[:, 0:D_MODEL], m[:, D_MODEL:2 * D_MODEL]).astype(BF16)
        return _dot(a, w_ref[...])

    def finish(r, rows, h):
        q = _dot(rms(h[:, :c1], gcq_ref[...]), wuq_ref[...])
        ckv = rms(h[:, c1:c2], gckv_ref[...])
        k = _dot(ckv, wuk_ref[...]) + jnp.tile(h[:, c2:c3], (1, MLA_HEADS))
        mv_ref[:, rows] = _values_with_ones_t(_dot(ckv, wuv_ref[...]), MLA_HEADS).astype(BF16)

        def mla_heads(t, g):
            parts = []
            for hh in range(MLA_HEADS):
                th = t[:, hh * MLA_PAD:(hh + 1) * MLA_PAD]
                ss = jnp.sum(th * th, axis=-1, keepdims=True)
                parts.append(th * lax.rsqrt(ss * (1.0 / MLA_QK) + EPS))
            t = jnp.concatenate(parts, axis=1) * g
            return _rope(t, cos_ref[rows, :], sa_ref[rows, :], sb_ref[rows, :], MLA_ROPE // 4)

        mq_ref[:, rows] = mla_heads(q, gmq_ref[...]).T.astype(BF16)
        mk_ref[rows, :] = mla_heads(k, gmk_ref[...]).astype(BF16)

        nqk = _head_rms(h[:, c3:c3 + 2 * NA_W], down_ref[...], up_ref[...], HEAD_DIM) * gnqk_ref[...]
        nq_ref[:, rows] = nqk[:, :NA_W].T.astype(BF16)
        nk_ref[rows, :] = nqk[:, NA_W:].astype(BF16)
        nv_ref[:, rows] = _values_with_ones_t(h[:, c3 + 2 * NA_W:], NA_HEADS).astype(BF16)

    _issue_ahead(project, finish, ODD_SUB_ROWS)


def _odd_in_proj(x, modtab, consts, tabs, rows):
    row_spec = lambda w: pl.BlockSpec((TM, w), lambda i: (i, 0))
    tab_spec = pl.BlockSpec((TM, 128), lambda i: (rows.rope_index(i), 0))
    col_spec = lambda w: pl.BlockSpec((w, TM), lambda i: (0, i))
    mla_w = MLA_HEADS * MLA_PAD
    v_w = MLA_HEADS * V_EXT
    rows_out = lambda w: jax.ShapeDtypeStruct((rows.n_rows, w), BF16)
    cols_out = lambda w: jax.ShapeDtypeStruct((w, rows.n_rows), BF16)
    return pl.pallas_call(
        _odd_in_kernel,
        grid=(rows.n_rows // TM,),
        in_specs=[row_spec(D_MODEL), _mod_spec(rows)] + [_resident(c.shape) for c in consts]
                 + [tab_spec] * len(tabs),
        out_specs=[col_spec(mla_w), row_spec(mla_w), col_spec(v_w),
                   col_spec(NA_W), row_spec(NA_W), col_spec(NA_HEADS * V_EXT)],
        out_shape=[cols_out(mla_w), rows_out(mla_w), cols_out(v_w),
                   cols_out(NA_W), rows_out(NA_W), cols_out(NA_HEADS * V_EXT)],
        compiler_params=_cparams(1),
        name="odd_in_proj",
    )(x, modtab, *consts, *tabs)


def _key_max8(s, ways=4):
    n = s.shape[0] // ways
    parts = [jnp.max(s[i * n:(i + 1) * n].reshape(n // 8, 8, s.shape[1]), axis=0) for i in range(ways)]
    while len(parts) > 1:
        parts = [jnp.maximum(a, b) for a, b in zip(parts[::2], parts[1::2])]
    return parts[0]


def _softmax_heads(n_heads, scores, values, o_ref):
    def probabilities(s_parts):
        m = jnp.max(functools.reduce(jnp.maximum, [_key_max8(s) for s in s_parts]), axis=0, keepdims=True)
        return [jnp.exp2(s - m).astype(BF16) for s in s_parts]

    def weighted_values(h, p_parts):
        acc = None
        for p, vt in zip(p_parts, values(h)):
            part = _dot(vt, p)
            acc = part if acc is None else acc + part
        return acc[:HEAD_DIM] / acc[HEAD_DIM:]

    outs = []
    s_cur, p_prev = scores(0), None
    for h in range(n_heads):
        s_next = scores(h + 1) if h + 1 < n_heads else None
        p_cur = probabilities(s_cur)
        if p_prev is not None:
            outs.append(weighted_values(h - 1, p_prev))
        s_cur, p_prev = s_next, p_cur
    outs.append(weighted_values(n_heads - 1, p_prev))
    o_ref[...] = jnp.concatenate(outs, axis=0).T.astype(o_ref.dtype)


def _attn_kernel(qt_ref, kc_ref, kl_ref, vtc_ref, vtl_ref, o_ref, *, n_heads, group, dk, first_step):
    tq = qt_ref.shape[1]
    n_lat = kl_ref.shape[0]
    pad_q = dk == HEAD_DIM

    def scores(h, with_latent):
        kv = h // group
        qt = qt_ref[h * dk:(h + 1) * dk, :]
        if pad_q:
            zeros = jnp.zeros((dk, tq), qt.dtype)
            qt = jnp.concatenate([qt, zeros] if kv % 2 == 0 else [zeros, qt], axis=0)
            ksl = slice((kv // 2) * 2 * dk, (kv // 2 + 1) * 2 * dk)
        else:
            ksl = slice(kv * dk, (kv + 1) * dk)
        parts = [_dot(kc_ref[:, ksl], qt)]
        if with_latent:
            parts += [_dot(kl_ref[c * KEY_CHUNK:(c + 1) * KEY_CHUNK, ksl], qt) for c in range(n_lat // KEY_CHUNK)]
        return parts

    def values(h, with_latent):
        vsl = slice((h // group) * V_EXT, (h // group + 1) * V_EXT)
        parts = [vtc_ref[vsl, :]]
        if with_latent:
            parts += [vtl_ref[vsl, c * KEY_CHUNK:(c + 1) * KEY_CHUNK] for c in range(n_lat // KEY_CHUNK)]
        return parts

    def attend(with_latent):
        _softmax_heads(n_heads, lambda h: scores(h, with_latent), lambda h: values(h, with_latent), o_ref)

    if first_step == 0:
        is_ctx = pl.program_id(1) == 0

        @pl.when(is_ctx)
        def _():
            attend(False)

        @pl.when(jnp.logical_not(is_ctx))
        def _():
            attend(True)
    else:
        attend(True)


def _attention(qt, k, vt, rows, *, n_heads, group, dk, skip_ctx_queries):
    first = 1 if skip_ctx_queries else 0
    kern = functools.partial(_attn_kernel, n_heads=n_heads, group=group, dk=dk, first_step=first)
    wo = n_heads * HEAD_DIM
    return pl.pallas_call(
        kern,
        grid=(rows.batch, rows.n_lat // TQ + 1 - first),
        in_specs=[pl.BlockSpec((qt.shape[0], TQ), lambda b, i: (0, rows.query_block(b, i + first, TQ))),
                  pl.BlockSpec((N_CTX, k.shape[1]), lambda b, i: (b, 0)),
                  pl.BlockSpec((rows.n_lat, k.shape[1]), lambda b, i: (rows.latent_block(b), 0)),
                  pl.BlockSpec((vt.shape[0], N_CTX), lambda b, i: (0, b)),
                  pl.BlockSpec((vt.shape[0], rows.n_lat), lambda b, i: (0, rows.latent_block(b)))],
        out_specs=pl.BlockSpec((TQ, wo), lambda b, i: (rows.query_block(b, i + first, TQ), 0)),
        out_shape=jax.ShapeDtypeStruct((rows.n_rows, wo), BF16),
        compiler_params=_cparams(2),
        name="attention_dk%d" % dk,
    )(qt, k, k, vt, vt)


def _na_geometry(grid_rows):
    rb = NA_ROWS_PER_BLOCK
    wr = min(NA_WIN_R, grid_rows)
    assert grid_rows % 2 == 0 and GRID_W * 2 == 128
    span = min(grid_rows, wr + rb - 1 + (wr + rb - 1) % 2)
    starts, variants, keys = [], [], {}
    for r0 in range(0, grid_rows, rb):
        rs0 = int(np.clip(r0 - wr // 2, 0, grid_rows - wr))
        start = min(rs0, grid_rows - span)
        start -= start % 2
        rel = tuple(int(np.clip(r0 + dr - wr // 2, 0, grid_rows - wr)) - (r0 + dr) for dr in range(rb))
        key = (rel, start - r0)
        variants.append(keys.setdefault(key, len(keys)))
        starts.append(start)
    firsts = [variants.index(v) for v in range(len(keys))]
    return wr, span, starts, variants, firsts


def _na_bias_table(rpb, grid_rows):
    rb = NA_ROWS_PER_BLOCK
    wr, span, starts, variants, firsts = _na_geometry(grid_rows)
    nv = len(firsts)
    idx_r = np.zeros((nv, rb, span), np.int32)
    ok_r = np.zeros((nv, rb, span), bool)
    for v, blk in enumerate(firsts):
        r0, start = blk * rb, starts[blk]
        for dr in range(rb):
            r = r0 + dr
            rs = int(np.clip(r - wr // 2, 0, grid_rows - wr))
            for j in range(span):
                kr = start + j
                ok_r[v, dr, j] = rs <= kr < rs + wr
                idx_r[v, dr, j] = np.clip(kr - r + NA_WIN_R - 1, 0, 2 * NA_WIN_R - 2)
    c = np.arange(GRID_W)
    cs = np.clip(c - NA_WIN_C // 2, 0, GRID_W - NA_WIN_C)
    kc = np.arange(GRID_W)
    ok_c = (kc[None, :] >= cs[:, None]) & (kc[None, :] < cs[:, None] + NA_WIN_C)
    idx_c = np.clip(kc[None, :] - c[:, None] + NA_WIN_C - 1, 0, 2 * NA_WIN_C - 2)
    n_rel_c = 2 * NA_WIN_C - 1
    picked = rpb[:, idx_r.reshape(-1)].reshape(rpb.shape[0], nv, rb, span, n_rel_c)
    picked = jnp.transpose(picked, (1, 0, 3, 2, 4)).reshape(nv, rpb.shape[0], span, rb * n_rel_c) * LOG2E
    col_hit = idx_c.T[None, :, :] == np.arange(n_rel_c)[:, None, None]
    spread = (np.eye(rb, dtype=bool)[:, None, None, :, None] & col_hit[None, :, :, None, :])
    spread = jnp.asarray(spread.reshape(rb * n_rel_c, GRID_W, rb * GRID_W), F32)
    bias = jnp.einsum("vhjy,ykq->vhjkq", picked, spread, precision=lax.Precision.HIGHEST)
    ok = ok_r[:, None, :, None, :, None] & ok_c[None, None, None, :, None, :]
    ok = np.transpose(ok, (0, 1, 4, 5, 2, 3)).reshape(nv, 1, span, GRID_W, rb * GRID_W)
    bias = jnp.where(ok, bias, MASK_VALUE)
    return bias.reshape(nv, rpb.shape[0], span * GRID_W, rb * GRID_W)


def _na_kernel(start_ref, var_ref, qt_ref, kc_ref, kl_ref, vtc_ref, vtl_ref, bias_ref, o_ref, *, n_loc, first_step):
    i = pl.program_id(1) + first_step
    tq = qt_ref.shape[1]

    def padded_query(h):
        qt = qt_ref[h * HEAD_DIM:(h + 1) * HEAD_DIM, :]
        zeros = jnp.zeros((HEAD_DIM, tq), qt.dtype)
        return jnp.concatenate([qt, zeros] if h % 2 == 0 else [zeros, qt], axis=0)

    def pair_lanes(h):
        return slice((h // 2) * 2 * HEAD_DIM, (h // 2 + 1) * 2 * HEAD_DIM)

    def value_rows(h):
        return slice(h * V_EXT, (h + 1) * V_EXT)

    def context_queries():
        _softmax_heads(NA_HEADS, lambda h: [_dot(kc_ref[:, pair_lanes(h)], padded_query(h))],
                       lambda h: [vtc_ref[value_rows(h), :]], o_ref)

    def latent_queries():
        off = pl.multiple_of(start_ref[i - 1] * GRID_W, 2 * GRID_W)
        var = var_ref[i - 1]

        def scores(h):
            qt = padded_query(h)
            return [_dot(kc_ref[:, pair_lanes(h)], qt),
                    _dot(kl_ref[pl.ds(off, n_loc), pair_lanes(h)], qt) + bias_ref[var, h]]

        _softmax_heads(NA_HEADS, scores,
                       lambda h: [vtc_ref[value_rows(h), :], vtl_ref[value_rows(h), pl.ds(off, n_loc)]], o_ref)

    if first_step == 0:
        pl.when(i == 0)(context_queries)
        pl.when(i > 0)(latent_queries)
    else:
        latent_queries()


def _neighbourhood_attention(qt, k, vt, bias, rows, skip_ctx_queries):
    grid_rows = rows.n_lat // GRID_W
    _, span, starts, variants, _ = _na_geometry(grid_rows)
    assert NA_ROWS_PER_BLOCK * GRID_W == TQ
    first = 1 if skip_ctx_queries else 0
    kern = functools.partial(_na_kernel, n_loc=span * GRID_W, first_step=first)
    qblock = lambda b, i: rows.query_block(b, i + first, TQ)
    v_w = vt.shape[0]
    grid_spec = pltpu.PrefetchScalarGridSpec(
        num_scalar_prefetch=2,
        grid=(rows.batch, rows.n_lat // TQ + 1 - first),
        in_specs=[pl.BlockSpec((NA_W, TQ), lambda b, i, *_: (0, qblock(b, i))),
                  pl.BlockSpec((N_CTX, NA_W), lambda b, i, *_: (b, 0)),
                  pl.BlockSpec((rows.n_lat, NA_W), lambda b, i, *_: (rows.latent_block(b), 0)),
                  pl.BlockSpec((v_w, N_CTX), lambda b, i, *_: (0, b)),
                  pl.BlockSpec((v_w, rows.n_lat), lambda b, i, *_: (0, rows.latent_block(b))),
                  pl.BlockSpec(bias.shape, lambda b, i, *_: (0, 0, 0, 0), pipeline_mode=pl.Buffered(1))],
        out_specs=pl.BlockSpec((TQ, NA_W), lambda b, i, *_: (qblock(b, i), 0)),
    )
    return pl.pallas_call(
        kern,
        grid_spec=grid_spec,
        out_shape=jax.ShapeDtypeStruct((rows.n_rows, NA_W), BF16),
        compiler_params=_cparams(2),
        name="neighbourhood_attention",
    )(jnp.asarray(starts, jnp.int32), jnp.asarray(variants, jnp.int32), qt, k, k, vt, vt, bias)


def _s5_matrices(lam_re, lam_im, log_dt, b_re, b_im, c_re, c_im, d_skip):
    L, G, P, N = SSM_CHUNK, SSM_GROUPS, SSM_GROUP, SSM_STATE
    hi = lax.Precision.HIGHEST
    dt = jnp.exp(log_dt)[..., None]
    pw = jnp.arange(L + 1, dtype=F32)[:, None, None, None]
    mag = jnp.exp(lam_re * dt * pw)
    e_re = mag * jnp.cos(lam_im * dt * pw)
    e_im = mag * jnp.sin(lam_im * dt * pw)
    a_re, a_im = e_re[1], e_im[1]
    den = jnp.square(lam_re) + jnp.square(lam_im)
    f_re = ((a_re - 1.0) * lam_re + a_im * lam_im) / den
    f_im = (a_im * lam_re - (a_re - 1.0) * lam_im) / den
    bb_re = f_re[..., None] * b_re - f_im[..., None] * b_im
    bb_im = f_re[..., None] * b_im + f_im[..., None] * b_re
    ce_re = c_re[None] * e_re[:, :, :, None, :] - c_im[None] * e_im[:, :, :, None, :]
    ce_im = c_re[None] * e_im[:, :, :, None, :] + c_im[None] * e_re[:, :, :, None, :]
    kk = (jnp.einsum("kdgqn,dgnp->kdgqp", ce_re, bb_re, precision=hi)
          - jnp.einsum("kdgqn,dgnp->kdgqp", ce_im, bb_im, precision=hi))
    lag = np.arange(L)[None, :] - np.arange(L)[:, None]
    kf = kk[np.clip(lag, 0, L - 1), 0]
    kb = kk[np.clip(-lag, 0, L - 1), 1]
    skip = jnp.eye(P, dtype=F32)[None] * d_skip.reshape(G, P)[:, None, :]
    m = lambda cond: jnp.asarray(cond, F32)[:, :, None, None, None]
    kst = m(lag >= 0) * kf + m(lag <= 0) * kb + m(lag == 0) * skip[None, None]
    a_t = jnp.transpose(kst, (0, 2, 4, 1, 3)).reshape(L * G * P, L * P)

    def state_in(pows, d):
        x_re = e_re[pows, d][..., None] * bb_re[d][None] - e_im[pows, d][..., None] * bb_im[d][None]
        x_im = e_re[pows, d][..., None] * bb_im[d][None] + e_im[pows, d][..., None] * bb_re[d][None]
        flat = lambda x: jnp.pad(jnp.transpose(x, (0, 1, 3, 2)).reshape(L * G * P, N), ((0, 0), (0, L * P - N)))
        return [flat(x_re), flat(x_im)]

    a_w1 = jnp.concatenate([a_t] + state_in(np.arange(L)[::-1].copy(), 0) + state_in(np.arange(L), 1), axis=1)
    w1 = _expand_group_blocks(a_w1[None], L * G * P + 4 * G * N, row_shift=4)

    def state_out(x):
        return jnp.transpose(x, (1, 3, 0, 2)).reshape(G * N, L * P)

    pf, pb = np.arange(1, L + 1), np.arange(L, 0, -1)
    a_wc = jnp.stack([jnp.concatenate([state_out(ce_re[pf, 0]), state_out(ce_re[pb, 1])], axis=0),
                      jnp.concatenate([state_out(-ce_im[pf, 0]), state_out(-ce_im[pb, 1])], axis=0)])
    wc = _expand_group_blocks(a_wc, L * G * P, row_shift=6)
    al_re = e_re[L].reshape(2, 1, G * N)
    al_im = e_im[L].reshape(2, 1, G * N)
    return w1[0], wc[0], wc[1], al_re, al_im


def _expand_kernel(a_ref, o_ref, *, row_shift, n_response_tiles):
    j = pl.program_id(1)
    a = a_ref[0].astype(BF16)
    n_rows, tn = o_ref.shape[1], o_ref.shape[2]
    src = lax.broadcasted_iota(jnp.int32, (a.shape[1], tn), 0)
    col = lax.broadcasted_iota(jnp.int32, (a.shape[1], tn), 1)
    row_group = (lax.broadcasted_iota(jnp.int32, (n_rows, tn), 0) >> row_shift) & (SSM_GROUPS - 1)
    out_col = lax.broadcasted_iota(jnp.int32, (n_rows, tn), 1)

    def emit(spread, col_group):
        val = _dot(a, jnp.where(spread, 1.0, 0.0).astype(BF16))
        o_ref[0] = jnp.where(row_group == col_group, val, 0.0).astype(o_ref.dtype)

    @pl.when(j < n_response_tiles)
    def _():
        cg = col + j * tn
        spread = ((src >> 4) == (cg >> 8)) & ((src & 15) == (cg & 15))
        emit(spread, ((out_col + j * tn) >> 4) & (SSM_GROUPS - 1))

    @pl.when(j >= n_response_tiles)
    def _():
        spread = src == (col & (SSM_STATE - 1))
        emit(spread, (out_col >> 6) & (SSM_GROUPS - 1))


def _expand_group_blocks(a, n_cols, *, row_shift):
    k, n_rows, _ = a.shape
    tn = 1024
    n_response_tiles = SSM_CHUNK * SSM_WIDTH // tn
    assert SSM_GROUP == 16 and SSM_STATE == 64 and SSM_CHUNK * SSM_GROUP == 128 and tn % (SSM_GROUPS * SSM_STATE) == 0
    kern = functools.partial(_expand_kernel, row_shift=row_shift, n_response_tiles=n_response_tiles)
    return pl.pallas_call(
        kern,
        grid=(k, n_cols // tn),
        in_specs=[pl.BlockSpec((1, n_rows, 128), lambda d, j: (d, 0, jnp.maximum(j - n_response_tiles + 1, 0)))],
        out_specs=pl.BlockSpec((1, n_rows, tn), lambda d, j: (d, 0, j)),
        out_shape=jax.ShapeDtypeStruct((k, n_rows, n_cols), BF16),
        compiler_params=_cparams(2),
        name="s5_expand",
    )(a)


def _mm_kernel(a_ref, b_ref, o_ref):
    o_ref[...] = _dot(a_ref[...], b_ref[...]).astype(o_ref.dtype)


def _matmul(a, b, tm, tn, out_dtype):
    m, kdim = a.shape
    n = b.shape[1]
    return pl.pallas_call(
        _mm_kernel,
        grid=(n // tn, m // tm),
        in_specs=[pl.BlockSpec((tm, kdim), lambda j, i: (i, 0)),
                  pl.BlockSpec((kdim, tn), lambda j, i: (0, j))],
        out_specs=pl.BlockSpec((tm, tn), lambda j, i: (i, j)),
        out_shape=jax.ShapeDtypeStruct((m, n), out_dtype),
        compiler_params=_cparams(2),
        name="s5_chunk_matmul",
    )(a, b)


def _s5_carry_kernel(sre_ref, sim_ref, are_ref, aim_ref, hre_ref, him_ref, *, batch, ctx_chunks, lat_chunks):
    backward = pl.program_id(0) == 1
    shape = (batch, sre_ref.shape[1])
    a_re = jnp.broadcast_to(are_ref[0], shape)
    a_im = jnp.broadcast_to(aim_ref[0], shape)

    def phase(n, base, carry):
        def step(j, carry):
            h_re, h_im = carry
            c = jnp.where(backward, n - 1 - j, j)
            rows = pl.ds(base + c, batch, stride=n)
            hre_ref[rows, :] = h_re
            him_ref[rows, :] = h_im
            return (a_re * h_re - a_im * h_im + sre_ref[rows, :],
                    a_re * h_im + a_im * h_re + sim_ref[rows, :])

        return lax.fori_loop(0, n, step, carry)

    zero = jnp.zeros(shape, F32)
    carry = phase(ctx_chunks, 0, (zero, zero))
    phase(lat_chunks, batch * ctx_chunks, carry)


def _s5_carry(ys, al_re, al_im, batch, ctx_chunks, lat_chunks):
    gn = SSM_GROUPS * SSM_STATE
    tw = 128
    base = SSM_CHUNK * SSM_WIDTH // tw
    per_dir = 2 * gn // tw
    n_rows = ys.shape[0]
    kern = functools.partial(_s5_carry_kernel, batch=batch, ctx_chunks=ctx_chunks, lat_chunks=lat_chunks)
    blk = (n_rows, tw)
    return pl.pallas_call(
        kern,
        grid=(2, gn // tw),
        in_specs=[pl.BlockSpec(blk, lambda d, j: (0, base + d * per_dir + j)),
                  pl.BlockSpec(blk, lambda d, j: (0, base + d * per_dir + gn // tw + j)),
                  pl.BlockSpec((1, 1, tw), lambda d, j: (d, 0, j)),
                  pl.BlockSpec((1, 1, tw), lambda d, j: (d, 0, j))],
        out_specs=[pl.BlockSpec(blk, lambda d, j: (0, d * (gn // tw) + j))] * 2,
        out_shape=[jax.ShapeDtypeStruct((n_rows, 2 * gn), F32)] * 2,
        compiler_params=_cparams(2),
        name="s5_carry",
    )(ys, ys, al_re, al_im)


def _s5_readout_kernel(hre_ref, him_ref, wre_ref, wim_ref, y_ref, o_ref):
    o_ref[...] = (y_ref[...] + _dot(hre_ref[...].astype(BF16), wre_ref[...])
                  + _dot(him_ref[...].astype(BF16), wim_ref[...]))


def _s5_readout(h_re, h_im, wc_re, wc_im, ys, tm):
    m, kdim = h_re.shape
    n = wc_re.shape[1]
    tn = 1024
    return pl.pallas_call(
        _s5_readout_kernel,
        grid=(n // tn, m // tm),
        in_specs=[pl.BlockSpec((tm, kdim), lambda j, i: (i, 0)),
                  pl.BlockSpec((tm, kdim), lambda j, i: (i, 0)),
                  pl.BlockSpec((kdim, tn), lambda j, i: (0, j)),
                  pl.BlockSpec((kdim, tn), lambda j, i: (0, j)),
                  pl.BlockSpec((tm, tn), lambda j, i: (i, j))],
        out_specs=pl.BlockSpec((tm, tn), lambda j, i: (i, j)),
        out_shape=jax.ShapeDtypeStruct((m, n), F32),
        compiler_params=_cparams(2),
        name="s5_readout",
    )(h_re, h_im, wc_re, wc_im, ys)


def _row_tile(m, target=512):
    t = min(m, target)
    while m % t or t % 16:
        t -= 16
    return t


def _s5_mixer(u, mats, rows):
    w1, wc_re, wc_im, al_re, al_im = mats
    tm = _row_tile(u.shape[0])
    ys = _matmul(u, w1, tm, 1024, F32)
    h_re, h_im = _s5_carry(ys, al_re, al_im, rows.batch, N_CTX // SSM_CHUNK, rows.n_lat // SSM_CHUNK)
    return _s5_readout(h_re, h_im, wc_re, wc_im, ys, tm)


def _gelu_tanh(y):
    return 0.5 * y * (1.0 + jnp.tanh(math.sqrt(2.0 / math.pi) * (y + 0.044715 * (y * y * y))))


def _out_mlp_kernel(*refs, even, ctx_blocks):
    read_x, refs = _stream_reader(refs, ctx_blocks)
    if even:
        m1_ref, m2_ref, mod_ref, perm_ref, wglu_ref, bglu_ref, wo_ref, g2_ref, w1_ref, w2_ref, o_ref = refs
        yc = m2_ref[...]
        ys = jnp.concatenate([yc[:, s * SSM_WIDTH:(s + 1) * SSM_WIDTH] for s in range(SSM_CHUNK)], axis=0)
        hi = ys.astype(BF16)
        lo = (ys - hi.astype(F32)).astype(BF16)
        y = _gelu_tanh(_dot(perm_ref[...], hi) + _dot(perm_ref[...], lo))
        z = _dot(y.astype(BF16), wglu_ref[...]) + bglu_ref[...]
        second = (y * jax.nn.sigmoid(z)).astype(BF16)
    else:
        m1_ref, m2_ref, mod_ref, wo_ref, g2_ref, w1_ref, w2_ref, o_ref = refs
        second = m2_ref[...]
    m = mod_ref[0]
    mod = lambda j: m[:, j * D_MODEL:(j + 1) * D_MODEL]
    mix = jnp.concatenate([m1_ref[...], second], axis=1)
    x1 = read_x(slice(None)) + mod(2) * _dot(mix, wo_ref[...])
    a = _modulate(x1, g2_ref[...], mod(3), mod(4)).astype(BF16)
    acc = None
    ck = 1024
    for c in range(D_FF // ck):
        h = jnp.maximum(_dot(a, w1_ref[:, c * ck:(c + 1) * ck]), 0.0)
        part = _dot((h * h).astype(BF16), w2_ref[c * ck:(c + 1) * ck, :])
        acc = part if acc is None else acc + part
    o_ref[...] = x1 + mod(5) * acc


def _out_mlp(x, mix1, mix2, modtab, consts, rows, even, latent_only):
    first = rows.ctx_blocks if latent_only else 0
    row_spec = lambda w: pl.BlockSpec((TM, w), lambda i: (i + first, 0))
    n_blocks = rows.n_rows // TM - first
    mix2_rows = TM // SSM_CHUNK if even else TM
    if isinstance(x, tuple):
        assert not latent_only
        x_parts, x_specs, ctx_blocks = _stream_specs(x, rows)
    else:
        x_parts, x_specs, ctx_blocks = (x,), [row_spec(D_MODEL)], None
    return pl.pallas_call(
        functools.partial(_out_mlp_kernel, even=even, ctx_blocks=ctx_blocks),
        grid=(n_blocks,),
        in_specs=x_specs + [row_spec(mix1.shape[1]),
                  pl.BlockSpec((mix2_rows, mix2.shape[1]), lambda i: (i + first, 0)),
                  pl.BlockSpec((1, 1, N_MOD * D_MODEL), lambda i: (rows.mod_index(i + first), 0, 0))]
                 + [_resident(c.shape) for c in consts],
        out_specs=pl.BlockSpec((TM, D_MODEL), lambda i: (i, 0)),
        out_shape=jax.ShapeDtypeStruct((n_blocks * TM, D_MODEL), F32),
        compiler_params=_cparams(1),
        name="out_mlp_even" if even else "out_mlp_odd",
    )(*x_parts, mix1, mix2, modtab, *consts)


def _rope_tables(n_lat, rot_dim, head_w, lane_off):
    t = jnp.arange(n_lat)
    grid_r = (t // GRID_W).astype(F32)
    grid_c = (t % GRID_W).astype(F32)
    axis_dim = rot_dim // 2
    freqs = ROPE_BASE ** (-jnp.arange(0, axis_dim, 2, dtype=F32) / axis_dim)
    ang_r = grid_r[:, None] * freqs
    ang_c = grid_c[:, None] * freqs
    ang = jnp.concatenate([ang_r, ang_r, ang_c, ang_c], axis=-1)
    cos, sin = jnp.cos(ang), jnp.sin(ang)
    quarter = rot_dim // 4
    first = (np.arange(rot_dim) // quarter) % 2 == 0
    sin_a = jnp.where(first, -sin, 0.0)
    sin_b = jnp.where(first, 0.0, sin)

    def widen(tab, fill):
        full = jnp.full((n_lat, head_w), fill, F32).at[:, lane_off:lane_off + rot_dim].set(tab)
        full = jnp.concatenate([jnp.full((TM, head_w), fill, F32), full], axis=0)
        return jnp.tile(full, (1, 128 // head_w))

    return widen(cos, 1.0), widen(sin_a, 0.0), widen(sin_b, 0.0)


def _with_gain(tabs, gain, shift):
    cos, sin_a, sin_b = tabs
    return cos * gain, sin_a * jnp.roll(gain, -shift), sin_b * jnp.roll(gain, shift)


def _head_sum_matrices(n_heads, head_dim):
    down = np.kron(np.eye(n_heads), np.ones((head_dim, 1)))
    pad = (-n_heads) % 128
    down = np.pad(down, ((0, 0), (0, pad)))
    return jnp.asarray(down, BF16), jnp.asarray(down.T, BF16)


def _pad_heads(w, n_heads, width, pad_to):
    lead = w.shape[:-1]
    w = w.reshape(lead + (n_heads, width))
    w = jnp.pad(w, [(0, 0)] * len(lead) + [(0, 0), (0, pad_to - width)])
    return w.reshape(lead + (n_heads * pad_to,))


def kernel(x, c, ctx, c_ctx, w_mod, b_mod, g_norm1, g_norm2, w_ff1, w_ff2, e_w_in, e_w_out, e_g_q, e_g_k, ssm_lam_re, ssm_lam_im, ssm_log_dt, ssm_b_re, ssm_b_im, ssm_c_re, ssm_c_im, ssm_d, ssm_w_glu, ssm_b_glu, o_w_in, o_w_out, mla_g_cq, mla_g_ckv, mla_w_uq, mla_w_ukv, mla_g_q, mla_g_k, na_g_q, na_g_k, na_rpb):
    batch, n_lat, d = x.shape
    depth = w_mod.shape[0]
    assert d == D_MODEL and ctx.shape[1] == N_CTX and n_lat % GRID_W == 0
    rows = _Rows(batch, n_lat)

    pad_rows = (-(batch + 1)) % 8
    cond = jnp.concatenate([c_ctx[None], c, jnp.zeros((pad_rows, d), F32)], axis=0)
    modtabs = _mod_vectors(cond, w_mod, b_mod).reshape(depth, -1, 1, N_MOD * d)

    xs = (ctx.reshape(rows.n_ctx_rows, d), x.reshape(batch * n_lat, d))

    even_tabs = _rope_tables(n_lat, HEAD_DIM, HEAD_DIM, 0)
    mla_tabs = _rope_tables(n_lat, MLA_ROPE, MLA_PAD, MLA_NOPE)
    down_e, up_e = _head_sum_matrices(GQA_Q_HEADS + GQA_KV_HEADS, HEAD_DIM)
    down_o, up_o = _head_sum_matrices(2 * NA_HEADS, HEAD_DIM)
    row = lambda v: v.reshape(1, -1)

    def chunk_perm(n):
        tok = np.arange(n)
        mat = np.zeros((n, n), np.float32)
        mat[(tok % SSM_CHUNK) * (n // SSM_CHUNK) + tok // SSM_CHUNK, tok] = 1.0
        return mat

    perm = jnp.asarray(chunk_perm(EVEN_SUB_ROWS), BF16)
    perm_t = jnp.asarray(chunk_perm(TM).T, BF16)

    for i in range(depth):
        j = i // 2
        last = i == depth - 1
        g1, g2 = row(g_norm1[i]), row(g_norm2[i])
        w1, w2 = w_ff1[i].astype(BF16), w_ff2[i].astype(BF16)
        if i % 2 == 0:
            lanes = 128 // HEAD_DIM
            tabs = (_with_gain(even_tabs, jnp.tile(e_g_q[j], lanes) * (HEAD_DIM ** -0.5 * LOG2E), HEAD_DIM // 4)
                    + _with_gain(even_tabs, jnp.tile(e_g_k[j], lanes), HEAD_DIM // 4))
            q, k, v, u = _even_in_proj(xs, modtabs[i], (g1, e_w_in[j].astype(BF16), down_e, up_e, perm), tabs, rows)
            att = _attention(q, k, v, rows, n_heads=GQA_Q_HEADS, group=GQA_GROUP, dk=HEAD_DIM, skip_ctx_queries=last)
            mats = _s5_matrices(ssm_lam_re[j], ssm_lam_im[j], ssm_log_dt[j], ssm_b_re[j], ssm_b_im[j],
                                ssm_c_re[j], ssm_c_im[j], ssm_d[j])
            y = _s5_mixer(u, mats, rows)
            consts = (perm_t, ssm_w_glu[j].astype(BF16), row(ssm_b_glu[j]), e_w_out[j].astype(BF16), g2, w1, w2)
            xs = _out_mlp(xs, att, y, modtabs[i], consts, rows, True, last)
        else:
            w_in = o_w_in[j]
            c1 = MLA_Q_RANK
            c2 = c1 + MLA_KV_RANK
            c3 = c2 + MLA_ROPE
            kr_cols = jnp.pad(w_in[:, c2:c3], ((0, 0), (MLA_NOPE, MLA_PAD - MLA_QK)))
            w_in_p = jnp.concatenate([w_in[:, :c2], kr_cols, w_in[:, c3:]], axis=1).astype(BF16)
            wuq = _pad_heads(mla_w_uq[j], MLA_HEADS, MLA_QK, MLA_PAD).astype(BF16)
            wukv = mla_w_ukv[j].reshape(MLA_KV_RANK, MLA_HEADS, MLA_NOPE + MLA_V)
            wuk = _pad_heads(wukv[:, :, :MLA_NOPE].reshape(MLA_KV_RANK, -1), MLA_HEADS, MLA_NOPE, MLA_PAD).astype(BF16)
            wuv = wukv[:, :, MLA_NOPE:].reshape(MLA_KV_RANK, -1).astype(BF16)
            gmq = _pad_heads(jnp.tile(mla_g_q[j], MLA_HEADS) * (MLA_QK ** -0.5 * LOG2E), MLA_HEADS, MLA_QK, MLA_PAD)
            gmk = _pad_heads(jnp.tile(mla_g_k[j], MLA_HEADS), MLA_HEADS, MLA_QK, MLA_PAD)
            gnqk = jnp.concatenate([jnp.tile(na_g_q[j], NA_HEADS) * (HEAD_DIM ** -0.5 * LOG2E),
                                    jnp.tile(na_g_k[j], NA_HEADS)])
            consts = (g1, w_in_p, row(mla_g_cq[j]), row(mla_g_ckv[j]), wuq, wuk, wuv, row(gmq), row(gmk),
                      down_o, up_o, row(gnqk))
            mq, mk, mv, nq, nk, nv = _odd_in_proj(xs, modtabs[i], consts, mla_tabs, rows)
            mla = _attention(mq, mk, mv, rows, n_heads=MLA_HEADS, group=1, dk=MLA_PAD, skip_ctx_queries=last)
            bias = _na_bias_table(na_rpb[j], n_lat // GRID_W)
            na = _neighbourhood_attention(nq, nk, nv, bias, rows, last)
            consts = (o_w_out[j].astype(BF16), g2, w1, w2)
            xs = _out_mlp(xs, mla, na, modtabs[i], consts, rows, False, last)
    return xs.reshape(batch, n_lat, d)
```
